```python
import math
import jax
import jax.numpy as jnp
from jax import lax
import numpy as np

D_MODEL = 1024
BATCH = 8
SEQ = 2048
DEPTH = 4

GRID_W = 64
CTX_LEN = 256
N_MOD = 9
FFN_HIDDEN = 2816
EPS = 1e-6
ROPE_THETA = 10000.0
Q_BLOCK = 128
GROUP_W = D_MODEL // 4
D_MIX = 4 * GROUP_W
HEAD_DIM = 64
ML_HEADS = GROUP_W // HEAD_DIM
ML_HEAD_DIM = HEAD_DIM
ML_CHUNK = 64
ML_CONV = 3
ML_N_GATES = 4 * ML_HEADS
GQA_HEADS = GROUP_W // HEAD_DIM
GQA_KV_HEADS = GQA_HEADS // 2
HY_CH = GROUP_W
HY_ORDER = 2
HY_SHORT = 3
HY_POS_BANDS = 16
HY_POS_DIM = 1 + 2 * HY_POS_BANDS
HY_FILT_HIDDEN = 64
HY_DECAY_TARGET = 1e-2
HY_FAST_DECAY = 0.3
HY_SLOW_DECAY = 1.5
DIFF_HEADS = GROUP_W // HEAD_DIM
DIFF_SUB_DIM = HEAD_DIM // 2
N_IN_ML = 4 * GROUP_W + ML_N_GATES
N_IN_GQA = (GQA_HEADS + 2 * GQA_KV_HEADS) * HEAD_DIM
N_IN_HY = (HY_ORDER + 1) * HY_CH
N_IN_DIFF = 3 * GROUP_W
N_IN = N_IN_ML + N_IN_GQA + N_IN_HY + N_IN_DIFF
IN_SPLITS = [N_IN_ML, N_IN_ML + N_IN_GQA, N_IN_ML + N_IN_GQA + N_IN_HY]

kernel_name = 'hybrid_mlstm_gqa_hyena_diffattn_dit_trunk'


def _rmsnorm(x, g):
    xf = x.astype(jnp.float32)
    y = xf * lax.rsqrt(jnp.mean(xf * xf, axis=-1, keepdims=True) + EPS)
    return (y * g.astype(jnp.float32)).astype(x.dtype)


def _modnorm(x, g, shift, scale):
    return _rmsnorm(x, g) * (1 + scale) + shift


def _swiglu(x, w13, w2):
    a, b = jnp.split(x @ w13, 2, axis=-1)
    return (jax.nn.silu(a) * b) @ w2


def _ffn_half(x, g, shift, scale, gate, w13, w2):
    return x + 0.5 * gate * _swiglu(_modnorm(x, g, shift, scale), w13, w2)


def _split_heads(x, n_heads):
    B, L, _ = x.shape
    return x.reshape(B, L, n_heads, -1).transpose(0, 2, 1, 3)


def _merge_heads(x):
    B, H, L, d = x.shape
    return x.transpose(0, 2, 1, 3).reshape(B, L, H * d)


def _dwconv_centred(x, w, b):
    C, K = x.shape[-1], w.shape[0]
    y = lax.conv_general_dilated(x, w[:, None, :].astype(x.dtype), window_strides=(1,),
                                 padding=[(K // 2, K // 2)],
                                 dimension_numbers=('NWC', 'WIO', 'NWC'), feature_group_count=C)
    return y + b


def _axial_rope_tables(length, dim):
    rows = length // GRID_W
    row = jnp.repeat(jnp.arange(rows), GRID_W).astype(jnp.float32)
    col = jnp.tile(jnp.arange(GRID_W), rows).astype(jnp.float32)
    n_freq = dim // 4
    inv = ROPE_THETA ** (-jnp.arange(n_freq, dtype=jnp.float32) / n_freq)
    ang = jnp.concatenate([row[:, None] * inv, col[:, None] * inv], axis=-1)
    return jnp.cos(ang), jnp.sin(ang)


def _rope(x, rope):
    cos, sin = rope
    x1, x2 = jnp.split(x, 2, axis=-1)
    return jnp.concatenate([x1 * cos - x2 * sin, x1 * sin + x2 * cos], axis=-1).astype(x.dtype)


def _sweep(fn, *qs):
    B, H, L, _ = qs[0].shape
    nb = L // Q_BLOCK
    blocks = tuple(q.reshape(B, H, nb, Q_BLOCK, q.shape[-1]).transpose(2, 0, 1, 3, 4) for q in qs)
    out = lax.map(lambda bq: fn(*bq), blocks)
    return out.transpose(1, 2, 0, 3, 4).reshape(B, H, L, out.shape[-1])


def _mlstm_scan(q, k, v, log_i, log_f, state):
    B, H, L, d = q.shape
    nc = L // ML_CHUNK

    def to_chunks(t):
        return jnp.moveaxis(t.reshape(B, H, nc, ML_CHUNK, *t.shape[3:]), 2, 0)

    tril = jnp.tril(jnp.ones((ML_CHUNK, ML_CHUNK), dtype=bool))

    def step(carry, xs):
        C, n, m = carry
        qc, kc, vc, ic, fc = xs
        b = jnp.cumsum(fc, axis=-1)
        log_w = jnp.where(tril, b[..., :, None] - b[..., None, :] + ic[..., None, :], -jnp.inf)
        log_inter = b + m[..., None]
        m_t = jnp.maximum(log_inter, jnp.max(log_w, axis=-1))
        w_intra = jnp.exp(log_w - m_t[..., None])
        w_inter = jnp.exp(log_inter - m_t)
        s = jnp.einsum('bhtd,bhsd->bhts', qc, kc) * w_intra
        num = (w_inter[..., None] * jnp.einsum('bhde,bhte->bhtd', C, qc)
               + jnp.einsum('bhts,bhsd->bhtd', s, vc))
        den = w_inter * jnp.einsum('bhd,bhtd->bht', n, qc) + jnp.sum(s, axis=-1)
        h = num / jnp.maximum(jnp.abs(den), jnp.exp(-m_t))[..., None]
        b_end = b[..., -1]
        log_g = b_end[..., None] - b + ic
        m_new = jnp.maximum(b_end + m, jnp.max(log_g, axis=-1))
        w_g = jnp.exp(log_g - m_new[..., None])
        w_c = jnp.exp(b_end + m - m_new)
        C = w_c[..., None, None] * C + jnp.einsum('bhs,bhsd,bhse->bhde', w_g, vc, kc)
        n = w_c[..., None] * n + jnp.einsum('bhs,bhsd->bhd', w_g, kc)
        return (C, n, m_new), h

    xs = tuple(to_chunks(t) for t in (q, k, v, log_i, log_f))
    state, hs = lax.scan(step, state, xs)
    return jnp.moveaxis(hs, 0, 2).reshape(B, H, L, d), state


def _mlstm_bidir(q, k, v, g, state_f, state_b):
    h_f, s_f = _mlstm_scan(q, k, v, g[0], jax.nn.log_sigmoid(g[1]), state_f)
    fl = lambda t: jnp.flip(t, axis=2)
    h_b, s_b = _mlstm_scan(fl(q), fl(k), fl(v), fl(g[2]), fl(jax.nn.log_sigmoid(g[3])), state_b)
    return h_f + fl(h_b), s_f, s_b


def _mlstm_mixer(pl, pc, conv_w, conv_b, gate_b, norm_g, need_ctx):
    def prep(p):
        B, L, _ = p.shape
        q, k, v, o, g = jnp.split(p, [GROUP_W, 2 * GROUP_W, 3 * GROUP_W, 4 * GROUP_W], axis=-1)
        qk = jax.nn.silu(_dwconv_centred(jnp.concatenate([q, k], axis=-1), conv_w, conv_b))
        q, k = jnp.split(qk, 2, axis=-1)
        g = (g.astype(jnp.float32) + gate_b).reshape(B, L, 4, ML_HEADS).transpose(2, 0, 3, 1)
        heads = lambda t: _split_heads(t, ML_HEADS).astype(jnp.float32)
        return heads(q) * ML_HEAD_DIM ** -0.5, heads(k), heads(v), o, g

    def out(h, o, dtype):
        hn = _rmsnorm(h, norm_g.reshape(ML_HEADS, 1, ML_HEAD_DIM))
        return (_merge_heads(hn) * jax.nn.sigmoid(o.astype(jnp.float32))).astype(dtype)

    qc, kc, vc, oc, gc = prep(pc)
    ql, kl, vl, ol, gl = prep(pl)
    B = pl.shape[0]
    zero = (jnp.zeros((B, ML_HEADS, ML_HEAD_DIM, ML_HEAD_DIM), jnp.float32),
            jnp.zeros((B, ML_HEADS, ML_HEAD_DIM), jnp.float32),
            jnp.zeros((B, ML_HEADS), jnp.float32))
    hc, s_f, s_b = _mlstm_bidir(qc, kc, vc, gc, zero, zero)
    hl, _, _ = _mlstm_bidir(ql, kl, vl, gl, s_f, s_b)
    yc = out(hc, oc, pc.dtype) if need_ctx else None
    return out(hl, ol, pl.dtype), yc


def _gqa_block(qb, k, v):
    B, Hq, Q, d = qb.shape
    Hkv = k.shape[1]
    qg = qb.reshape(B, Hkv, Hq // Hkv, Q, d)
    s = jnp.einsum('bhgqd,bhkd->bhgqk', qg, k, preferred_element_type=jnp.float32) * d ** -0.5
    p = jax.nn.softmax(s, axis=-1)
    o = jnp.einsum('bhgqk,bhkd->bhgqd', p.astype(v.dtype), v)
    return o.reshape(B, Hq, Q, d)


def _gqa_mixer(pl, pc, qk_g, rope, need_ctx):
    def prep(p):
        q, k, v = jnp.split(p, [GQA_HEADS * HEAD_DIM, (GQA_HEADS + GQA_KV_HEADS) * HEAD_DIM], axis=-1)
        return (_rmsnorm(_split_heads(q, GQA_HEADS), qk_g[0]),
                _rmsnorm(_split_heads(k, GQA_KV_HEADS), qk_g[1]),
                _split_heads(v, GQA_KV_HEADS))

    ql, kl, vl = prep(pl)
    qc, kc, vc = prep(pc)
    ql, kl = _rope(ql, rope), _rope(kl, rope)
    k_all = jnp.concatenate([kc, kl], axis=2)
    v_all = jnp.concatenate([vc, vl], axis=2)
    yl = _merge_heads(_sweep(lambda qb: _gqa_block(qb, k_all, v_all), ql))
    yc = _merge_heads(_sweep(lambda qb: _gqa_block(qb, kc, vc), qc)) if need_ctx else None
    return yl, yc


def _hyena_filters(length, w1, b1, w2, b2, w3, b3):
    t = jnp.arange(length, dtype=jnp.float32)
    tn = t / length
    bands = jnp.arange(1, HY_POS_BANDS + 1, dtype=jnp.float32)
    ang = 2.0 * math.pi * tn[:, None] * bands
    feats = jnp.concatenate([tn[:, None], jnp.cos(ang), jnp.sin(ang)], axis=-1)
    h = jnp.sin(feats @ w1.astype(jnp.float32) + b1.astype(jnp.float32))
    h = jnp.sin(h @ w2.astype(jnp.float32) + b2.astype(jnp.float32))
    h = h @ w3.astype(jnp.float32) + b3.astype(jnp.float32)
    dist = jnp.abs(t - length // 2) / (length / 2)
    deltas = jnp.abs(jnp.linspace(math.log(HY_DECAY_TARGET) / HY_SLOW_DECAY,
                                  math.log(HY_DECAY_TARGET) / HY_FAST_DECAY, HY_CH, dtype=jnp.float32))
    h = h * jnp.exp(-dist[:, None] * jnp.tile(deltas, HY_ORDER))
    return h / jnp.sum(jnp.abs(h), axis=0, keepdims=True)


def _fft_conv_centred(u, h, skip):
    L = u.shape[1]
    n = 2 * L
    y = jnp.fft.irfft(jnp.fft.rfft(u, n=n, axis=1) * jnp.fft.rfft(h, n=n, axis=0)[None], n=n, axis=1)
    return y[:, L // 2: L // 2 + L] + u * skip


def _hyena_seq(p, conv_w, conv_b, filt, skip):
    u = _dwconv_centred(p, conv_w, conv_b).astype(jnp.float32)
    v, x1, x2 = jnp.split(u, 3, axis=-1)
    h = _hyena_filters(p.shape[1], *filt)
    skip = skip.astype(jnp.float32)
    z = x1 * _fft_conv_centred(v, h[:, :HY_CH], skip[0])
    return (x2 * _fft_conv_centred(z, h[:, HY_CH:], skip[1])).astype(p.dtype)


def _diff_block(q1b, q2b, k1, k2, v, lam):
    scale = DIFF_SUB_DIM ** -0.5
    s1 = jnp.einsum('bhqd,bhkd->bhqk', q1b, k1, preferred_element_type=jnp.float32) * scale
    s2 = jnp.einsum('bhqd,bhkd->bhqk', q2b, k2, preferred_element_type=jnp.float32) * scale
    p = jax.nn.softmax(s1, axis=-1) - lam * jax.nn.softmax(s2, axis=-1)
    return jnp.einsum('bhqk,bhkd->bhqd', p.astype(v.dtype), v)


def _diff_mixer(pl, pc, qk_g, lam_p, subln_g, lam_init, rope, need_ctx):
    def prep(p):
        q, k, v = jnp.split(p, 3, axis=-1)
        q1, q2 = jnp.split(_split_heads(q, DIFF_HEADS), 2, axis=-1)
        k1, k2 = jnp.split(_split_heads(k, DIFF_HEADS), 2, axis=-1)
        return (_rmsnorm(q1, qk_g[0]), _rmsnorm(q2, qk_g[0]),
                _rmsnorm(k1, qk_g[1]), _rmsnorm(k2, qk_g[1]), _split_heads(v, DIFF_HEADS))

    lp = lam_p.astype(jnp.float32)
    lam = jnp.exp(jnp.sum(lp[0] * lp[1])) - jnp.exp(jnp.sum(lp[2] * lp[3])) + lam_init
    q1l, q2l, k1l, k2l, vl = prep(pl)
    q1l, q2l, k1l, k2l = (_rope(t, rope) for t in (q1l, q2l, k1l, k2l))
    q1c, q2c, k1c, k2c, vc = prep(pc)

    def out(q1, q2, k1, k2, v):
        y = _sweep(lambda a, b: _diff_block(a, b, k1, k2, v, lam), q1, q2)
        return _merge_heads(_rmsnorm(y, subln_g) * (1.0 - lam_init))

    cat = lambda a, b: jnp.concatenate([a, b], axis=2)
    yl = out(q1l, q2l, cat(k1c, k1l), cat(k2c, k2l), cat(vc, vl))
    yc = out(q1c, q2c, k1c, k2c, vc) if need_ctx else None
    return yl, yc


def setup_inputs(seed: int = 0) -> dict:
    key = jax.random.key(seed)
    it = iter(list(jax.random.split(key, 32)))
    nrm = lambda shape, scale: jax.random.normal(next(it), shape, jnp.float32) * scale
    f_bias = jnp.linspace(3.0, 6.0, ML_HEADS, dtype=jnp.float32)
    z = jnp.zeros((ML_HEADS,), jnp.float32)
    gate_struct = jnp.concatenate([z, f_bias, z, f_bias])
    return {
        'x': nrm((BATCH, SEQ, D_MODEL), 1.0),
        'c': nrm((BATCH, D_MODEL), 1.0),
        'ctx': nrm((BATCH, CTX_LEN, D_MODEL), 1.0),
        'c_ctx': nrm((D_MODEL,), 1.0),
        'ada_w': nrm((DEPTH, D_MODEL, N_MOD * D_MODEL), 0.5 * D_MODEL ** -0.5),
        'ada_b': nrm((DEPTH, N_MOD * D_MODEL), 0.01),
        'norm_g': 1.0 + nrm((DEPTH, 3, D_MODEL), 0.02),
        'ffn_w13': nrm((DEPTH, 2, D_MODEL, 2 * FFN_HIDDEN), D_MODEL ** -0.5),
        'ffn_w2': nrm((DEPTH, 2, FFN_HIDDEN, D_MODEL), FFN_HIDDEN ** -0.5),
        'w_in': nrm((DEPTH, D_MODEL, N_IN), D_MODEL ** -0.5),
        'w_out': nrm((DEPTH, D_MIX, D_MODEL), D_MIX ** -0.5),
        'ml_gate_b': gate_struct[None] + nrm((DEPTH, ML_N_GATES), 0.1),
        'ml_conv_w': nrm((DEPTH, ML_CONV, 2 * GROUP_W), ML_CONV ** -0.5),
        'ml_conv_b': nrm((DEPTH, 2 * GROUP_W), 0.02),
        'ml_norm_g': 1.0 + nrm((DEPTH, GROUP_W), 0.02),
        'gqa_qk_g': 1.0 + nrm((DEPTH, 2, HEAD_DIM), 0.02),
        'hy_conv_w': nrm((DEPTH, HY_SHORT, N_IN_HY), HY_SHORT ** -0.5),
        'hy_conv_b': nrm((DEPTH, N_IN_HY), 0.02),
        'hy_filt_w1': nrm((DEPTH, HY_POS_DIM, HY_FILT_HIDDEN), 1.0),
        'hy_filt_b1': nrm((DEPTH, HY_FILT_HIDDEN), 0.1),
        'hy_filt_w2': nrm((DEPTH, HY_FILT_HIDDEN, HY_FILT_HIDDEN), HY_FILT_HIDDEN ** -0.5),
        'hy_filt_b2': nrm((DEPTH, HY_FILT_HIDDEN), 0.1),
        'hy_filt_w3': nrm((DEPTH, HY_FILT_HIDDEN, HY_ORDER * HY_CH), HY_FILT_HIDDEN ** -0.5),
        'hy_filt_b3': nrm((DEPTH, HY_ORDER * HY_CH), 0.1),
        'hy_skip': nrm((DEPTH, HY_ORDER, HY_CH), 0.1),
        'diff_qk_g': 1.0 + nrm((DEPTH, 2, DIFF_SUB_DIM), 0.02),
        'diff_lambda': nrm((DEPTH, 4, DIFF_SUB_DIM), 0.1),
        'diff_subln_g': 1.0 + nrm((DEPTH, 2 * DIFF_SUB_DIM), 0.02),
    }


def reference(x, c, ctx, c_ctx, ada_w, ada_b, norm_g, ffn_w13, ffn_w2, w_in, w_out,
              ml_gate_b, ml_conv_w, ml_conv_b, ml_norm_g, gqa_qk_g,
              hy_conv_w, hy_conv_b, hy_filt_w1, hy_filt_b1, hy_filt_w2, hy_filt_b2, hy_filt_w3, hy_filt_b3,
              hy_skip, diff_qk_g, diff_lambda, diff_subln_g):
    B, L, D = x.shape
    rope_gqa = _axial_rope_tables(L, HEAD_DIM)
    rope_diff = _axial_rope_tables(L, DIFF_SUB_DIM)
    sc, scc = jax.nn.silu(c), jax.nn.silu(c_ctx)
    h, hc = x, ctx
    for l in range(DEPTH):
        need_ctx = l < DEPTH - 1
        mod_l = (sc @ ada_w[l] + ada_b[l]).reshape(B, 1, N_MOD, D)
        mod_c = (scc @ ada_w[l] + ada_b[l]).reshape(N_MOD, D)
        lam_init = 0.8 - 0.6 * math.exp(-0.3 * l)
        filt = (hy_filt_w1[l], hy_filt_b1[l], hy_filt_w2[l], hy_filt_b2[l], hy_filt_w3[l], hy_filt_b3[l])

        h = _ffn_half(h, norm_g[l, 0], mod_l[:, :, 0], mod_l[:, :, 1], mod_l[:, :, 2], ffn_w13[l, 0], ffn_w2[l, 0])
        hc = _ffn_half(hc, norm_g[l, 0], mod_c[0], mod_c[1], mod_c[2], ffn_w13[l, 0], ffn_w2[l, 0])

        pl = _modnorm(h, norm_g[l, 1], mod_l[:, :, 3], mod_l[:, :, 4]) @ w_in[l]
        pc = _modnorm(hc, norm_g[l, 1], mod_c[3], mod_c[4]) @ w_in[l]
        pl_ml, pl_gqa, pl_hy, pl_diff = jnp.split(pl, IN_SPLITS, axis=-1)
        pc_ml, pc_gqa, pc_hy, pc_diff = jnp.split(pc, IN_SPLITS, axis=-1)

        ym_l, ym_c = _mlstm_mixer(pl_ml, pc_ml, ml_conv_w[l], ml_conv_b[l], ml_gate_b[l], ml_norm_g[l], need_ctx)
        yg_l, yg_c = _gqa_mixer(pl_gqa, pc_gqa, gqa_qk_g[l], rope_gqa, need_ctx)
        yh_l = _hyena_seq(pl_hy, hy_conv_w[l], hy_conv_b[l], filt, hy_skip[l])
        yd_l, yd_c = _diff_mixer(pl_diff, pc_diff, diff_qk_g[l], diff_lambda[l], diff_subln_g[l],
                                 lam_init, rope_diff, need_ctx)

        h = h + mod_l[:, :, 5] * (jnp.concatenate([ym_l, yg_l, yh_l, yd_l], axis=-1) @ w_out[l])
        h = _ffn_half(h, norm_g[l, 2], mod_l[:, :, 6], mod_l[:, :, 7], mod_l[:, :, 8], ffn_w13[l, 1], ffn_w2[l, 1])

        if need_ctx:
            yh_c = _hyena_seq(pc_hy, hy_conv_w[l], hy_conv_b[l], filt, hy_skip[l])
            hc = hc + mod_c[5] * (jnp.concatenate([ym_c, yg_c, yh_c, yd_c], axis=-1) @ w_out[l])
            hc = _ffn_half(hc, norm_g[l, 2], mod_c[6], mod_c[7], mod_c[8], ffn_w13[l, 1], ffn_w2[l, 1])
    return h
```

```python
import functools
import math

import jax
import jax.numpy as jnp
from jax import lax
from jax.experimental import pallas as pl
from jax.experimental.pallas import tpu as pltpu

F32 = jnp.float32
BF16 = jnp.bfloat16

EPS = 1e-6
ROPE_THETA = 10000.0
GRID_W = 64
N_MOD = 9
GROUP_W = 256
HEAD_DIM = 64
N_HEADS = 4
DIFF_SUB = 32
ML_CHUNK = 256
HY_POS_BANDS = 16
HY_DECAY_TARGET = 1e-2
HY_FAST_DECAY = 0.3
HY_SLOW_DECAY = 1.5
ROW_TILE = 512
VMEM_LIMIT = 56 * 1024 * 1024


def _cparams(*sem):
    return pltpu.CompilerParams(dimension_semantics=sem, vmem_limit_bytes=VMEM_LIMIT)


def _resident(shape):
    nd = len(shape)
    return pl.BlockSpec(shape, lambda *_: (0,) * nd, pipeline_mode=pl.Buffered(1))


def _dot(a, b):
    return jnp.dot(a, b, preferred_element_type=F32)


def _dot_nt(a, b):
    return lax.dot_general(a, b, (((1,), (1,)), ((), ())), preferred_element_type=F32)


def _dot_tn(a, b):
    return lax.dot_general(a, b, (((0,), (0,)), ((), ())), preferred_element_type=F32)


def _split3(x):
    x1 = x.astype(BF16)
    r = x - x1.astype(F32)
    x2 = r.astype(BF16)
    x3 = (r - x2.astype(F32)).astype(BF16)
    return x1, x2, x3


def _dot_exact_rhs(a_bf16, x):
    x1, x2, x3 = _split3(x)
    return _dot(a_bf16, x1) + _dot(a_bf16, x2) + _dot(a_bf16, x3)


def _dot_f32ish(a, b):
    a1 = a.astype(BF16)
    a2 = (a - a1.astype(F32)).astype(BF16)
    b1 = b.astype(BF16)
    b2 = (b - b1.astype(F32)).astype(BF16)
    return _dot(a1, b1) + _dot(a1, b2) + _dot(a2, b1)


def _silu(x):
    return x * jax.nn.sigmoid(x)


def _modnorm(x, g, shift, scale):
    y = x * lax.rsqrt(jnp.mean(x * x, axis=-1, keepdims=True) + EPS)
    return (y * g) * (1.0 + scale) + shift


def _lane_iota(shape):
    return lax.broadcasted_iota(jnp.int32, shape, len(shape) - 1)


def _group_ones(n, group):
    r = lax.broadcasted_iota(jnp.int32, (n, n), 0) // group
    c = lax.broadcasted_iota(jnp.int32, (n, n), 1) // group
    return jnp.where(r == c, 1.0, 0.0).astype(BF16)


def _group_rmsnorm(x, gain, ones_g, group):
    x2 = x * x
    hi = x2.astype(BF16)
    lo = (x2 - hi.astype(F32)).astype(BF16)
    ms = (_dot(hi, ones_g) + _dot(lo, ones_g)) * (1.0 / group)
    return x * lax.rsqrt(ms + EPS) * gain


def _rope(x, cos_t, sin_t, half):
    n = x.shape[-1]
    left = pltpu.roll(x, n - half, 1)
    right = pltpu.roll(x, half, 1)
    sw = jnp.where((_lane_iota(x.shape) & (2 * half - 1)) < half, left, right)
    return x * cos_t + sw * sin_t


def _dwconv3(x, w, b):
    rows = x.shape[0]
    row = lax.broadcasted_iota(jnp.int32, x.shape, 0)
    xm = jnp.where(row == 0, 0.0, pltpu.roll(x, 1, 0))
    xp = jnp.where(row == rows - 1, 0.0, pltpu.roll(x, rows - 1, 0))
    return xm * w[0:1] + x * w[1:2] + xp * w[2:3] + b


def _head_mask(shape, h, width, dtype):
    lane = _lane_iota(shape)
    return jnp.where((lane >= h * width) & (lane < (h + 1) * width), 1.0, 0.0).astype(dtype)


def _mod_kernel(c_ref, w_ref, b_ref, o_ref):
    sc = _silu(c_ref[...]).astype(BF16)
    o_ref[0] = _dot(sc, w_ref[0].astype(BF16)) + b_ref[0]


def _modulation(c_all, ada_w, ada_b):
    depth, d, nmod = ada_w.shape
    rows = c_all.shape[0]
    tn = 1536
    return pl.pallas_call(
        _mod_kernel,
        grid=(depth, nmod // tn),
        in_specs=[pl.BlockSpec((rows, d), lambda l, j: (0, 0)),
                  pl.BlockSpec((1, d, tn), lambda l, j: (l, 0, j)),
                  pl.BlockSpec((1, 1, tn), lambda l, j: (l, 0, j))],
        out_specs=pl.BlockSpec((1, rows, tn), lambda l, j: (l, 0, j)),
        out_shape=jax.ShapeDtypeStruct((depth, rows, nmod), F32),
        compiler_params=_cparams("parallel", "parallel"),
        name="adaln_modulation",
    )(c_all, ada_w, ada_b.reshape(depth, 1, nmod))


def _ffn_kernel(x_ref, mod_ref, g_ref, w13_ref, w2_ref, o_ref, acc_ref, *, base, hidden, chunk):
    x = x_ref[...]
    shift, scale, gate = mod_ref[0, base:base + 1], mod_ref[0, base + 1:base + 2], mod_ref[0, base + 2:base + 3]
    xn = _modnorm(x, g_ref[...], shift, scale).astype(BF16)
    first = True
    for start in range(0, hidden, chunk):
        size = min(chunk, hidden - start)
        a = _dot(xn, w13_ref[:, start:start + size])
        b = _dot(xn, w13_ref[:, hidden + start:hidden + start + size])
        part = _dot((_silu(a) * b).astype(BF16), w2_ref[start:start + size, :])
        if first:
            acc_ref[...] = part
            first = False
        else:
            acc_ref[...] += part
    o_ref[...] = x + (0.5 * gate) * acc_ref[...]


def _mod_index(tile, seq, n_batch):
    return lambda i: (jnp.minimum((i * tile) // seq, n_batch), 0, 0)


def _ffn_half(h, mod, g, w13, w2, *, base, n_rows, seq, n_batch):
    d = h.shape[1]
    hidden = w2.shape[0]
    tm = ROW_TILE
    kern = functools.partial(_ffn_kernel, base=base, hidden=hidden, chunk=512)
    return pl.pallas_call(
        kern,
        grid=(n_rows // tm,),
        in_specs=[pl.BlockSpec((tm, d), lambda i: (i, 0)),
                  pl.BlockSpec((1, N_MOD, d), _mod_index(tm, seq, n_batch)),
                  _resident((1, d)), _resident(w13.shape), _resident(w2.shape)],
        out_specs=pl.BlockSpec((tm, d), lambda i: (i, 0)),
        out_shape=jax.ShapeDtypeStruct((n_rows, d), F32),
        scratch_shapes=[pltpu.VMEM((tm, d), F32)],
        compiler_params=_cparams("parallel"),
        name="swiglu_half_step",
    )(h, mod, g.reshape(1, d), w13, w2)


IN_ML, IN_GQA, IN_HY, IN_DIFF, IN_GATE = 1024, 768, 768, 768, 128


def _inproj_kernel(x_ref, mod_ref, g_ref, w_ref, ml_ref, gqa_ref, hy_ref, df_ref, gate_ref):
    xn = _modnorm(x_ref[...], g_ref[...], mod_ref[0, 3:4], mod_ref[0, 4:5]).astype(BF16)
    off = 0
    for ref, width in ((ml_ref, IN_ML), (gqa_ref, IN_GQA), (hy_ref, IN_HY), (df_ref, IN_DIFF), (gate_ref, IN_GATE)):
        ref[...] = _dot(xn, w_ref[:, off:off + width]).astype(ref.dtype)
        off += width


def _in_projection(h, mod, g, w, *, seq, n_batch):
    n_rows, d = h.shape
    tm = ROW_TILE
    widths = (IN_ML, IN_GQA, IN_HY, IN_DIFF, IN_GATE)
    dtypes = (BF16, BF16, BF16, BF16, F32)
    return pl.pallas_call(
        _inproj_kernel,
        grid=(n_rows // tm,),
        in_specs=[pl.BlockSpec((tm, d), lambda i: (i, 0)),
                  pl.BlockSpec((1, N_MOD, d), _mod_index(tm, seq, n_batch)),
                  _resident((1, d)), _resident(w.shape)],
        out_specs=[pl.BlockSpec((tm, wd), lambda i: (i, 0)) for wd in widths],
        out_shape=[jax.ShapeDtypeStruct((n_rows, wd), dt) for wd, dt in zip(widths, dtypes)],
        compiler_params=_cparams("parallel"),
        name="input_projection",
    )(h, mod, g.reshape(1, d), w)


def _outproj_kernel(h_ref, mod_ref, ym_ref, yg_ref, yh_ref, yd_ref, w_ref, o_ref):
    acc = _dot(ym_ref[...], w_ref[0:GROUP_W, :])
    for k, ref in enumerate((yg_ref, yh_ref, yd_ref), start=1):
        acc = acc + _dot(ref[...], w_ref[k * GROUP_W:(k + 1) * GROUP_W, :])
    o_ref[...] = h_ref[...] + mod_ref[0, 5:6] * acc


def _out_projection(h, mod, ys, w, *, n_rows, seq, n_batch):
    d = h.shape[1]
    tm = ROW_TILE
    return pl.pallas_call(
        _outproj_kernel,
        grid=(n_rows // tm,),
        in_specs=[pl.BlockSpec((tm, d), lambda i: (i, 0)),
                  pl.BlockSpec((1, N_MOD, d), _mod_index(tm, seq, n_batch))]
                 + [pl.BlockSpec((tm, GROUP_W), lambda i: (i, 0))] * 4
                 + [_resident(w.shape)],
        out_specs=pl.BlockSpec((tm, d), lambda i: (i, 0)),
        out_shape=jax.ShapeDtypeStruct((n_rows, d), F32),
        compiler_params=_cparams("parallel"),
        name="output_projection",
    )(h, mod, *ys, w)


def _softmax_pv(qm, k, v):
    s = _dot_nt(qm, k)
    p = jnp.exp(s - jnp.max(s, axis=-1, keepdims=True))
    inv = 1.0 / jnp.sum(p, axis=-1, keepdims=True)
    return _dot(p.astype(BF16), v), inv


def _gqa_kernel(xl_ref, xc_ref, cos_ref, sin_ref, gq_ref, gk_ref, yl_ref, yc_ref, q_s, k_s, *, seq, ctx, tq):
    ones_g = _group_ones(GROUP_W, HEAD_DIM)
    scale = HEAD_DIM ** -0.5

    def prep(x_ref, rows, dst, rope_rows):
        q = _group_rmsnorm(x_ref[rows, 0:GROUP_W].astype(F32), gq_ref[...], ones_g, HEAD_DIM)
        k = _group_rmsnorm(x_ref[rows, GROUP_W:2 * GROUP_W].astype(F32), gk_ref[...], ones_g, HEAD_DIM)
        if rope_rows is not None:
            cos_t, sin_t = cos_ref[rope_rows, :], sin_ref[rope_rows, :]
            q, k = _rope(q, cos_t, sin_t, HEAD_DIM // 2), _rope(k, cos_t, sin_t, HEAD_DIM // 2)
        q_s[dst, :] = (q * scale).astype(BF16)
        k_s[dst, :] = k.astype(BF16)

    prep(xc_ref, pl.ds(0, ctx), pl.ds(0, ctx), None)

    def prep_body(i, carry):
        r = pl.ds(pl.multiple_of(i * tq, tq), tq)
        prep(xl_ref, r, pl.ds(pl.multiple_of(ctx + i * tq, tq), tq), r)
        return carry

    lax.fori_loop(0, seq // tq, prep_body, 0)

    def attend(q, k, v):
        acc = jnp.zeros((q.shape[0], GROUP_W), F32)
        for h in range(N_HEADS):
            o, inv = _softmax_pv(q * _head_mask((1, GROUP_W), h, HEAD_DIM, BF16), k, v)
            acc = acc + (o * inv) * _head_mask((1, GROUP_W), h, HEAD_DIM, F32)
        return acc

    yc_ref[...] = attend(q_s[0:ctx, :], k_s[0:ctx, :], xc_ref[:, 2 * GROUP_W:3 * GROUP_W]).astype(yc_ref.dtype)

    def body(i, carry):
        r = pl.ds(pl.multiple_of(ctx + i * tq, tq), tq)
        o_c, inv_c = None, None
        acc = jnp.zeros((tq, GROUP_W), F32)
        for h in range(N_HEADS):
            qm = q_s[r, :] * _head_mask((1, GROUP_W), h, HEAD_DIM, BF16)
            s_c = _dot_nt(qm, k_s[0:ctx, :])
            s_l = _dot_nt(qm, k_s[ctx:ctx + seq, :])
            m = jnp.maximum(jnp.max(s_c, axis=-1, keepdims=True), jnp.max(s_l, axis=-1, keepdims=True))
            p_c, p_l = jnp.exp(s_c - m), jnp.exp(s_l - m)
            inv = 1.0 / (jnp.sum(p_c, axis=-1, keepdims=True) + jnp.sum(p_l, axis=-1, keepdims=True))
            o = (_dot(p_c.astype(BF16), xc_ref[:, 2 * GROUP_W:3 * GROUP_W])
                 + _dot(p_l.astype(BF16), xl_ref[:, 2 * GROUP_W:3 * GROUP_W]))
            acc = acc + (o * inv) * _head_mask((1, GROUP_W), h, HEAD_DIM, F32)
        yl_ref[pl.ds(pl.multiple_of(i * tq, tq), tq), :] = acc.astype(yl_ref.dtype)
        return carry

    lax.fori_loop(0, seq // tq, body, 0)


def _gqa_mixer(p_gqa, cos_t, sin_t, gq, gk, *, n_batch, seq, ctx):
    tq = 256
    lat_blocks = n_batch * seq // ctx
    kern = functools.partial(_gqa_kernel, seq=seq, ctx=ctx, tq=tq)
    yl, yc = pl.pallas_call(
        kern,
        grid=(n_batch,),
        in_specs=[pl.BlockSpec((seq, IN_GQA), lambda b: (b, 0)),
                  pl.BlockSpec((ctx, IN_GQA), lambda b: (lat_blocks + b, 0)),
                  _resident(cos_t.shape), _resident(sin_t.shape),
                  _resident((1, GROUP_W)), _resident((1, GROUP_W))],
        out_specs=[pl.BlockSpec((seq, GROUP_W), lambda b: (b, 0)),
                   pl.BlockSpec((ctx, GROUP_W), lambda b: (b, 0))],
        out_shape=[jax.ShapeDtypeStruct((n_batch * seq, GROUP_W), BF16),
                   jax.ShapeDtypeStruct((n_batch * ctx, GROUP_W), BF16)],
        scratch_shapes=[pltpu.VMEM((ctx + seq, GROUP_W), BF16), pltpu.VMEM((ctx + seq, GROUP_W), BF16)],
        compiler_params=_cparams("parallel"),
        name="gqa_mixer",
    )(p_gqa, p_gqa, cos_t, sin_t, gq, gk)
    return jnp.concatenate([yl, yc], axis=0)


def _diff_kernel(xl_ref, xc_ref, cos_ref, sin_ref, gq_ref, gk_ref, lam_ref, sub_ref, yl_ref, yc_ref, q_s, k_s,
                 *, seq, ctx, tq, lam_init):
    ones_sub = _group_ones(GROUP_W, DIFF_SUB)
    ones_head = _group_ones(GROUP_W, HEAD_DIM)
    scale = DIFF_SUB ** -0.5
    lp = lam_ref[...]
    lam = (jnp.exp(jnp.sum(lp[0:1] * lp[1:2], axis=-1, keepdims=True))
           - jnp.exp(jnp.sum(lp[2:3] * lp[3:4], axis=-1, keepdims=True)) + lam_init)

    def prep(x_ref, rows, dst, rope_rows):
        q = _group_rmsnorm(x_ref[rows, 0:GROUP_W].astype(F32), gq_ref[...], ones_sub, DIFF_SUB)
        k = _group_rmsnorm(x_ref[rows, GROUP_W:2 * GROUP_W].astype(F32), gk_ref[...], ones_sub, DIFF_SUB)
        if rope_rows is not None:
            cos_t, sin_t = cos_ref[rope_rows, :], sin_ref[rope_rows, :]
            q, k = _rope(q, cos_t, sin_t, DIFF_SUB // 2), _rope(k, cos_t, sin_t, DIFF_SUB // 2)
        q_s[dst, :] = (q * scale).astype(BF16)
        k_s[dst, :] = k.astype(BF16)

    prep(xc_ref, pl.ds(0, ctx), pl.ds(0, ctx), None)

    def prep_body(i, carry):
        r = pl.ds(pl.multiple_of(i * tq, tq), tq)
        prep(xl_ref, r, pl.ds(pl.multiple_of(ctx + i * tq, tq), tq), r)
        return carry

    lax.fori_loop(0, seq // tq, prep_body, 0)

    def finish(acc):
        return (_group_rmsnorm(acc, sub_ref[...], ones_head, HEAD_DIM) * (1.0 - lam_init))

    def sub_mask(h, j, dtype):
        return _head_mask((1, GROUP_W), 2 * h + j, DIFF_SUB, dtype)

    acc = jnp.zeros((ctx, GROUP_W), F32)
    v_c = xc_ref[:, 2 * GROUP_W:3 * GROUP_W]
    for h in range(N_HEADS):
        o1, inv1 = _softmax_pv(q_s[0:ctx, :] * sub_mask(h, 0, BF16), k_s[0:ctx, :], v_c)
        o2, inv2 = _softmax_pv(q_s[0:ctx, :] * sub_mask(h, 1, BF16), k_s[0:ctx, :], v_c)
        acc = acc + (o1 * inv1 - lam * (o2 * inv2)) * _head_mask((1, GROUP_W), h, HEAD_DIM, F32)
    yc_ref[...] = finish(acc).astype(yc_ref.dtype)

    def body(i, carry):
        r = pl.ds(pl.multiple_of(ctx + i * tq, tq), tq)
        acc = jnp.zeros((tq, GROUP_W), F32)
        for h in range(N_HEADS):
            outs = []
            for j in range(2):
                qm = q_s[r, :] * sub_mask(h, j, BF16)
                s_c = _dot_nt(qm, k_s[0:ctx, :])
                s_l = _dot_nt(qm, k_s[ctx:ctx + seq, :])
                m = jnp.maximum(jnp.max(s_c, axis=-1, keepdims=True), jnp.max(s_l, axis=-1, keepdims=True))
                p_c, p_l = jnp.exp(s_c - m), jnp.exp(s_l - m)
                inv = 1.0 / (jnp.sum(p_c, axis=-1, keepdims=True) + jnp.sum(p_l, axis=-1, keepdims=True))
                o = (_dot(p_c.astype(BF16), xc_ref[:, 2 * GROUP_W:3 * GROUP_W])
                     + _dot(p_l.astype(BF16), xl_ref[:, 2 * GROUP_W:3 * GROUP_W]))
                outs.append(o * inv)
            acc = acc + (outs[0] - lam * outs[1]) * _head_mask((1, GROUP_W), h, HEAD_DIM, F32)
        yl_ref[pl.ds(pl.multiple_of(i * tq, tq), tq), :] = finish(acc).astype(yl_ref.dtype)
        return carry

    lax.fori_loop(0, seq // tq, body, 0)


def _diff_mixer(p_df, cos_t, sin_t, gq, gk, lam_p, sub_g, *, lam_init, n_batch, seq, ctx):
    tq = 256
    lat_blocks = n_batch * seq // ctx
    kern = functools.partial(_diff_kernel, seq=seq, ctx=ctx, tq=tq, lam_init=lam_init)
    yl, yc = pl.pallas_call(
        kern,
        grid=(n_batch,),
        in_specs=[pl.BlockSpec((seq, IN_DIFF), lambda b: (b, 0)),
                  pl.BlockSpec((ctx, IN_DIFF), lambda b: (lat_blocks + b, 0)),
                  _resident(cos_t.shape), _resident(sin_t.shape),
                  _resident((1, GROUP_W)), _resident((1, GROUP_W)),
                  _resident(lam_p.shape), _resident((1, GROUP_W))],
        out_specs=[pl.BlockSpec((seq, GROUP_W), lambda b: (b, 0)),
                   pl.BlockSpec((ctx, GROUP_W), lambda b: (b, 0))],
        out_shape=[jax.ShapeDtypeStruct((n_batch * seq, GROUP_W), BF16),
                   jax.ShapeDtypeStruct((n_batch * ctx, GROUP_W), BF16)],
        scratch_shapes=[pltpu.VMEM((ctx + seq, GROUP_W), BF16), pltpu.VMEM((ctx + seq, GROUP_W), BF16)],
        compiler_params=_cparams("parallel"),
        name="diff_attention_mixer",
    )(p_df, p_df, cos_t, sin_t, gq, gk, lam_p, sub_g)
    return jnp.concatenate([yl, yc], axis=0)


def _mlstm_kernel(xl_ref, xc_ref, gl_ref, gc_ref, cw_ref, cb_ref, gb_ref, ng_ref, yl_ref, yc_ref,
                  q_s, k_s, hsum, c_s, n_s, m_s, *, seq, ctx):
    t = ML_CHUNK
    n_chunks = seq // t
    ones_head = _group_ones(GROUP_W, HEAD_DIM)
    block_diag = ones_head.astype(F32)
    r_i = lax.broadcasted_iota(jnp.int32, (t, t), 0)
    c_i = lax.broadcasted_iota(jnp.int32, (t, t), 1)
    tril_b = jnp.where(c_i <= r_i, 1.0, 0.0).astype(BF16)
    causal = (c_i <= r_i, c_i >= r_i)
    glane = _lane_iota((1, IN_GATE))
    fwd_cols = glane < 8

    for x_ref, rows, off in ((xc_ref, ctx, 0), (xl_ref, seq, ctx)):
        qk = _silu(_dwconv3(x_ref[:, 0:2 * GROUP_W].astype(F32), cw_ref[...], cb_ref[...]))
        q_s[off:off + rows, :] = (qk[:, 0:GROUP_W] * (HEAD_DIM ** -0.5)).astype(BF16)
        k_s[off:off + rows, :] = qk[:, GROUP_W:2 * GROUP_W].astype(BF16)

    hsum[...] = jnp.zeros_like(hsum)
    c_s[...] = jnp.zeros_like(c_s)
    n_s[...] = jnp.zeros_like(n_s)
    m_s[...] = jnp.zeros_like(m_s)

    def gate_terms(g_raw):
        g = g_raw + gb_ref[...]
        logf = pltpu.roll(jax.nn.log_sigmoid(g), IN_GATE - N_HEADS, 1)
        pre = _dot_exact_rhs(tril_b, logf)
        tot = pre[t - 1:t, :]
        suf = tot - pre + logf
        cum = jnp.where(fwd_cols, pre, suf)
        return g, cum, tot

    def chunk(d, rows_q, v, gates, first):
        g, cum, tot = gates
        q = q_s[rows_q, :]
        k = k_s[rows_q, :]
        r_t = jnp.transpose(g - cum)
        ct = c_s[d]
        n_full = n_s[d]
        m_full = m_s[d]
        inter = _dot(q, ct.astype(BF16))
        qn = q.astype(F32) * n_full
        h_out = jnp.zeros((t, GROUP_W), F32)
        wg_full = jnp.zeros((t, GROUP_W), F32)
        wc_full = jnp.zeros((1, GROUP_W), F32)
        m_new_full = jnp.zeros((1, GROUP_W), F32)
        for h in range(N_HEADS):
            col = d * 8 + h
            mask_f = _head_mask((1, GROUP_W), h, HEAD_DIM, F32)
            b_col = cum[:, col:col + 1]
            i_col = g[:, col:col + 1]
            b_end = tot[:, col:col + 1]
            m_prev = m_full[:, h * HEAD_DIM:h * HEAD_DIM + 1]
            log_w = jnp.where(causal[d], b_col + r_t[col:col + 1, :], -jnp.inf)
            log_inter = b_col + m_prev
            m_t = jnp.maximum(log_inter, jnp.max(log_w, axis=-1, keepdims=True))
            w_intra = jnp.exp(log_w - m_t)
            w_inter = jnp.exp(log_inter - m_t)
            s = _dot_nt(q * _head_mask((1, GROUP_W), h, HEAD_DIM, BF16), k) * w_intra
            num = w_inter * inter + _dot(s.astype(BF16), v)
            den = (w_inter * jnp.sum(qn * mask_f, axis=-1, keepdims=True) + jnp.sum(s, axis=-1, keepdims=True))
            h_out = h_out + (num / jnp.maximum(jnp.abs(den), jnp.exp(-m_t))) * mask_f
            log_g = (b_end - b_col) + i_col
            m_new = jnp.maximum(b_end + m_prev, jnp.max(log_g, axis=0, keepdims=True))
            wg_full = wg_full + jnp.exp(log_g - m_new) * mask_f
            wc_full = wc_full + jnp.exp(b_end + m_prev - m_new) * mask_f
            m_new_full = m_new_full + m_new * mask_f
        kw = k.astype(F32) * wg_full
        c_s[d] = wc_full * ct + _dot_tn(kw.astype(BF16), v) * block_diag
        n_s[d] = wc_full * n_full + jnp.sum(kw, axis=0, keepdims=True)
        m_s[d] = m_new_full
        return h_out

    assert ctx == t
    gates_c = gate_terms(gc_ref[...])
    v_c = xc_ref[:, 2 * GROUP_W:3 * GROUP_W]
    hsum[0:t, :] = chunk(0, pl.ds(0, t), v_c, gates_c, True) + chunk(1, pl.ds(0, t), v_c, gates_c, True)

    def body(j, carry):
        for d in range(2):
            cidx = j if d == 0 else n_chunks - 1 - j
            lat = pl.ds(pl.multiple_of(cidx * t, t), t)
            rows = pl.ds(pl.multiple_of(ctx + cidx * t, t), t)
            hsum[rows, :] += chunk(d, rows, xl_ref[lat, 2 * GROUP_W:3 * GROUP_W], gate_terms(gl_ref[lat, :]), False)
        return carry

    lax.fori_loop(0, n_chunks, body, 0)

    for x_ref, y_ref, rows, off in ((xc_ref, yc_ref, ctx, 0), (xl_ref, yl_ref, seq, ctx)):
        hn = _group_rmsnorm(hsum[off:off + rows, :], ng_ref[...], ones_head, HEAD_DIM)
        y_ref[...] = (hn * jax.nn.sigmoid(x_ref[:, 3 * GROUP_W:4 * GROUP_W].astype(F32))).astype(y_ref.dtype)


def _mlstm_mixer(p_ml, p_gate, conv_w, conv_b, gate_b, norm_g, *, n_batch, seq, ctx):
    lat_blocks = n_batch * seq // ctx
    kern = functools.partial(_mlstm_kernel, seq=seq, ctx=ctx)
    yl, yc = pl.pallas_call(
        kern,
        grid=(n_batch,),
        in_specs=[pl.BlockSpec((seq, IN_ML), lambda b: (b, 0)),
                  pl.BlockSpec((ctx, IN_ML), lambda b: (lat_blocks + b, 0)),
                  pl.BlockSpec((seq, IN_GATE), lambda b: (b, 0)),
                  pl.BlockSpec((ctx, IN_GATE), lambda b: (lat_blocks + b, 0)),
                  _resident(conv_w.shape), _resident(conv_b.shape), _resident(gate_b.shape), _resident(norm_g.shape)],
        out_specs=[pl.BlockSpec((seq, GROUP_W), lambda b: (b, 0)),
                   pl.BlockSpec((ctx, GROUP_W), lambda b: (b, 0))],
        out_shape=[jax.ShapeDtypeStruct((n_batch * seq, GROUP_W), BF16),
                   jax.ShapeDtypeStruct((n_batch * ctx, GROUP_W), BF16)],
        scratch_shapes=[pltpu.VMEM((ctx + seq, GROUP_W), BF16), pltpu.VMEM((ctx + seq, GROUP_W), BF16),
                        pltpu.VMEM((ctx + seq, GROUP_W), F32),
                        pltpu.VMEM((2, GROUP_W, GROUP_W), F32), pltpu.VMEM((2, 1, GROUP_W), F32),
                        pltpu.VMEM((2, 1, GROUP_W), F32)],
        compiler_params=_cparams("parallel"),
        name="mlstm_mixer",
    )(p_ml, p_ml, p_gate, p_gate, conv_w, conv_b, gate_b, norm_g)
    return jnp.concatenate([yl, yc], axis=0)


def _dft_matrices(length, n):
    nf = n // 2
    k = jnp.arange(nf, dtype=jnp.int32)
    j = jnp.arange(length, dtype=jnp.int32)
    ang = (2.0 * math.pi / n) * ((k[:, None] * j[None, :]) % n).astype(F32)
    alt = jnp.where(j % 2 == 0, 1.0, -1.0).astype(F32)
    sin_rows = jnp.where(k[:, None] == 0, alt[None, :], jnp.sin(ang))
    fwd = jnp.concatenate([jnp.cos(ang), sin_rows], axis=0)
    m = j + length // 2
    ang_i = (2.0 * math.pi / n) * ((m[:, None] * k[None, :]) % n).astype(F32)
    wk = jnp.where(k == 0, 1.0, 2.0).astype(F32) / n
    alt_m = jnp.where(m % 2 == 0, 1.0, -1.0).astype(F32) / n
    inv_sin = jnp.where(k[None, :] == 0, alt_m[:, None], jnp.sin(ang_i) * (2.0 / n))
    inv = jnp.concatenate([jnp.cos(ang_i) * wk[None, :], inv_sin], axis=1)
    return fwd.astype(BF16), inv.astype(BF16)


def _hyena_filter_consts(length):
    t = jnp.arange(length, dtype=F32)
    tn = t / length
    bands = jnp.arange(1, HY_POS_BANDS + 1, dtype=F32)
    ang = 2.0 * math.pi * tn[:, None] * bands
    feats = jnp.concatenate([tn[:, None], jnp.cos(ang), jnp.sin(ang)], axis=-1)
    feats = jnp.pad(feats, ((0, 0), (0, 128 - feats.shape[1])))
    dist = jnp.abs(t - length // 2) / (length / 2)
    deltas = jnp.abs(jnp.linspace(math.log(HY_DECAY_TARGET) / HY_SLOW_DECAY,
                                  math.log(HY_DECAY_TARGET) / HY_FAST_DECAY, GROUP_W, dtype=F32))
    window = jnp.exp(-dist[:, None] * jnp.tile(deltas, 2))
    return feats, window


def _filter_kernel(feats_ref, win_ref, w1_ref, b1_ref, w2_ref, b2_ref, w3_ref, b3_ref, f_ref, p_ref, qz_ref, p2_ref):
    h = jnp.sin(_dot_f32ish(feats_ref[...], w1_ref[0]) + b1_ref[0])
    h = jnp.sin(_dot_f32ish(h, w2_ref[0]) + b2_ref[0])
    h = (_dot_f32ish(h, w3_ref[0]) + b3_ref[0]) * win_ref[...]
    h = h / jnp.sum(jnp.abs(h), axis=0, keepdims=True)
    h1, h2, h3 = _split3(h)
    nf = f_ref.shape[0] // 2
    step = math.gcd(nf, 512)
    for r in range(0, nf, step):
        fc, fs = f_ref[r:r + step, :], f_ref[nf + r:nf + r + step, :]
        p = _dot(fc, h1) + _dot(fc, h2) + _dot(fc, h3)
        q = _dot(fs, h1) + _dot(fs, h2) + _dot(fs, h3)
        p_ref[0, r:r + step, :] = p
        if r == 0:
            first = lax.broadcasted_iota(jnp.int32, p.shape, 0) == 0
            qz_ref[0, r:r + step, :] = jnp.where(first, 0.0, q)
            p2_ref[0, r:r + step, :] = jnp.where(first, q, p)
        else:
            qz_ref[0, r:r + step, :] = q
            p2_ref[0, r:r + step, :] = p


def _hyena_filter_spectra(length, fwd, w1, b1, w2, b2, w3, b3):
    depth = w1.shape[0]
    feats, window = _hyena_filter_consts(length)
    nf = fwd.shape[0] // 2
    pad2 = lambda a, r, c: jnp.pad(a, ((0, 0), (0, r - a.shape[1]), (0, c - a.shape[2])))
    w1p, w2p = pad2(w1, 128, 128), pad2(w2, 128, 128)
    w3p = pad2(w3, 128, w3.shape[2])
    b1p = jnp.pad(b1, ((0, 0), (0, 128 - b1.shape[1])))[:, None, :]
    b2p = jnp.pad(b2, ((0, 0), (0, 128 - b2.shape[1])))[:, None, :]
    b3p = b3[:, None, :]
    lay = lambda a: pl.BlockSpec((1,) + a.shape[1:], lambda l, o: (l, 0, 0))
    per_order = lambda rows: pl.BlockSpec((1, rows, GROUP_W), lambda l, o: (l, 0, o))
    out = jax.ShapeDtypeStruct((depth, nf, 2 * GROUP_W), F32)
    return pl.pallas_call(
        _filter_kernel,
        grid=(depth, 2),
        in_specs=[_resident(feats.shape), pl.BlockSpec((length, GROUP_W), lambda l, o: (0, o)),
                  lay(w1p), lay(b1p), lay(w2p), lay(b2p), per_order(128), per_order(1), _resident(fwd.shape)],
        out_specs=[per_order(nf)] * 3,
        out_shape=[out, out, out],
        compiler_params=_cparams("parallel", "parallel"),
        name="hyena_filter_spectra",
    )(feats, window, w1p, b1p, w2p, b2p, w3p, b3p, fwd)


def _hyena_conv_kernel(sig_ref, gate_ref, cws_ref, cbs_ref, cwg_ref, cbg_ref, skip_ref, f_ref, g_ref,
                       p_ref, qz_ref, p2_ref, o_ref, *, sig_conv):
    s = sig_ref[...].astype(F32)
    if sig_conv:
        s = _dwconv3(s, cws_ref[...], cbs_ref[...])
    gate = _dwconv3(gate_ref[...].astype(F32), cwg_ref[...], cbg_ref[...])
    sb = s.astype(BF16)
    nf = f_ref.shape[0] // 2
    a = _dot(f_ref[0:nf, :], sb)
    b = _dot(f_ref[nf:2 * nf, :], sb)
    ay = a * p_ref[0] - b * qz_ref[0]
    by = a * qz_ref[0] + b * p2_ref[0]
    y = _dot(g_ref[:, 0:nf], ay.astype(BF16)) + _dot(g_ref[:, nf:2 * nf], by.astype(BF16))
    o_ref[...] = (gate * (y + skip_ref[...] * s)).astype(o_ref.dtype)


def _hyena_conv(sig, sig_col, gate, gate_col, conv_w, conv_b, skip, fwd, inv, spectra, order, layer,
                *, rows, first_block, n_batch, out_dtype, sig_conv):
    cw = lambda col: pl.BlockSpec((3, GROUP_W), lambda b: (0, col))
    cb = lambda col: pl.BlockSpec((1, GROUP_W), lambda b: (0, col))
    spec = pl.BlockSpec((1, fwd.shape[0] // 2, GROUP_W), lambda b: (layer, 0, order), pipeline_mode=pl.Buffered(1))
    kern = functools.partial(_hyena_conv_kernel, sig_conv=sig_conv)
    return pl.pallas_call(
        kern,
        grid=(n_batch,),
        in_specs=[pl.BlockSpec((rows, GROUP_W), lambda b: (first_block + b, sig_col)),
                  pl.BlockSpec((rows, GROUP_W), lambda b: (first_block + b, gate_col)),
                  cw(sig_col if sig_conv else 0), cb(sig_col if sig_conv else 0), cw(gate_col), cb(gate_col),
                  pl.BlockSpec((1, GROUP_W), lambda b: (0, 0)),
                  _resident(fwd.shape), _resident(inv.shape), spec, spec, spec],
        out_specs=pl.BlockSpec((rows, GROUP_W), lambda b: (b, 0)),
        out_shape=jax.ShapeDtypeStruct((n_batch * rows, GROUP_W), out_dtype),
        compiler_params=_cparams("parallel"),
        name="hyena_long_conv",
    )(sig, gate, conv_w, conv_b, conv_w, conv_b, skip, fwd, inv, *spectra)


def _hyena_mixer(p_hy, conv_w, conv_b, skip, dft, spectra, layer, *, rows, first_block, n_batch):
    fwd, inv = dft
    common = dict(rows=rows, n_batch=n_batch)
    z = _hyena_conv(p_hy, 0, p_hy, 1, conv_w, conv_b, skip[0:1], fwd, inv, spectra, 0, layer,
                    first_block=first_block, out_dtype=F32, sig_conv=True, **common)
    cw = lambda col: pl.BlockSpec((3, GROUP_W), lambda b: (0, col))
    cb = lambda col: pl.BlockSpec((1, GROUP_W), lambda b: (0, col))
    spec = pl.BlockSpec((1, fwd.shape[0] // 2, GROUP_W), lambda b: (layer, 0, 1), pipeline_mode=pl.Buffered(1))
    kern = functools.partial(_hyena_conv_kernel, sig_conv=False)
    return pl.pallas_call(
        kern,
        grid=(n_batch,),
        in_specs=[pl.BlockSpec((rows, GROUP_W), lambda b: (b, 0)),
                  pl.BlockSpec((rows, GROUP_W), lambda b: (first_block + b, 2)),
                  cw(0), cb(0), cw(2), cb(2),
                  pl.BlockSpec((1, GROUP_W), lambda b: (0, 0)),
                  _resident(fwd.shape), _resident(inv.shape), spec, spec, spec],
        out_specs=pl.BlockSpec((rows, GROUP_W), lambda b: (b, 0)),
        out_shape=jax.ShapeDtypeStruct((n_batch * rows, GROUP_W), BF16),
        compiler_params=_cparams("parallel"),
        name="hyena_long_conv_2",
    )(z, p_hy, conv_w, conv_b, conv_w, conv_b, skip[1:2], fwd, inv, *spectra)


def _rope_tables(length, dim, reps):
    rows = length // GRID_W
    row = jnp.repeat(jnp.arange(rows), GRID_W).astype(F32)
    col = jnp.tile(jnp.arange(GRID_W), rows).astype(F32)
    n_freq = dim // 4
    inv = ROPE_THETA ** (-jnp.arange(n_freq, dtype=F32) / n_freq)
    ang = jnp.concatenate([row[:, None] * inv, col[:, None] * inv], axis=-1)
    cos, sin = jnp.cos(ang), jnp.sin(ang)
    return (jnp.tile(jnp.concatenate([cos, cos], axis=-1), (1, reps)),
            jnp.tile(jnp.concatenate([-sin, sin], axis=-1), (1, reps)))


def _in_weight_layout(w_in):
    ml, n_gate = 4 * GROUP_W, 16
    o_gqa = ml + n_gate
    o_hy = o_gqa + 2 * GROUP_W
    o_df = o_hy + 3 * GROUP_W
    rep = lambda start: jnp.concatenate([w_in[..., start + (h // 2) * HEAD_DIM: start + (h // 2 + 1) * HEAD_DIM]
                                         for h in range(N_HEADS)], axis=-1)
    gates = jnp.pad(w_in[..., ml:ml + n_gate], ((0, 0), (0, 0), (0, IN_GATE - n_gate)))
    return jnp.concatenate([w_in[..., 0:ml],
                            w_in[..., o_gqa:o_gqa + GROUP_W], rep(o_gqa + GROUP_W), rep(o_gqa + GROUP_W + 2 * HEAD_DIM),
                            w_in[..., o_hy:o_hy + 3 * GROUP_W],
                            w_in[..., o_df:o_df + 3 * GROUP_W],
                            gates], axis=-1).astype(BF16)


def kernel(x, c, ctx, c_ctx, ada_w, ada_b, norm_g, ffn_w13, ffn_w2, w_in, w_out, ml_gate_b, ml_conv_w, ml_conv_b,
           ml_norm_g, gqa_qk_g, hy_conv_w, hy_conv_b, hy_filt_w1, hy_filt_b1, hy_filt_w2, hy_filt_b2, hy_filt_w3,
           hy_filt_b3, hy_skip, diff_qk_g, diff_lambda, diff_subln_g):
    n_batch, seq, d = x.shape
    n_ctx = ctx.shape[1]
    depth = ada_w.shape[0]
    n_lat = n_batch * seq
    n_all = n_lat + n_batch * n_ctx
    assert n_batch * n_ctx == seq and n_ctx == ML_CHUNK and seq % ROW_TILE == 0

    c_all = jnp.pad(jnp.concatenate([c, c_ctx[None]], axis=0), ((0, 16 - n_batch - 1), (0, 0)))
    mod = _modulation(c_all, ada_w, ada_b)[:, :n_batch + 1].reshape(depth, n_batch + 1, N_MOD, d)

    w13 = ffn_w13.astype(BF16)
    w2 = ffn_w2.astype(BF16)
    w_in_l = _in_weight_layout(w_in)
    w_out_b = w_out.astype(BF16)

    rope_gqa = _rope_tables(seq, HEAD_DIM, N_HEADS)
    rope_diff = _rope_tables(seq, DIFF_SUB, 2 * N_HEADS)
    dft_l = _dft_matrices(seq, 3 * seq // 2)
    dft_c = _dft_matrices(n_ctx, 2 * n_ctx)
    filt = (hy_filt_w1, hy_filt_b1, hy_filt_w2, hy_filt_b2, hy_filt_w3, hy_filt_b3)
    spec_l = _hyena_filter_spectra(seq, dft_l[0], *filt)
    spec_c = _hyena_filter_spectra(n_ctx, dft_c[0], *filt)

    h = jnp.concatenate([x.reshape(n_lat, d), ctx.reshape(n_batch * n_ctx, d)], axis=0)
    dims = dict(seq=seq, n_batch=n_batch)
    for l in range(depth):
        need_ctx = l < depth - 1
        lam_init = 0.8 - 0.6 * math.exp(-0.3 * l)
        h = _ffn_half(h, mod[l], norm_g[l, 0], w13[l, 0], w2[l, 0], base=0, n_rows=n_all, **dims)
        p_ml, p_gqa, p_hy, p_df, p_gate = _in_projection(h, mod[l], norm_g[l, 1], w_in_l[l], **dims)

        gate_b = jnp.pad(ml_gate_b[l], (0, IN_GATE - ml_gate_b.shape[1]))[None]
        y_ml = _mlstm_mixer(p_ml, p_gate, ml_conv_w[l], ml_conv_b[l][None], gate_b, ml_norm_g[l][None],
                            n_batch=n_batch, seq=seq, ctx=n_ctx)
        tile_g = lambda g, reps: jnp.tile(g, reps)[None]
        y_gqa = _gqa_mixer(p_gqa, *rope_gqa, tile_g(gqa_qk_g[l, 0], N_HEADS), tile_g(gqa_qk_g[l, 1], N_HEADS),
                           n_batch=n_batch, seq=seq, ctx=n_ctx)
        y_df = _diff_mixer(p_df, *rope_diff, tile_g(diff_qk_g[l, 0], 2 * N_HEADS), tile_g(diff_qk_g[l, 1], 2 * N_HEADS),
                           diff_lambda[l], tile_g(diff_subln_g[l], N_HEADS),
                           lam_init=lam_init, n_batch=n_batch, seq=seq, ctx=n_ctx)
        y_hy = _hyena_mixer(p_hy, hy_conv_w[l], hy_conv_b[l][None], hy_skip[l], dft_l, spec_l, l,
                            rows=seq, first_block=0, n_batch=n_batch)
        n_rows = n_all if need_ctx else n_lat
        if need_ctx:
            y_hy_c = _hyena_mixer(p_hy, hy_conv_w[l], hy_conv_b[l][None], hy_skip[l], dft_c, spec_c, l,
                                  rows=n_ctx, first_block=n_lat // n_ctx, n_batch=n_batch)
            y_hy = jnp.concatenate([y_hy, y_hy_c], axis=0)
        h = _out_projection(h, mod[l], (y_ml, y_gqa, y_hy, y_df), w_out_b[l], n_rows=n_rows, **dims)
        h = _ffn_half(h, mod[l], norm_g[l, 2], w13[l, 1], w2[l, 1], base=6, n_rows=n_rows, **dims)
    return h[:n_lat].reshape(n_batch, seq, d)
```

```python
import functools
import math

import jax
import jax.numpy as jnp
from jax import lax
from jax.experimental import pallas as pl
from jax.experimental.pallas import tpu as pltpu

F32 = jnp.float32
BF16 = jnp.bfloat16

EPS = 1e-6
ROPE_THETA = 10000.0
GRID_W = 64
N_MOD = 9
GROUP_W = 256
HEAD_DIM = 64
N_HEADS = 4
DIFF_SUB = 32
ML_CHUNK = 256
HY_POS_BANDS = 16
HY_DECAY_TARGET = 1e-2
HY_FAST_DECAY = 0.3
HY_SLOW_DECAY = 1.5
ROW_TILE = 512
VMEM_LIMIT = 56 * 1024 * 1024


def _cparams(*sem):
    return pltpu.CompilerParams(dimension_semantics=sem, vmem_limit_bytes=VMEM_LIMIT)


def _resident(shape):
    nd = len(shape)
    return pl.BlockSpec(shape, lambda *_: (0,) * nd, pipeline_mode=pl.Buffered(1))


def _dot(a, b):
    return jnp.dot(a, b, preferred_element_type=F32)


def _dot_nt(a, b):
    return lax.dot_general(a, b, (((1,), (1,)), ((), ())), preferred_element_type=F32)


def _dot_tn(a, b):
    return lax.dot_general(a, b, (((0,), (0,)), ((), ())), preferred_element_type=F32)


def _split3(x):
    x1 = x.astype(BF16)
    r = x - x1.astype(F32)
    x2 = r.astype(BF16)
    x3 = (r - x2.astype(F32)).astype(BF16)
    return x1, x2, x3


def _dot_exact_rhs(a_bf16, x):
    x1, x2, x3 = _split3(x)
    return _dot(a_bf16, x1) + _dot(a_bf16, x2) + _dot(a_bf16, x3)


def _dot_f32ish(a, b):
    a1 = a.astype(BF16)
    a2 = (a - a1.astype(F32)).astype(BF16)
    b1 = b.astype(BF16)
    b2 = (b - b1.astype(F32)).astype(BF16)
    return _dot(a1, b1) + _dot(a1, b2) + _dot(a2, b1)


def _silu(x):
    return x * jax.nn.sigmoid(x)


def _modnorm(x, g, shift, scale):
    y = x * lax.rsqrt(jnp.mean(x * x, axis=-1, keepdims=True) + EPS)
    return (y * g) * (1.0 + scale) + shift


def _lane_iota(shape):
    return lax.broadcasted_iota(jnp.int32, shape, len(shape) - 1)


def _group_ones(n, group):
    r = lax.broadcasted_iota(jnp.int32, (n, n), 0) // group
    c = lax.broadcasted_iota(jnp.int32, (n, n), 1) // group
    return jnp.where(r == c, 1.0, 0.0).astype(BF16)


def _group_rmsnorm(x, gain, ones_g, group):
    x2 = x * x
    hi = x2.astype(BF16)
    lo = (x2 - hi.astype(F32)).astype(BF16)
    ms = (_dot(hi, ones_g) + _dot(lo, ones_g)) * (1.0 / group)
    return x * lax.rsqrt(ms + EPS) * gain


def _rope(x, cos_t, sin_t, half):
    n = x.shape[-1]
    left = pltpu.roll(x, n - half, 1)
    right = pltpu.roll(x, half, 1)
    sw = jnp.where((_lane_iota(x.shape) & (2 * half - 1)) < half, left, right)
    return x * cos_t + sw * sin_t


def _dwconv3(x, w, b):
    rows = x.shape[0]
    row = lax.broadcasted_iota(jnp.int32, x.shape, 0)
    xm = jnp.where(row == 0, 0.0, pltpu.roll(x, 1, 0))
    xp = jnp.where(row == rows - 1, 0.0, pltpu.roll(x, rows - 1, 0))
    return xm * w[0:1] + x * w[1:2] + xp * w[2:3] + b


def _head_mask(shape, h, width, dtype):
    lane = _lane_iota(shape)
    return jnp.where((lane >= h * width) & (lane < (h + 1) * width), 1.0, 0.0).astype(dtype)


def _mod_kernel(c_ref, w_ref, b_ref, o_ref):
    sc = _silu(c_ref[...]).astype(BF16)
    o_ref[0] = _dot(sc, w_ref[0].astype(BF16)) + b_ref[0]


def _modulation(c_all, ada_w, ada_b):
    depth, d, nmod = ada_w.shape
    rows = c_all.shape[0]
    tn = 1536
    return pl.pallas_call(
        _mod_kernel,
        grid=(depth, nmod // tn),
        in_specs=[pl.BlockSpec((rows, d), lambda l, j: (0, 0)),
                  pl.BlockSpec((1, d, tn), lambda l, j: (l, 0, j)),
                  pl.BlockSpec((1, 1, tn), lambda l, j: (l, 0, j))],
        out_specs=pl.BlockSpec((1, rows, tn), lambda l, j: (l, 0, j)),
        out_shape=jax.ShapeDtypeStruct((depth, rows, nmod), F32),
        compiler_params=_cparams("parallel", "parallel"),
        name="adaln_modulation",
    )(c_all, ada_w, ada_b.reshape(depth, 1, nmod))


def _row_specs(width, tile, n_lat_tiles):
    return [pl.BlockSpec((tile, width), lambda i: (jnp.minimum(i, n_lat_tiles - 1), 0)),
            pl.BlockSpec((tile, width), lambda i: (jnp.maximum(i - n_lat_tiles, 0), 0))]


def _pick_rows(lat_ref, ctx_ref, n_lat_tiles):
    return jnp.where(pl.program_id(0) < n_lat_tiles, lat_ref[...], ctx_ref[...])


def _ffn_kernel(*refs, base, hidden, chunk, n_lat_tiles, split_x, n_mix):
    refs = list(refs)
    x = _pick_rows(refs.pop(0), refs.pop(0), n_lat_tiles) if split_x else refs.pop(0)[...]
    mod_ref, g_ref, w13_ref, w2_ref = refs[:4]
    mix_refs, (o_ref, acc_ref) = refs[4:-2], refs[-2:]
    if n_mix:
        wo_ref = mix_refs[-1]
        mixed = None
        for k in range(n_mix):
            y = _pick_rows(mix_refs[2 * k], mix_refs[2 * k + 1], n_lat_tiles)
            part = _dot(y, wo_ref[k * GROUP_W:(k + 1) * GROUP_W, :])
            mixed = part if mixed is None else mixed + part
        x = x + mod_ref[0, 5:6] * mixed
    shift, scale, gate = mod_ref[0, base:base + 1], mod_ref[0, base + 1:base + 2], mod_ref[0, base + 2:base + 3]
    xn = _modnorm(x, g_ref[...], shift, scale).astype(BF16)
    for start in range(0, hidden, chunk):
        size = min(chunk, hidden - start)
        a = _dot(xn, w13_ref[:, start:start + size])
        b = _dot(xn, w13_ref[:, hidden + start:hidden + start + size])
        part = _dot((_silu(a) * b).astype(BF16), w2_ref[start:start + size, :])
        if start == 0:
            acc_ref[...] = part
        else:
            acc_ref[...] += part
    o_ref[...] = x + (0.5 * gate) * acc_ref[...]


def _mod_index(tile, seq, n_batch):
    return lambda i: (jnp.minimum((i * tile) // seq, n_batch), 0, 0)


def _ffn_half(h, mod, g, w13, w2, *, base, n_rows, seq, n_batch, mix=None, w_out=None):
    split_x = isinstance(h, tuple)
    d = w13.shape[0]
    hidden = w2.shape[0]
    tm = ROW_TILE
    n_lat_tiles = n_batch * seq // tm
    kern = functools.partial(_ffn_kernel, base=base, hidden=hidden, chunk=512, n_lat_tiles=n_lat_tiles,
                             split_x=split_x, n_mix=len(mix) if mix else 0)
    x_specs = _row_specs(d, tm, n_lat_tiles) if split_x else [pl.BlockSpec((tm, d), lambda i: (i, 0))]
    mix_args, mix_specs = [], []
    for pair in mix or ():
        mix_args += list(pair)
        mix_specs += _row_specs(GROUP_W, tm, n_lat_tiles)
    if mix:
        mix_args.append(w_out)
        mix_specs.append(_resident(w_out.shape))
    return pl.pallas_call(
        kern,
        grid=(n_rows // tm,),
        in_specs=x_specs + [pl.BlockSpec((1, N_MOD, d), _mod_index(tm, seq, n_batch)),
                            _resident((1, d)), _resident(w13.shape), _resident(w2.shape)] + mix_specs,
        out_specs=pl.BlockSpec((tm, d), lambda i: (i, 0)),
        out_shape=jax.ShapeDtypeStruct((n_rows, d), F32),
        scratch_shapes=[pltpu.VMEM((tm, d), F32)],
        compiler_params=_cparams("parallel"),
        name="swiglu_half_step",
    )(*(h if split_x else (h,)), mod, g.reshape(1, d), w13, w2, *mix_args)


IN_ML, IN_GQA, IN_HY, IN_DIFF, IN_GATE = 1024, 768, 768, 768, 128


def _inproj_kernel(x_ref, mod_ref, g_ref, w_ref, ml_ref, gqa_ref, hy_ref, df_ref, gate_ref):
    xn = _modnorm(x_ref[...], g_ref[...], mod_ref[0, 3:4], mod_ref[0, 4:5]).astype(BF16)
    off = 0
    for ref, width in ((ml_ref, IN_ML), (gqa_ref, IN_GQA), (hy_ref, IN_HY), (df_ref, IN_DIFF), (gate_ref, IN_GATE)):
        ref[...] = _dot(xn, w_ref[:, off:off + width]).astype(ref.dtype)
        off += width


def _in_projection(h, mod, g, w, *, seq, n_batch):
    n_rows, d = h.shape
    tm = ROW_TILE
    widths = (IN_ML, IN_GQA, IN_HY, IN_DIFF, IN_GATE)
    dtypes = (BF16, BF16, BF16, BF16, F32)
    return pl.pallas_call(
        _inproj_kernel,
        grid=(n_rows // tm,),
        in_specs=[pl.BlockSpec((tm, d), lambda i: (i, 0)),
                  pl.BlockSpec((1, N_MOD, d), _mod_index(tm, seq, n_batch)),
                  _resident((1, d)), _resident(w.shape)],
        out_specs=[pl.BlockSpec((tm, wd), lambda i: (i, 0)) for wd in widths],
        out_shape=[jax.ShapeDtypeStruct((n_rows, wd), dt) for wd, dt in zip(widths, dtypes)],
        compiler_params=_cparams("parallel"),
        name="input_projection",
    )(h, mod, g.reshape(1, d), w)


LOG2E = 1.4426950408889634
ATTN_LAG = 2
ATTN_UNROLL = 2


def _ones_lane_variants(v):
    grp = _lane_iota((1, GROUP_W)) // HEAD_DIM
    odd = jnp.where(grp % 2 == 1, 1.0, 0.0).astype(BF16)
    even = jnp.where(grp % 2 == 0, 1.0, 0.0).astype(BF16)
    return v * even + odd, v * odd + even


def _scores_to_probs(qm, k):
    s = _dot_nt(qm, k)
    return jnp.exp2(s - jnp.max(s, axis=-1, keepdims=True)).astype(BF16)


def _probs_times_v(p, v_aug, sum_lane):
    o = _dot(p, v_aug)
    return o * (1.0 / o[:, sum_lane:sum_lane + 1])


def _head_v(h, va, vb):
    return (va, (h + 1) * HEAD_DIM) if h % 2 == 0 else (vb, (h - 1) * HEAD_DIM)


def _attention_stages(q_s, k_s, y_ref, p_s, acc_s, stages, finish, *, ctx, seq, tq):
    n_keys = ctx + seq

    def probs(q, st):
        return _scores_to_probs(q * st[0], k_s[0:n_keys, :])

    def out(p, st):
        return _probs_times_v(p, st[1][0:n_keys, :], st[2]) * st[3]

    p_s[...] = jnp.ones_like(p_s)
    acc_s[...] = jnp.zeros_like(acc_s)
    n, lag = len(stages), p_s.shape[0]

    def tail():
        acc = acc_s[...]
        for c in range(lag):
            acc = acc + out(p_s[c], stages[n - lag + c])
        return finish(acc)

    def body(i, carry):
        q = q_s[pl.ds(pl.multiple_of(ctx + i * tq, tq), tq), :]
        prev = jnp.maximum(i - 1, 0)
        y_ref[pl.ds(pl.multiple_of(prev * tq, tq), tq), :] = tail().astype(y_ref.dtype)
        ps, acc = [], None
        for j in range(n):
            ps.append(probs(q, stages[j]))
            if j >= lag:
                o = out(ps[j - lag], stages[j - lag])
                acc = o if acc is None else acc + o
        for c in range(lag):
            p_s[c] = ps[n - lag + c]
        acc_s[...] = acc
        return carry

    n_tiles = seq // tq
    lax.fori_loop(0, n_tiles, body, 0, unroll=ATTN_UNROLL)
    y_ref[(n_tiles - 1) * tq:n_tiles * tq, :] = tail().astype(y_ref.dtype)


def _context_attention(q, k, stages, finish):
    acc = None
    for st in stages:
        o = _probs_times_v(_scores_to_probs(q * st[0], k), st[1][0:k.shape[0], :], st[2]) * st[3]
        acc = o if acc is None else acc + o
    return finish(acc)


def _gqa_kernel(xl_ref, xc_ref, cos_ref, sin_ref, gq_ref, gk_ref, yl_ref, yc_ref, q_s, k_s, va_s, vb_s, p_s, acc_s,
                *, seq, ctx, tq):
    ones_g = _group_ones(GROUP_W, HEAD_DIM)
    scale = HEAD_DIM ** -0.5 * LOG2E

    def prep(x_ref, rows, dst, rope_rows):
        q = _group_rmsnorm(x_ref[rows, 0:GROUP_W].astype(F32), gq_ref[...], ones_g, HEAD_DIM)
        k = _group_rmsnorm(x_ref[rows, GROUP_W:2 * GROUP_W].astype(F32), gk_ref[...], ones_g, HEAD_DIM)
        if rope_rows is not None:
            cos_t, sin_t = cos_ref[rope_rows, :], sin_ref[rope_rows, :]
            q, k = _rope(q, cos_t, sin_t, HEAD_DIM // 2), _rope(k, cos_t, sin_t, HEAD_DIM // 2)
        q_s[dst, :] = (q * scale).astype(BF16)
        k_s[dst, :] = k.astype(BF16)
        va_s[dst, :], vb_s[dst, :] = _ones_lane_variants(x_ref[rows, 2 * GROUP_W:3 * GROUP_W])

    prep(xc_ref, pl.ds(0, ctx), pl.ds(0, ctx), None)

    def prep_body(i, carry):
        r = pl.ds(pl.multiple_of(i * tq, tq), tq)
        prep(xl_ref, r, pl.ds(pl.multiple_of(ctx + i * tq, tq), tq), r)
        return carry

    lax.fori_loop(0, seq // tq, prep_body, 0)

    stages = [(_head_mask((1, GROUP_W), h, HEAD_DIM, BF16), *_head_v(h, va_s, vb_s),
               _head_mask((1, GROUP_W), h, HEAD_DIM, F32)) for h in range(N_HEADS)]
    finish = lambda acc: acc
    yc_ref[...] = _context_attention(q_s[0:ctx, :], k_s[0:ctx, :], stages, finish).astype(yc_ref.dtype)
    _attention_stages(q_s, k_s, yl_ref, p_s, acc_s, stages, finish, ctx=ctx, seq=seq, tq=tq)


def _gqa_mixer(p_gqa, cos_t, sin_t, gq, gk, *, n_batch, seq, ctx):
    tq = 256
    lat_blocks = n_batch * seq // ctx
    kern = functools.partial(_gqa_kernel, seq=seq, ctx=ctx, tq=tq)
    yl, yc = pl.pallas_call(
        kern,
        grid=(n_batch,),
        in_specs=[pl.BlockSpec((seq, IN_GQA), lambda b: (b, 0)),
                  pl.BlockSpec((ctx, IN_GQA), lambda b: (lat_blocks + b, 0)),
                  _resident(cos_t.shape), _resident(sin_t.shape),
                  _resident((1, GROUP_W)), _resident((1, GROUP_W))],
        out_specs=[pl.BlockSpec((seq, GROUP_W), lambda b: (b, 0)),
                   pl.BlockSpec((ctx, GROUP_W), lambda b: (b, 0))],
        out_shape=[jax.ShapeDtypeStruct((n_batch * seq, GROUP_W), BF16),
                   jax.ShapeDtypeStruct((n_batch * ctx, GROUP_W), BF16)],
        scratch_shapes=[pltpu.VMEM((ctx + seq, GROUP_W), BF16)] * 4
                       + [pltpu.VMEM((ATTN_LAG, tq, ctx + seq), BF16), pltpu.VMEM((tq, GROUP_W), F32)],
        compiler_params=_cparams("parallel"),
        name="gqa_mixer",
    )(p_gqa, p_gqa, cos_t, sin_t, gq, gk)
    return yl, yc


def _diff_kernel(xl_ref, xc_ref, cos_ref, sin_ref, gq_ref, gk_ref, lam_ref, sub_ref, yl_ref, yc_ref,
                 q_s, k_s, va_s, vb_s, p_s, acc_s, *, seq, ctx, tq, lam_init):
    ones_sub = _group_ones(GROUP_W, DIFF_SUB)
    ones_head = _group_ones(GROUP_W, HEAD_DIM)
    scale = DIFF_SUB ** -0.5 * LOG2E
    lp = lam_ref[...]
    lam = (jnp.exp(jnp.sum(lp[0:1] * lp[1:2], axis=-1, keepdims=True))
           - jnp.exp(jnp.sum(lp[2:3] * lp[3:4], axis=-1, keepdims=True)) + lam_init)

    def prep(x_ref, rows, dst, rope_rows):
        q = _group_rmsnorm(x_ref[rows, 0:GROUP_W].astype(F32), gq_ref[...], ones_sub, DIFF_SUB)
        k = _group_rmsnorm(x_ref[rows, GROUP_W:2 * GROUP_W].astype(F32), gk_ref[...], ones_sub, DIFF_SUB)
        if rope_rows is not None:
            cos_t, sin_t = cos_ref[rope_rows, :], sin_ref[rope_rows, :]
            q, k = _rope(q, cos_t, sin_t, DIFF_SUB // 2), _rope(k, cos_t, sin_t, DIFF_SUB // 2)
        q_s[dst, :] = (q * scale).astype(BF16)
        k_s[dst, :] = k.astype(BF16)
        va_s[dst, :], vb_s[dst, :] = _ones_lane_variants(x_ref[rows, 2 * GROUP_W:3 * GROUP_W])

    prep(xc_ref, pl.ds(0, ctx), pl.ds(0, ctx), None)

    def prep_body(i, carry):
        r = pl.ds(pl.multiple_of(i * tq, tq), tq)
        prep(xl_ref, r, pl.ds(pl.multiple_of(ctx + i * tq, tq), tq), r)
        return carry

    lax.fori_loop(0, seq // tq, prep_body, 0)

    stages = []
    for h in range(N_HEADS):
        head = _head_mask((1, GROUP_W), h, HEAD_DIM, F32)
        for j, weight in enumerate((head, -lam * head)):
            stages.append((_head_mask((1, GROUP_W), 2 * h + j, DIFF_SUB, BF16), *_head_v(h, va_s, vb_s), weight))
    finish = lambda acc: _group_rmsnorm(acc, sub_ref[...], ones_head, HEAD_DIM) * (1.0 - lam_init)
    yc_ref[...] = _context_attention(q_s[0:ctx, :], k_s[0:ctx, :], stages, finish).astype(yc_ref.dtype)
    _attention_stages(q_s, k_s, yl_ref, p_s, acc_s, stages, finish, ctx=ctx, seq=seq, tq=tq)


def _diff_mixer(p_df, cos_t, sin_t, gq, gk, lam_p, sub_g, *, lam_init, n_batch, seq, ctx):
    tq = 256
    lat_blocks = n_batch * seq // ctx
    kern = functools.partial(_diff_kernel, seq=seq, ctx=ctx, tq=tq, lam_init=lam_init)
    yl, yc = pl.pallas_call(
        kern,
        grid=(n_batch,),
        in_specs=[pl.BlockSpec((seq, IN_DIFF), lambda b: (b, 0)),
                  pl.BlockSpec((ctx, IN_DIFF), lambda b: (lat_blocks + b, 0)),
                  _resident(cos_t.shape), _resident(sin_t.shape),
                  _resident((1, GROUP_W)), _resident((1, GROUP_W)),
                  _resident(lam_p.shape), _resident((1, GROUP_W))],
        out_specs=[pl.BlockSpec((seq, GROUP_W), lambda b: (b, 0)),
                   pl.BlockSpec((ctx, GROUP_W), lambda b: (b, 0))],
        out_shape=[jax.ShapeDtypeStruct((n_batch * seq, GROUP_W), BF16),
                   jax.ShapeDtypeStruct((n_batch * ctx, GROUP_W), BF16)],
        scratch_shapes=[pltpu.VMEM((ctx + seq, GROUP_W), BF16)] * 4
                       + [pltpu.VMEM((ATTN_LAG, tq, ctx + seq), BF16), pltpu.VMEM((tq, GROUP_W), F32)],
        compiler_params=_cparams("parallel"),
        name="diff_attention_mixer",
    )(p_df, p_df, cos_t, sin_t, gq, gk, lam_p, sub_g)
    return yl, yc


def _dot_split2(x, w_bf16):
    hi = x.astype(BF16)
    lo = (x - hi.astype(F32)).astype(BF16)
    return _dot(hi, w_bf16) + _dot(lo, w_bf16)


def _mlstm_kernel(xl_ref, xc_ref, gl_ref, gc_ref, cw_ref, cb_ref, gb_ref, ng_ref, yl_ref, yc_ref,
                  q_s, k_s, hsum, c_s, n_s, m_s, *, seq, ctx):
    t = ML_CHUNK
    n_chunks = seq // t
    ones_head = _group_ones(GROUP_W, HEAD_DIM)
    block_diag = ones_head.astype(F32)
    r_i = lax.broadcasted_iota(jnp.int32, (t, t), 0)
    c_i = lax.broadcasted_iota(jnp.int32, (t, t), 1)
    tril_b = jnp.where(c_i <= r_i, 1.0, 0.0).astype(BF16)
    causal = (c_i <= r_i, c_i >= r_i)
    row_g = lax.broadcasted_iota(jnp.int32, (t, IN_GATE), 0)
    sel_r = lax.broadcasted_iota(jnp.int32, (IN_GATE, GROUP_W), 0)
    sel_c = lax.broadcasted_iota(jnp.int32, (IN_GATE, GROUP_W), 1) // HEAD_DIM
    select = [jnp.where(sel_r == 8 * d + sel_c, 1.0, 0.0).astype(BF16) for d in range(2)]
    q_masks = [_head_mask((1, GROUP_W), h, HEAD_DIM, BF16) for h in range(N_HEADS)]

    for x_ref, rows, off in ((xc_ref, ctx, 0), (xl_ref, seq, ctx)):
        qk = _silu(_dwconv3(x_ref[:, 0:2 * GROUP_W].astype(F32), cw_ref[...], cb_ref[...]))
        q_s[off:off + rows, :] = (qk[:, 0:GROUP_W] * (HEAD_DIM ** -0.5)).astype(BF16)
        k_s[off:off + rows, :] = qk[:, GROUP_W:2 * GROUP_W].astype(BF16)

    hsum[...] = jnp.zeros_like(hsum)
    c_s[...] = jnp.zeros_like(c_s)
    n_s[...] = jnp.zeros_like(n_s)
    m_s[...] = jnp.zeros_like(m_s)

    def running_max(x, d):
        step = 1
        while step < t:
            if d == 0:
                shifted, valid = pltpu.roll(x, step, 0), row_g >= step
            else:
                shifted, valid = pltpu.roll(x, t - step, 0), row_g < t - step
            x = jnp.maximum(x, jnp.where(valid, shifted, -jnp.inf))
            step *= 2
        return x

    def chunk(d, rows, v, g_raw):
        g = g_raw + gb_ref[...]
        logf = pltpu.roll(jax.nn.log_sigmoid(g), IN_GATE - N_HEADS, 1)
        pre = _dot_exact_rhs(tril_b, logf)
        tot = pre[t - 1:t, :]
        cum = pre if d == 0 else tot - pre + logf
        r = g - cum
        m_prev = m_s[d]
        log_inter = cum + m_prev
        m_t = jnp.maximum(log_inter, cum + running_max(r, d))
        a_t = cum - m_t
        log_g = (tot - cum) + g
        m_new = jnp.maximum(tot + m_prev, jnp.max(log_g, axis=0, keepdims=True))
        m_s[d] = m_new
        rowwise = jnp.concatenate([jnp.exp(log_inter - m_t), jnp.exp(-m_t), jnp.exp(log_g - m_new),
                                   jnp.broadcast_to(jnp.exp(tot + m_prev - m_new), (8, IN_GATE))], axis=0)
        full = _dot_split2(rowwise, select[d])
        w_inter, e_m, w_g, w_c = full[0:t], full[t:2 * t], full[2 * t:3 * t], full[3 * t:3 * t + 1]

        q, k = q_s[rows, :], k_s[rows, :]
        r_t = jnp.transpose(r)
        pair = [None, None]
        for h in range(N_HEADS):
            col = 8 * d + h
            w_intra = jnp.exp(jnp.where(causal[d], a_t[:, col:col + 1] + r_t[col:col + 1, :], -jnp.inf))
            s = (_dot_nt(q * q_masks[h], k) * w_intra).astype(BF16)
            o = _dot(s, v * q_masks[h] + q_masks[(h + 2) % N_HEADS])
            pair[h // 2] = o if pair[h // 2] is None else pair[h // 2] + o
        half = GROUP_W // 2
        num_i = jnp.concatenate([pair[0][:, :half], pair[1][:, half:]], axis=1)
        den_i = jnp.concatenate([pair[0][:, half:], pair[1][:, :half]], axis=1)
        ct, n_full = c_s[d], n_s[d]
        inter = _dot(q, ct.astype(BF16))
        qn = _dot((q.astype(F32) * n_full).astype(BF16), ones_head)
        num = w_inter * inter + num_i
        den = w_inter * qn + den_i
        kw = k.astype(F32) * w_g
        c_s[d] = w_c * ct + _dot_tn(kw.astype(BF16), v) * block_diag
        n_s[d] = w_c * n_full + jnp.sum(kw, axis=0, keepdims=True)
        return num / jnp.maximum(jnp.abs(den), e_m)

    assert ctx == t
    v_c = xc_ref[:, 2 * GROUP_W:3 * GROUP_W]
    hsum[0:t, :] = chunk(0, pl.ds(0, t), v_c, gc_ref[...]) + chunk(1, pl.ds(0, t), v_c, gc_ref[...])

    def body(j, carry):
        for d in range(2):
            cidx = j if d == 0 else n_chunks - 1 - j
            lat = pl.ds(pl.multiple_of(cidx * t, t), t)
            rows = pl.ds(pl.multiple_of(ctx + cidx * t, t), t)
            hsum[rows, :] += chunk(d, rows, xl_ref[lat, 2 * GROUP_W:3 * GROUP_W], gl_ref[lat, :])
        return carry

    lax.fori_loop(0, n_chunks, body, 0)

    for x_ref, y_ref, rows, off in ((xc_ref, yc_ref, ctx, 0), (xl_ref, yl_ref, seq, ctx)):
        hn = _group_rmsnorm(hsum[off:off + rows, :], ng_ref[...], ones_head, HEAD_DIM)
        y_ref[...] = (hn * jax.nn.sigmoid(x_ref[:, 3 * GROUP_W:4 * GROUP_W].astype(F32))).astype(y_ref.dtype)


def _mlstm_mixer(p_ml, p_gate, conv_w, conv_b, gate_b, norm_g, *, n_batch, seq, ctx):
    lat_blocks = n_batch * seq // ctx
    kern = functools.partial(_mlstm_kernel, seq=seq, ctx=ctx)
    yl, yc = pl.pallas_call(
        kern,
        grid=(n_batch,),
        in_specs=[pl.BlockSpec((seq, IN_ML), lambda b: (b, 0)),
                  pl.BlockSpec((ctx, IN_ML), lambda b: (lat_blocks + b, 0)),
                  pl.BlockSpec((seq, IN_GATE), lambda b: (b, 0)),
                  pl.BlockSpec((ctx, IN_GATE), lambda b: (lat_blocks + b, 0)),
                  _resident(conv_w.shape), _resident(conv_b.shape), _resident(gate_b.shape), _resident(norm_g.shape)],
        out_specs=[pl.BlockSpec((seq, GROUP_W), lambda b: (b, 0)),
                   pl.BlockSpec((ctx, GROUP_W), lambda b: (b, 0))],
        out_shape=[jax.ShapeDtypeStruct((n_batch * seq, GROUP_W), BF16),
                   jax.ShapeDtypeStruct((n_batch * ctx, GROUP_W), BF16)],
        scratch_shapes=[pltpu.VMEM((ctx + seq, GROUP_W), BF16), pltpu.VMEM((ctx + seq, GROUP_W), BF16),
                        pltpu.VMEM((ctx + seq, GROUP_W), F32),
                        pltpu.VMEM((2, GROUP_W, GROUP_W), F32), pltpu.VMEM((2, 1, GROUP_W), F32),
                        pltpu.VMEM((2, 1, IN_GATE), F32)],
        compiler_params=_cparams("parallel"),
        name="mlstm_mixer",
    )(p_ml, p_ml, p_gate, p_gate, conv_w, conv_b, gate_b, norm_g)
    return yl, yc


def _phase_tables(idx_a, idx_b, n):
    ang = (2.0 * math.pi / n) * ((idx_a[:, None] * idx_b[None, :]) % n).astype(F32)
    return jnp.cos(ang), jnp.sin(ang)


def _dft_matrices(length, n):
    nf, lo = n // 2, 64
    k = jnp.arange(nf, dtype=jnp.int32)
    k1, k0 = lo * jnp.arange(nf // lo, dtype=jnp.int32), jnp.arange(lo, dtype=jnp.int32)
    j = jnp.arange(length, dtype=jnp.int32)
    m = j + length // 2
    (ca, sa), (cb, sb) = _phase_tables(k1, j, n), _phase_tables(k0, j, n)
    cos_f = (ca[:, None, :] * cb[None, :, :] - sa[:, None, :] * sb[None, :, :]).reshape(nf, length)
    sin_f = (sa[:, None, :] * cb[None, :, :] + ca[:, None, :] * sb[None, :, :]).reshape(nf, length)
    alt = jnp.where(j % 2 == 0, 1.0, -1.0).astype(F32)
    fwd = jnp.concatenate([cos_f, jnp.where(k[:, None] == 0, alt[None, :], sin_f)], axis=0)
    (ca, sa), (cb, sb) = _phase_tables(m, k1, n), _phase_tables(m, k0, n)
    cos_i = (ca[:, :, None] * cb[:, None, :] - sa[:, :, None] * sb[:, None, :]).reshape(length, nf)
    sin_i = (sa[:, :, None] * cb[:, None, :] + ca[:, :, None] * sb[:, None, :]).reshape(length, nf)
    wk = jnp.where(k == 0, 1.0, 2.0).astype(F32) / n
    alt_m = jnp.where(m % 2 == 0, 1.0, -1.0).astype(F32) / n
    inv = jnp.concatenate([cos_i * wk[None, :], jnp.where(k[None, :] == 0, alt_m[:, None], sin_i * (2.0 / n))], axis=1)
    return fwd.astype(BF16), inv.astype(BF16)


def _hyena_filter_consts(length):
    t = jnp.arange(length, dtype=F32)
    tn = t / length
    bands = jnp.arange(1, HY_POS_BANDS + 1, dtype=F32)
    ang = 2.0 * math.pi * tn[:, None] * bands
    feats = jnp.concatenate([tn[:, None], jnp.cos(ang), jnp.sin(ang)], axis=-1)
    feats = jnp.pad(feats, ((0, 0), (0, 128 - feats.shape[1])))
    dist = jnp.abs(t - length // 2) / (length / 2)
    deltas = jnp.abs(jnp.linspace(math.log(HY_DECAY_TARGET) / HY_SLOW_DECAY,
                                  math.log(HY_DECAY_TARGET) / HY_FAST_DECAY, GROUP_W, dtype=F32))
    window = jnp.exp(-dist[:, None] * jnp.tile(deltas, 2))
    return feats, window


def _filter_kernel(feats_ref, win_ref, w1_ref, b1_ref, w2_ref, b2_ref, w3_ref, b3_ref, f_ref, p_ref, qz_ref, p2_ref):
    h = jnp.sin(_dot_f32ish(feats_ref[...], w1_ref[0]) + b1_ref[0])
    h = jnp.sin(_dot_f32ish(h, w2_ref[0]) + b2_ref[0])
    h = (_dot_f32ish(h, w3_ref[0]) + b3_ref[0]) * win_ref[...]
    h = h / jnp.sum(jnp.abs(h), axis=0, keepdims=True)
    h1, h2, h3 = _split3(h)
    nf = f_ref.shape[0] // 2
    step = math.gcd(nf, 512)
    for r in range(0, nf, step):
        fc, fs = f_ref[r:r + step, :], f_ref[nf + r:nf + r + step, :]
        p = _dot(fc, h1) + _dot(fc, h2) + _dot(fc, h3)
        q = _dot(fs, h1) + _dot(fs, h2) + _dot(fs, h3)
        p_ref[0, r:r + step, :] = p
        if r == 0:
            first = lax.broadcasted_iota(jnp.int32, p.shape, 0) == 0
            qz_ref[0, r:r + step, :] = jnp.where(first, 0.0, q)
            p2_ref[0, r:r + step, :] = jnp.where(first, q, p)
        else:
            qz_ref[0, r:r + step, :] = q
            p2_ref[0, r:r + step, :] = p


def _hyena_filter_spectra(length, fwd, w1, b1, w2, b2, w3, b3):
    depth = w1.shape[0]
    feats, window = _hyena_filter_consts(length)
    nf = fwd.shape[0] // 2
    pad2 = lambda a, r, c: jnp.pad(a, ((0, 0), (0, r - a.shape[1]), (0, c - a.shape[2])))
    w1p, w2p = pad2(w1, 128, 128), pad2(w2, 128, 128)
    w3p = pad2(w3, 128, w3.shape[2])
    b1p = jnp.pad(b1, ((0, 0), (0, 128 - b1.shape[1])))[:, None, :]
    b2p = jnp.pad(b2, ((0, 0), (0, 128 - b2.shape[1])))[:, None, :]
    b3p = b3[:, None, :]
    lay = lambda a: pl.BlockSpec((1,) + a.shape[1:], lambda l, o: (l, 0, 0))
    per_order = lambda rows: pl.BlockSpec((1, rows, GROUP_W), lambda l, o: (l, 0, o))
    out = jax.ShapeDtypeStruct((depth, nf, 2 * GROUP_W), F32)
    return pl.pallas_call(
        _filter_kernel,
        grid=(depth, 2),
        in_specs=[_resident(feats.shape), pl.BlockSpec((length, GROUP_W), lambda l, o: (0, o)),
                  lay(w1p), lay(b1p), lay(w2p), lay(b2p), per_order(128), per_order(1), _resident(fwd.shape)],
        out_specs=[per_order(nf)] * 3,
        out_shape=[out, out, out],
        compiler_params=_cparams("parallel", "parallel"),
        name="hyena_filter_spectra",
    )(feats, window, w1p, b1p, w2p, b2p, w3p, b3p, fwd)


def _hyena_conv_kernel(sig_ref, gate_ref, cws_ref, cbs_ref, cwg_ref, cbg_ref, skip_ref, f_ref, g_ref,
                       p_ref, qz_ref, p2_ref, o_ref, *, sig_conv):
    s = sig_ref[...].astype(F32)
    if sig_conv:
        s = _dwconv3(s, cws_ref[...], cbs_ref[...])
    gate = _dwconv3(gate_ref[...].astype(F32), cwg_ref[...], cbg_ref[...])
    sb = s.astype(BF16)
    nf = f_ref.shape[0] // 2
    a = _dot(f_ref[0:nf, :], sb)
    b = _dot(f_ref[nf:2 * nf, :], sb)
    ay = a * p_ref[0] - b * qz_ref[0]
    by = a * qz_ref[0] + b * p2_ref[0]
    y = _dot(g_ref[:, 0:nf], ay.astype(BF16)) + _dot(g_ref[:, nf:2 * nf], by.astype(BF16))
    o_ref[...] = (gate * (y + skip_ref[...] * s)).astype(o_ref.dtype)


def _hyena_conv(sig, sig_col, gate, gate_col, conv_w, conv_b, skip, fwd, inv, spectra, order, layer,
                *, rows, first_block, n_batch, out_dtype, sig_conv):
    cw = lambda col: pl.BlockSpec((3, GROUP_W), lambda b: (0, col))
    cb = lambda col: pl.BlockSpec((1, GROUP_W), lambda b: (0, col))
    spec = pl.BlockSpec((1, fwd.shape[0] // 2, GROUP_W), lambda b: (layer, 0, order), pipeline_mode=pl.Buffered(1))
    kern = functools.partial(_hyena_conv_kernel, sig_conv=sig_conv)
    return pl.pallas_call(
        kern,
        grid=(n_batch,),
        in_specs=[pl.BlockSpec((rows, GROUP_W), lambda b: (first_block + b, sig_col)),
                  pl.BlockSpec((rows, GROUP_W), lambda b: (first_block + b, gate_col)),
                  cw(sig_col if sig_conv else 0), cb(sig_col if sig_conv else 0), cw(gate_col), cb(gate_col),
                  pl.BlockSpec((1, GROUP_W), lambda b: (0, 0)),
                  _resident(fwd.shape), _resident(inv.shape), spec, spec, spec],
        out_specs=pl.BlockSpec((rows, GROUP_W), lambda b: (b, 0)),
        out_shape=jax.ShapeDtypeStruct((n_batch * rows, GROUP_W), out_dtype),
        compiler_params=_cparams("parallel"),
        name="hyena_long_conv",
    )(sig, gate, conv_w, conv_b, conv_w, conv_b, skip, fwd, inv, *spectra)


def _hyena_mixer(p_hy, conv_w, conv_b, skip, dft, spectra, layer, *, rows, first_block, n_batch):
    fwd, inv = dft
    common = dict(rows=rows, n_batch=n_batch)
    z = _hyena_conv(p_hy, 0, p_hy, 1, conv_w, conv_b, skip[0:1], fwd, inv, spectra, 0, layer,
                    first_block=first_block, out_dtype=F32, sig_conv=True, **common)
    cw = lambda col: pl.BlockSpec((3, GROUP_W), lambda b: (0, col))
    cb = lambda col: pl.BlockSpec((1, GROUP_W), lambda b: (0, col))
    spec = pl.BlockSpec((1, fwd.shape[0] // 2, GROUP_W), lambda b: (layer, 0, 1), pipeline_mode=pl.Buffered(1))
    kern = functools.partial(_hyena_conv_kernel, sig_conv=False)
    return pl.pallas_call(
        kern,
        grid=(n_batch,),
        in_specs=[pl.BlockSpec((rows, GROUP_W), lambda b: (b, 0)),
                  pl.BlockSpec((rows, GROUP_W), lambda b: (first_block + b, 2)),
                  cw(0), cb(0), cw(2), cb(2),
                  pl.BlockSpec((1, GROUP_W), lambda b: (0, 0)),
                  _resident(fwd.shape), _resident(inv.shape), spec, spec, spec],
        out_specs=pl.BlockSpec((rows, GROUP_W), lambda b: (b, 0)),
        out_shape=jax.ShapeDtypeStruct((n_batch * rows, GROUP_W), BF16),
        compiler_params=_cparams("parallel"),
        name="hyena_long_conv_2",
    )(z, p_hy, conv_w, conv_b, conv_w, conv_b, skip[1:2], fwd, inv, *spectra)


def _rope_tables(length, dim, reps):
    rows = length // GRID_W
    row = jnp.repeat(jnp.arange(rows), GRID_W).astype(F32)
    col = jnp.tile(jnp.arange(GRID_W), rows).astype(F32)
    n_freq = dim // 4
    inv = ROPE_THETA ** (-jnp.arange(n_freq, dtype=F32) / n_freq)
    ang = jnp.concatenate([row[:, None] * inv, col[:, None] * inv], axis=-1)
    cos, sin = jnp.cos(ang), jnp.sin(ang)
    return (jnp.tile(jnp.concatenate([cos, cos], axis=-1), (1, reps)),
            jnp.tile(jnp.concatenate([-sin, sin], axis=-1), (1, reps)))


def _in_weight_layout(w_in):
    ml, n_gate = 4 * GROUP_W, 16
    o_gqa = ml + n_gate
    o_hy = o_gqa + 2 * GROUP_W
    o_df = o_hy + 3 * GROUP_W
    rep = lambda start: jnp.concatenate([w_in[..., start + (h // 2) * HEAD_DIM: start + (h // 2 + 1) * HEAD_DIM]
                                         for h in range(N_HEADS)], axis=-1)
    gates = jnp.pad(w_in[..., ml:ml + n_gate], ((0, 0), (0, 0), (0, IN_GATE - n_gate)))
    return jnp.concatenate([w_in[..., 0:ml],
                            w_in[..., o_gqa:o_gqa + GROUP_W], rep(o_gqa + GROUP_W), rep(o_gqa + GROUP_W + 2 * HEAD_DIM),
                            w_in[..., o_hy:o_hy + 3 * GROUP_W],
                            w_in[..., o_df:o_df + 3 * GROUP_W],
                            gates], axis=-1).astype(BF16)


def kernel(x, c, ctx, c_ctx, ada_w, ada_b, norm_g, ffn_w13, ffn_w2, w_in, w_out, ml_gate_b, ml_conv_w, ml_conv_b,
           ml_norm_g, gqa_qk_g, hy_conv_w, hy_conv_b, hy_filt_w1, hy_filt_b1, hy_filt_w2, hy_filt_b2, hy_filt_w3,
           hy_filt_b3, hy_skip, diff_qk_g, diff_lambda, diff_subln_g):
    n_batch, seq, d = x.shape
    n_ctx = ctx.shape[1]
    depth = ada_w.shape[0]
    n_lat = n_batch * seq
    n_all = n_lat + n_batch * n_ctx
    assert n_batch * n_ctx == seq and n_ctx == ML_CHUNK and seq % ROW_TILE == 0

    c_all = jnp.pad(jnp.concatenate([c, c_ctx[None]], axis=0), ((0, 16 - n_batch - 1), (0, 0)))
    mod = _modulation(c_all, ada_w, ada_b)[:, :n_batch + 1].reshape(depth, n_batch + 1, N_MOD, d)

    w13 = ffn_w13.astype(BF16)
    w2 = ffn_w2.astype(BF16)
    w_in_l = _in_weight_layout(w_in)
    w_out_b = w_out.astype(BF16)

    rope_gqa = _rope_tables(seq, HEAD_DIM, N_HEADS)
    rope_diff = _rope_tables(seq, DIFF_SUB, 2 * N_HEADS)
    dft_l = _dft_matrices(seq, 3 * seq // 2)
    dft_c = _dft_matrices(n_ctx, 2 * n_ctx)
    filt = (hy_filt_w1, hy_filt_b1, hy_filt_w2, hy_filt_b2, hy_filt_w3, hy_filt_b3)
    spec_l = _hyena_filter_spectra(seq, dft_l[0], *filt)
    spec_c = _hyena_filter_spectra(n_ctx, dft_c[0], *filt)

    h = (x.reshape(n_lat, d), ctx.reshape(n_batch * n_ctx, d))
    dims = dict(seq=seq, n_batch=n_batch)
    for l in range(depth):
        need_ctx = l < depth - 1
        lam_init = 0.8 - 0.6 * math.exp(-0.3 * l)
        h = _ffn_half(h, mod[l], norm_g[l, 0], w13[l, 0], w2[l, 0], base=0, n_rows=n_all, **dims)
        p_ml, p_gqa, p_hy, p_df, p_gate = _in_projection(h, mod[l], norm_g[l, 1], w_in_l[l], **dims)

        gate_b = jnp.pad(ml_gate_b[l], (0, IN_GATE - ml_gate_b.shape[1]))[None]
        y_ml = _mlstm_mixer(p_ml, p_gate, ml_conv_w[l], ml_conv_b[l][None], gate_b, ml_norm_g[l][None],
                            n_batch=n_batch, seq=seq, ctx=n_ctx)
        tile_g = lambda g, reps: jnp.tile(g, reps)[None]
        y_gqa = _gqa_mixer(p_gqa, *rope_gqa, tile_g(gqa_qk_g[l, 0], N_HEADS), tile_g(gqa_qk_g[l, 1], N_HEADS),
                           n_batch=n_batch, seq=seq, ctx=n_ctx)
        y_df = _diff_mixer(p_df, *rope_diff, tile_g(diff_qk_g[l, 0], 2 * N_HEADS), tile_g(diff_qk_g[l, 1], 2 * N_HEADS),
                           diff_lambda[l], tile_g(diff_subln_g[l], N_HEADS),
                           lam_init=lam_init, n_batch=n_batch, seq=seq, ctx=n_ctx)
        y_hy = _hyena_mixer(p_hy, hy_conv_w[l], hy_conv_b[l][None], hy_skip[l], dft_l, spec_l, l,
                            rows=seq, first_block=0, n_batch=n_batch)
        y_hy_c = y_hy if not need_ctx else _hyena_mixer(
            p_hy, hy_conv_w[l], hy_conv_b[l][None], hy_skip[l], dft_c, spec_c, l,
            rows=n_ctx, first_block=n_lat // n_ctx, n_batch=n_batch)
        h = _ffn_half(h, mod[l], norm_g[l, 2], w13[l, 1], w2[l, 1], base=6, n_rows=n_all if need_ctx else n_lat,
                      mix=[y_ml, y_gqa, (y_hy, y_hy_c), y_df], w_out=w_out_b[l], **dims)
    return h[:n_lat].reshape(n_batch, seq, d)
```

```python
import functools
import math

import jax
import jax.numpy as jnp
from jax import lax
from jax.experimental import pallas as pl
from jax.experimental.pallas import tpu as pltpu

F32 = jnp.float32
BF16 = jnp.bfloat16

EPS = 1e-6
ROPE_THETA = 10000.0
GRID_W = 64
N_MOD = 9
GROUP_W = 256
HEAD_DIM = 64
N_HEADS = 4
DIFF_SUB = 32
ML_CHUNK = 256
HY_POS_BANDS = 16
HY_DECAY_TARGET = 1e-2
HY_FAST_DECAY = 0.3
HY_SLOW_DECAY = 1.5
ROW_TILE = 512
VMEM_LIMIT = 56 * 1024 * 1024


def _cparams(*sem):
    return pltpu.CompilerParams(dimension_semantics=sem, vmem_limit_bytes=VMEM_LIMIT)


def _resident(shape):
    nd = len(shape)
    return pl.BlockSpec(shape, lambda *_: (0,) * nd, pipeline_mode=pl.Buffered(1))


def _resident_slice(arr, index):
    tail = arr.shape[len(index):]
    return pl.BlockSpec((None,) * len(index) + tail, lambda *_: tuple(index) + (0,) * len(tail),
                        pipeline_mode=pl.Buffered(1))


def _dot(a, b):
    return jnp.dot(a, b, preferred_element_type=F32)


def _dot_nt(a, b):
    return lax.dot_general(a, b, (((1,), (1,)), ((), ())), preferred_element_type=F32)


def _dot_tn(a, b):
    return lax.dot_general(a, b, (((0,), (0,)), ((), ())), preferred_element_type=F32)


def _split3(x):
    x1 = x.astype(BF16)
    r = x - x1.astype(F32)
    x2 = r.astype(BF16)
    x3 = (r - x2.astype(F32)).astype(BF16)
    return x1, x2, x3


def _dot_exact_rhs(a_bf16, x):
    x1, x2, x3 = _split3(x)
    return _dot(a_bf16, x1) + _dot(a_bf16, x2) + _dot(a_bf16, x3)


def _dot_f32ish(a, b):
    a1 = a.astype(BF16)
    a2 = (a - a1.astype(F32)).astype(BF16)
    b1 = b.astype(BF16)
    b2 = (b - b1.astype(F32)).astype(BF16)
    return _dot(a1, b1) + _dot(a1, b2) + _dot(a2, b1)


def _silu(x):
    return x * jax.nn.sigmoid(x)


def _modnorm(x, g, shift, scale):
    y = x * lax.rsqrt(jnp.mean(x * x, axis=-1, keepdims=True) + EPS)
    return (y * g) * (1.0 + scale) + shift


def _lane_iota(shape):
    return lax.broadcasted_iota(jnp.int32, shape, len(shape) - 1)


def _group_ones(n, group):
    r = lax.broadcasted_iota(jnp.int32, (n, n), 0) // group
    c = lax.broadcasted_iota(jnp.int32, (n, n), 1) // group
    return jnp.where(r == c, 1.0, 0.0).astype(BF16)


def _group_rmsnorm(x, gain, ones_g, group):
    x2 = x * x
    hi = x2.astype(BF16)
    lo = (x2 - hi.astype(F32)).astype(BF16)
    ms = (_dot(hi, ones_g) + _dot(lo, ones_g)) * (1.0 / group)
    return x * lax.rsqrt(ms + EPS) * gain


def _rope(x, cos_t, sin_t, half):
    n = x.shape[-1]
    left = pltpu.roll(x, n - half, 1)
    right = pltpu.roll(x, half, 1)
    sw = jnp.where((_lane_iota(x.shape) & (2 * half - 1)) < half, left, right)
    return x * cos_t + sw * sin_t


def _dwconv3(x, w, b):
    rows = x.shape[0]
    row = lax.broadcasted_iota(jnp.int32, x.shape, 0)
    xm = jnp.where(row == 0, 0.0, pltpu.roll(x, 1, 0))
    xp = jnp.where(row == rows - 1, 0.0, pltpu.roll(x, rows - 1, 0))
    return xm * w[0:1] + x * w[1:2] + xp * w[2:3] + b


def _head_mask(shape, h, width, dtype):
    lane = _lane_iota(shape)
    return jnp.where((lane >= h * width) & (lane < (h + 1) * width), 1.0, 0.0).astype(dtype)


def _mod_kernel(c_ref, w_ref, b_ref, o_ref):
    sc = _silu(c_ref[...]).astype(BF16)
    o_ref[0] = _dot(sc, w_ref[0].astype(BF16)) + b_ref[0]


def _modulation(c_all, ada_w, ada_b):
    depth, d, nmod = ada_w.shape
    rows = c_all.shape[0]
    tn = 1536
    return pl.pallas_call(
        _mod_kernel,
        grid=(depth, nmod // tn),
        in_specs=[pl.BlockSpec((rows, d), lambda l, j: (0, 0)),
                  pl.BlockSpec((1, d, tn), lambda l, j: (l, 0, j)),
                  pl.BlockSpec((1, 1, tn), lambda l, j: (l, 0, j))],
        out_specs=pl.BlockSpec((1, rows, tn), lambda l, j: (l, 0, j)),
        out_shape=jax.ShapeDtypeStruct((depth, rows, nmod), F32),
        compiler_params=_cparams("parallel", "parallel"),
        name="adaln_modulation",
    )(c_all, ada_w, ada_b.reshape(depth, 1, nmod))


def _row_specs(width, tile, n_lat_tiles):
    return [pl.BlockSpec((tile, width), lambda i: (jnp.minimum(i, n_lat_tiles - 1), 0)),
            pl.BlockSpec((tile, width), lambda i: (jnp.maximum(i - n_lat_tiles, 0), 0))]


def _pick_rows(lat_ref, ctx_ref, n_lat_tiles):
    return jnp.where(pl.program_id(0) < n_lat_tiles, lat_ref[...], ctx_ref[...])


def _ffn_kernel(*refs, base, hidden, chunk, n_lat_tiles, split_x, n_mix):
    refs = list(refs)
    x = _pick_rows(refs.pop(0), refs.pop(0), n_lat_tiles) if split_x else refs.pop(0)[...]
    mod_ref, g_ref, w13_ref, w2_ref = refs[:4]
    mix_refs, (o_ref, acc_ref) = refs[4:-2], refs[-2:]
    if n_mix:
        wo_ref = mix_refs[-1]
        mixed = None
        for k in range(n_mix):
            y = _pick_rows(mix_refs[2 * k], mix_refs[2 * k + 1], n_lat_tiles)
            part = _dot(y, wo_ref[k * GROUP_W:(k + 1) * GROUP_W, :])
            mixed = part if mixed is None else mixed + part
        x = x + mod_ref[0, 5:6] * mixed
    shift, scale, gate = mod_ref[0, base:base + 1], mod_ref[0, base + 1:base + 2], mod_ref[0, base + 2:base + 3]
    xn = _modnorm(x, g_ref[...], shift, scale).astype(BF16)
    for start in range(0, hidden, chunk):
        size = min(chunk, hidden - start)
        a = _dot(xn, w13_ref[:, start:start + size])
        b = _dot(xn, w13_ref[:, hidden + start:hidden + start + size])
        part = _dot((_silu(a) * b).astype(BF16), w2_ref[start:start + size, :])
        if start == 0:
            acc_ref[...] = part
        else:
            acc_ref[...] += part
    o_ref[...] = x + (0.5 * gate) * acc_ref[...]


def _mod_index(tile, seq, n_batch):
    return lambda i: (jnp.minimum((i * tile) // seq, n_batch), 0, 0)


def _ffn_half(h, mod, g, w13, w2, *, layer, half, base, n_rows, seq, n_batch, mix=None, w_out=None,
              tm=ROW_TILE, chunk=512):
    split_x = isinstance(h, tuple)
    d = w13.shape[-2]
    hidden = w2.shape[-2]
    n_lat_tiles = n_batch * seq // tm
    kern = functools.partial(_ffn_kernel, base=base, hidden=hidden, chunk=chunk, n_lat_tiles=n_lat_tiles,
                             split_x=split_x, n_mix=len(mix) if mix else 0)
    x_specs = _row_specs(d, tm, n_lat_tiles) if split_x else [pl.BlockSpec((tm, d), lambda i: (i, 0))]
    mix_args, mix_specs = [], []
    for pair in mix or ():
        mix_args += list(pair)
        mix_specs += _row_specs(GROUP_W, tm, n_lat_tiles)
    if mix:
        mix_args.append(w_out)
        mix_specs.append(_resident_slice(w_out, (layer,)))
    return pl.pallas_call(
        kern,
        grid=(n_rows // tm,),
        in_specs=x_specs + [pl.BlockSpec((1, N_MOD, d), _mod_index(tm, seq, n_batch)),
                            _resident((1, d)), _resident_slice(w13, (layer, half)),
                            _resident_slice(w2, (layer, half))] + mix_specs,
        out_specs=pl.BlockSpec((tm, d), lambda i: (i, 0)),
        out_shape=jax.ShapeDtypeStruct((n_rows, d), F32),
        scratch_shapes=[pltpu.VMEM((tm, d), F32)],
        compiler_params=_cparams("parallel"),
        name="swiglu_half_step",
    )(*(h if split_x else (h,)), mod, g.reshape(1, d), w13, w2, *mix_args)


IN_ML, IN_GQA, IN_HY, IN_DIFF, IN_GATE = 1024, 768, 768, 768, 128


def _inproj_kernel(x_ref, mod_ref, g_ref, w_ref, ml_ref, gqa_ref, hy_ref, df_ref, gate_ref):
    xn = _modnorm(x_ref[...], g_ref[...], mod_ref[0, 3:4], mod_ref[0, 4:5]).astype(BF16)
    off = 0
    for ref, width in ((ml_ref, IN_ML), (gqa_ref, IN_GQA), (hy_ref, IN_HY), (df_ref, IN_DIFF), (gate_ref, IN_GATE)):
        ref[...] = _dot(xn, w_ref[:, off:off + width]).astype(ref.dtype)
        off += width


def _in_projection(h, mod, g, w, *, layer, seq, n_batch, tm=ROW_TILE):
    n_rows, d = h.shape
    widths = (IN_ML, IN_GQA, IN_HY, IN_DIFF, IN_GATE)
    dtypes = (BF16, BF16, BF16, BF16, F32)
    return pl.pallas_call(
        _inproj_kernel,
        grid=(n_rows // tm,),
        in_specs=[pl.BlockSpec((tm, d), lambda i: (i, 0)),
                  pl.BlockSpec((1, N_MOD, d), _mod_index(tm, seq, n_batch)),
                  _resident((1, d)), _resident_slice(w, (layer,))],
        out_specs=[pl.BlockSpec((tm, wd), lambda i: (i, 0)) for wd in widths],
        out_shape=[jax.ShapeDtypeStruct((n_rows, wd), dt) for wd, dt in zip(widths, dtypes)],
        compiler_params=_cparams("parallel"),
        name="input_projection",
    )(h, mod, g.reshape(1, d), w)


LOG2E = 1.4426950408889634
ATTN_LAG = 2
ATTN_UNROLL = 1
ATTN_TQ = 512


def _ones_lane_variants(v):
    grp = _lane_iota((1, GROUP_W)) // HEAD_DIM
    odd = jnp.where(grp % 2 == 1, 1.0, 0.0).astype(BF16)
    even = jnp.where(grp % 2 == 0, 1.0, 0.0).astype(BF16)
    return v * even + odd, v * odd + even


def _scores_to_probs(qm, k):
    s = _dot_nt(qm, k)
    return jnp.exp2(s - jnp.max(s, axis=-1, keepdims=True)).astype(BF16)


def _probs_times_v(p, v_aug, sum_lane):
    o = _dot(p, v_aug)
    return o * (1.0 / o[:, sum_lane:sum_lane + 1])


def _head_v(h, va, vb):
    return (va, (h + 1) * HEAD_DIM) if h % 2 == 0 else (vb, (h - 1) * HEAD_DIM)


def _attention_stages(q_s, k_s, y_ref, p_s, acc_s, stages, finish, *, ctx, seq, tq, unroll):
    n_keys = ctx + seq

    def probs(q, st):
        return _scores_to_probs(q * st[0], k_s[0:n_keys, :])

    def out(p, st):
        return _probs_times_v(p, st[1][0:n_keys, :], st[2]) * st[3]

    p_s[...] = jnp.ones_like(p_s)
    acc_s[...] = jnp.zeros_like(acc_s)
    n, lag = len(stages), p_s.shape[0]

    def tail():
        acc = acc_s[...]
        for c in range(lag):
            acc = acc + out(p_s[c], stages[n - lag + c])
        return finish(acc)

    def body(i, carry):
        q = q_s[pl.ds(pl.multiple_of(ctx + i * tq, tq), tq), :]
        prev = jnp.maximum(i - 1, 0)
        y_ref[pl.ds(pl.multiple_of(prev * tq, tq), tq), :] = tail().astype(y_ref.dtype)
        ps, acc = [], None
        for j in range(n):
            ps.append(probs(q, stages[j]))
            if j >= lag:
                o = out(ps[j - lag], stages[j - lag])
                acc = o if acc is None else acc + o
        for c in range(lag):
            p_s[c] = ps[n - lag + c]
        acc_s[...] = acc
        return carry

    n_tiles = seq // tq
    lax.fori_loop(0, n_tiles, body, 0, unroll=unroll)
    y_ref[(n_tiles - 1) * tq:n_tiles * tq, :] = tail().astype(y_ref.dtype)


def _context_attention(q, k, stages, finish):
    acc = None
    for st in stages:
        o = _probs_times_v(_scores_to_probs(q * st[0], k), st[1][0:k.shape[0], :], st[2]) * st[3]
        acc = o if acc is None else acc + o
    return finish(acc)


def _gqa_kernel(xl_ref, xc_ref, cos_ref, sin_ref, gq_ref, gk_ref, yl_ref, yc_ref, q_s, k_s, va_s, vb_s, p_s, acc_s,
                *, seq, ctx, tq, unroll):
    ones_g = _group_ones(GROUP_W, HEAD_DIM)
    scale = HEAD_DIM ** -0.5 * LOG2E

    def prep(x_ref, rows, dst, rope_rows):
        q = _group_rmsnorm(x_ref[rows, 0:GROUP_W].astype(F32), gq_ref[...], ones_g, HEAD_DIM)
        k = _group_rmsnorm(x_ref[rows, GROUP_W:2 * GROUP_W].astype(F32), gk_ref[...], ones_g, HEAD_DIM)
        if rope_rows is not None:
            cos_t, sin_t = cos_ref[rope_rows, :], sin_ref[rope_rows, :]
            q, k = _rope(q, cos_t, sin_t, HEAD_DIM // 2), _rope(k, cos_t, sin_t, HEAD_DIM // 2)
        q_s[dst, :] = (q * scale).astype(BF16)
        k_s[dst, :] = k.astype(BF16)
        va_s[dst, :], vb_s[dst, :] = _ones_lane_variants(x_ref[rows, 2 * GROUP_W:3 * GROUP_W])

    prep(xc_ref, pl.ds(0, ctx), pl.ds(0, ctx), None)

    def prep_body(i, carry):
        r = pl.ds(pl.multiple_of(i * tq, tq), tq)
        prep(xl_ref, r, pl.ds(pl.multiple_of(ctx + i * tq, tq), tq), r)
        return carry

    lax.fori_loop(0, seq // tq, prep_body, 0)

    stages = [(_head_mask((1, GROUP_W), h, HEAD_DIM, BF16), *_head_v(h, va_s, vb_s),
               _head_mask((1, GROUP_W), h, HEAD_DIM, F32)) for h in range(N_HEADS)]
    finish = lambda acc: acc
    yc_ref[...] = _context_attention(q_s[0:ctx, :], k_s[0:ctx, :], stages, finish).astype(yc_ref.dtype)
    _attention_stages(q_s, k_s, yl_ref, p_s, acc_s, stages, finish, ctx=ctx, seq=seq, tq=tq, unroll=unroll)


def _gqa_mixer(p_gqa, cos_t, sin_t, gq, gk, *, n_batch, seq, ctx, tq=ATTN_TQ, unroll=ATTN_UNROLL, lag=ATTN_LAG):
    lat_blocks = n_batch * seq // ctx
    kern = functools.partial(_gqa_kernel, seq=seq, ctx=ctx, tq=tq, unroll=unroll)
    yl, yc = pl.pallas_call(
        kern,
        grid=(n_batch,),
        in_specs=[pl.BlockSpec((seq, IN_GQA), lambda b: (b, 0)),
                  pl.BlockSpec((ctx, IN_GQA), lambda b: (lat_blocks + b, 0)),
                  _resident(cos_t.shape), _resident(sin_t.shape),
                  _resident((1, GROUP_W)), _resident((1, GROUP_W))],
        out_specs=[pl.BlockSpec((seq, GROUP_W), lambda b: (b, 0)),
                   pl.BlockSpec((ctx, GROUP_W), lambda b: (b, 0))],
        out_shape=[jax.ShapeDtypeStruct((n_batch * seq, GROUP_W), BF16),
                   jax.ShapeDtypeStruct((n_batch * ctx, GROUP_W), BF16)],
        scratch_shapes=[pltpu.VMEM((ctx + seq, GROUP_W), BF16)] * 4
                       + [pltpu.VMEM((lag, tq, ctx + seq), BF16), pltpu.VMEM((tq, GROUP_W), F32)],
        compiler_params=_cparams("parallel"),
        name="gqa_mixer",
    )(p_gqa, p_gqa, cos_t, sin_t, gq, gk)
    return yl, yc


def _diff_kernel(xl_ref, xc_ref, cos_ref, sin_ref, gq_ref, gk_ref, lam_ref, sub_ref, yl_ref, yc_ref,
                 q_s, k_s, va_s, vb_s, p_s, acc_s, *, seq, ctx, tq, unroll, lam_init):
    ones_sub = _group_ones(GROUP_W, DIFF_SUB)
    ones_head = _group_ones(GROUP_W, HEAD_DIM)
    scale = DIFF_SUB ** -0.5 * LOG2E
    lp = lam_ref[...]
    lam = (jnp.exp(jnp.sum(lp[0:1] * lp[1:2], axis=-1, keepdims=True))
           - jnp.exp(jnp.sum(lp[2:3] * lp[3:4], axis=-1, keepdims=True)) + lam_init)

    def prep(x_ref, rows, dst, rope_rows):
        q = _group_rmsnorm(x_ref[rows, 0:GROUP_W].astype(F32), gq_ref[...], ones_sub, DIFF_SUB)
        k = _group_rmsnorm(x_ref[rows, GROUP_W:2 * GROUP_W].astype(F32), gk_ref[...], ones_sub, DIFF_SUB)
        if rope_rows is not None:
            cos_t, sin_t = cos_ref[rope_rows, :], sin_ref[rope_rows, :]
            q, k = _rope(q, cos_t, sin_t, DIFF_SUB // 2), _rope(k, cos_t, sin_t, DIFF_SUB // 2)
        q_s[dst, :] = (q * scale).astype(BF16)
        k_s[dst, :] = k.astype(BF16)
        va_s[dst, :], vb_s[dst, :] = _ones_lane_variants(x_ref[rows, 2 * GROUP_W:3 * GROUP_W])

    prep(xc_ref, pl.ds(0, ctx), pl.ds(0, ctx), None)

    def prep_body(i, carry):
        r = pl.ds(pl.multiple_of(i * tq, tq), tq)
        prep(xl_ref, r, pl.ds(pl.multiple_of(ctx + i * tq, tq), tq), r)
        return carry

    lax.fori_loop(0, seq // tq, prep_body, 0)

    stages = []
    for h in range(N_HEADS):
        head = _head_mask((1, GROUP_W), h, HEAD_DIM, F32)
        for j, weight in enumerate((head, -lam * head)):
            stages.append((_head_mask((1, GROUP_W), 2 * h + j, DIFF_SUB, BF16), *_head_v(h, va_s, vb_s), weight))
    finish = lambda acc: _group_rmsnorm(acc, sub_ref[...], ones_head, HEAD_DIM) * (1.0 - lam_init)
    yc_ref[...] = _context_attention(q_s[0:ctx, :], k_s[0:ctx, :], stages, finish).astype(yc_ref.dtype)
    _attention_stages(q_s, k_s, yl_ref, p_s, acc_s, stages, finish, ctx=ctx, seq=seq, tq=tq, unroll=unroll)


def _diff_mixer(p_df, cos_t, sin_t, gq, gk, lam_p, sub_g, *, lam_init, n_batch, seq, ctx,
                tq=ATTN_TQ, unroll=ATTN_UNROLL, lag=ATTN_LAG):
    lat_blocks = n_batch * seq // ctx
    kern = functools.partial(_diff_kernel, seq=seq, ctx=ctx, tq=tq, unroll=unroll, lam_init=lam_init)
    yl, yc = pl.pallas_call(
        kern,
        grid=(n_batch,),
        in_specs=[pl.BlockSpec((seq, IN_DIFF), lambda b: (b, 0)),
                  pl.BlockSpec((ctx, IN_DIFF), lambda b: (lat_blocks + b, 0)),
                  _resident(cos_t.shape), _resident(sin_t.shape),
                  _resident((1, GROUP_W)), _resident((1, GROUP_W)),
                  _resident(lam_p.shape), _resident((1, GROUP_W))],
        out_specs=[pl.BlockSpec((seq, GROUP_W), lambda b: (b, 0)),
                   pl.BlockSpec((ctx, GROUP_W), lambda b: (b, 0))],
        out_shape=[jax.ShapeDtypeStruct((n_batch * seq, GROUP_W), BF16),
                   jax.ShapeDtypeStruct((n_batch * ctx, GROUP_W), BF16)],
        scratch_shapes=[pltpu.VMEM((ctx + seq, GROUP_W), BF16)] * 4
                       + [pltpu.VMEM((lag, tq, ctx + seq), BF16), pltpu.VMEM((tq, GROUP_W), F32)],
        compiler_params=_cparams("parallel"),
        name="diff_attention_mixer",
    )(p_df, p_df, cos_t, sin_t, gq, gk, lam_p, sub_g)
    return yl, yc


def _dot_split2(x, w_bf16):
    hi = x.astype(BF16)
    lo = (x - hi.astype(F32)).astype(BF16)
    return _dot(hi, w_bf16) + _dot(lo, w_bf16)


def _mlstm_kernel(xl_ref, xc_ref, gl_ref, gc_ref, cw_ref, cb_ref, gb_ref, ng_ref, yl_ref, yc_ref,
                  q_s, k_s, hsum, c_s, n_s, m_s, *, seq, ctx, unroll):
    t = ML_CHUNK
    n_chunks = seq // t
    ones_head = _group_ones(GROUP_W, HEAD_DIM)
    block_diag = ones_head.astype(F32)
    r_i = lax.broadcasted_iota(jnp.int32, (t, t), 0)
    c_i = lax.broadcasted_iota(jnp.int32, (t, t), 1)
    tril_b = jnp.where(c_i <= r_i, 1.0, 0.0).astype(BF16)
    causal = (c_i <= r_i, c_i >= r_i)
    row_g = lax.broadcasted_iota(jnp.int32, (t, IN_GATE), 0)
    sel_r = lax.broadcasted_iota(jnp.int32, (IN_GATE, GROUP_W), 0)
    sel_c = lax.broadcasted_iota(jnp.int32, (IN_GATE, GROUP_W), 1) // HEAD_DIM
    select = [jnp.where(sel_r == 8 * d + sel_c, 1.0, 0.0).astype(BF16) for d in range(2)]
    q_masks = [_head_mask((1, GROUP_W), h, HEAD_DIM, BF16) for h in range(N_HEADS)]

    for x_ref, rows, off in ((xc_ref, ctx, 0), (xl_ref, seq, ctx)):
        qk = _silu(_dwconv3(x_ref[:, 0:2 * GROUP_W].astype(F32), cw_ref[...], cb_ref[...]))
        q_s[off:off + rows, :] = (qk[:, 0:GROUP_W] * (HEAD_DIM ** -0.5)).astype(BF16)
        k_s[off:off + rows, :] = qk[:, GROUP_W:2 * GROUP_W].astype(BF16)

    hsum[...] = jnp.zeros_like(hsum)
    c_s[...] = jnp.zeros_like(c_s)
    n_s[...] = jnp.zeros_like(n_s)
    m_s[...] = jnp.zeros_like(m_s)

    def running_max(x, d):
        step = 1
        while step < t:
            if d == 0:
                shifted, valid = pltpu.roll(x, step, 0), row_g >= step
            else:
                shifted, valid = pltpu.roll(x, t - step, 0), row_g < t - step
            x = jnp.maximum(x, jnp.where(valid, shifted, -jnp.inf))
            step *= 2
        return x

    def chunk(d, rows, v, g_raw):
        g = g_raw + gb_ref[...]
        logf = pltpu.roll(jax.nn.log_sigmoid(g), IN_GATE - N_HEADS, 1)
        pre = _dot_exact_rhs(tril_b, logf)
        tot = pre[t - 1:t, :]
        cum = pre if d == 0 else tot - pre + logf
        r = g - cum
        m_prev = m_s[d]
        log_inter = cum + m_prev
        m_t = jnp.maximum(log_inter, cum + running_max(r, d))
        a_t = cum - m_t
        log_g = (tot - cum) + g
        m_new = jnp.maximum(tot + m_prev, jnp.max(log_g, axis=0, keepdims=True))
        m_s[d] = m_new
        rowwise = jnp.concatenate([jnp.exp(log_inter - m_t), jnp.exp(-m_t), jnp.exp(log_g - m_new),
                                   jnp.broadcast_to(jnp.exp(tot + m_prev - m_new), (8, IN_GATE))], axis=0)
        full = _dot_split2(rowwise, select[d])
        w_inter, e_m, w_g, w_c = full[0:t], full[t:2 * t], full[2 * t:3 * t], full[3 * t:3 * t + 1]

        q, k = q_s[rows, :], k_s[rows, :]
        r_t = jnp.transpose(r)
        pair = [None, None]
        for h in range(N_HEADS):
            col = 8 * d + h
            w_intra = jnp.exp(jnp.where(causal[d], a_t[:, col:col + 1] + r_t[col:col + 1, :], -jnp.inf))
            s = (_dot_nt(q * q_masks[h], k) * w_intra).astype(BF16)
            o = _dot(s, v * q_masks[h] + q_masks[(h + 2) % N_HEADS])
            pair[h // 2] = o if pair[h // 2] is None else pair[h // 2] + o
        half = GROUP_W // 2
        num_i = jnp.concatenate([pair[0][:, :half], pair[1][:, half:]], axis=1)
        den_i = jnp.concatenate([pair[0][:, half:], pair[1][:, :half]], axis=1)
        ct, n_full = c_s[d], n_s[d]
        inter = _dot(q, ct.astype(BF16))
        qn = _dot((q.astype(F32) * n_full).astype(BF16), ones_head)
        num = w_inter * inter + num_i
        den = w_inter * qn + den_i
        kw = k.astype(F32) * w_g
        c_s[d] = w_c * ct + _dot_tn(kw.astype(BF16), v) * block_diag
        n_s[d] = w_c * n_full + jnp.sum(kw, axis=0, keepdims=True)
        return num / jnp.maximum(jnp.abs(den), e_m)

    assert ctx == t
    v_c = xc_ref[:, 2 * GROUP_W:3 * GROUP_W]
    hsum[0:t, :] = chunk(0, pl.ds(0, t), v_c, gc_ref[...]) + chunk(1, pl.ds(0, t), v_c, gc_ref[...])

    def body(j, carry):
        for d in range(2):
            cidx = j if d == 0 else n_chunks - 1 - j
            lat = pl.ds(pl.multiple_of(cidx * t, t), t)
            rows = pl.ds(pl.multiple_of(ctx + cidx * t, t), t)
            hsum[rows, :] += chunk(d, rows, xl_ref[lat, 2 * GROUP_W:3 * GROUP_W], gl_ref[lat, :])
        return carry

    lax.fori_loop(0, n_chunks, body, 0, unroll=unroll)

    for x_ref, y_ref, rows, off in ((xc_ref, yc_ref, ctx, 0), (xl_ref, yl_ref, seq, ctx)):
        hn = _group_rmsnorm(hsum[off:off + rows, :], ng_ref[...], ones_head, HEAD_DIM)
        y_ref[...] = (hn * jax.nn.sigmoid(x_ref[:, 3 * GROUP_W:4 * GROUP_W].astype(F32))).astype(y_ref.dtype)


def _mlstm_mixer(p_ml, p_gate, conv_w, conv_b, gate_b, norm_g, *, n_batch, seq, ctx, unroll=1):
    lat_blocks = n_batch * seq // ctx
    kern = functools.partial(_mlstm_kernel, seq=seq, ctx=ctx, unroll=unroll)
    yl, yc = pl.pallas_call(
        kern,
        grid=(n_batch,),
        in_specs=[pl.BlockSpec((seq, IN_ML), lambda b: (b, 0)),
                  pl.BlockSpec((ctx, IN_ML), lambda b: (lat_blocks + b, 0)),
                  pl.BlockSpec((seq, IN_GATE), lambda b: (b, 0)),
                  pl.BlockSpec((ctx, IN_GATE), lambda b: (lat_blocks + b, 0)),
                  _resident(conv_w.shape), _resident(conv_b.shape), _resident(gate_b.shape), _resident(norm_g.shape)],
        out_specs=[pl.BlockSpec((seq, GROUP_W), lambda b: (b, 0)),
                   pl.BlockSpec((ctx, GROUP_W), lambda b: (b, 0))],
        out_shape=[jax.ShapeDtypeStruct((n_batch * seq, GROUP_W), BF16),
                   jax.ShapeDtypeStruct((n_batch * ctx, GROUP_W), BF16)],
        scratch_shapes=[pltpu.VMEM((ctx + seq, GROUP_W), BF16), pltpu.VMEM((ctx + seq, GROUP_W), BF16),
                        pltpu.VMEM((ctx + seq, GROUP_W), F32),
                        pltpu.VMEM((2, GROUP_W, GROUP_W), F32), pltpu.VMEM((2, 1, GROUP_W), F32),
                        pltpu.VMEM((2, 1, IN_GATE), F32)],
        compiler_params=_cparams("parallel"),
        name="mlstm_mixer",
    )(p_ml, p_ml, p_gate, p_gate, conv_w, conv_b, gate_b, norm_g)
    return yl, yc


def _phase_tables(idx_a, idx_b, n):
    ang = (2.0 * math.pi / n) * ((idx_a[:, None] * idx_b[None, :]) % n).astype(F32)
    return jnp.cos(ang), jnp.sin(ang)


def _dft_matrices(length, n):
    nf, lo = n // 2, 64
    k = jnp.arange(nf, dtype=jnp.int32)
    k1, k0 = lo * jnp.arange(nf // lo, dtype=jnp.int32), jnp.arange(lo, dtype=jnp.int32)
    j = jnp.arange(length, dtype=jnp.int32)
    m = j + length // 2
    (ca, sa), (cb, sb) = _phase_tables(k1, j, n), _phase_tables(k0, j, n)
    cos_f = (ca[:, None, :] * cb[None, :, :] - sa[:, None, :] * sb[None, :, :]).reshape(nf, length)
    sin_f = (sa[:, None, :] * cb[None, :, :] + ca[:, None, :] * sb[None, :, :]).reshape(nf, length)
    alt = jnp.where(j % 2 == 0, 1.0, -1.0).astype(F32)
    fwd = jnp.concatenate([cos_f, jnp.where(k[:, None] == 0, alt[None, :], sin_f)], axis=0)
    (ca, sa), (cb, sb) = _phase_tables(m, k1, n), _phase_tables(m, k0, n)
    cos_i = (ca[:, :, None] * cb[:, None, :] - sa[:, :, None] * sb[:, None, :]).reshape(length, nf)
    sin_i = (sa[:, :, None] * cb[:, None, :] + ca[:, :, None] * sb[:, None, :]).reshape(length, nf)
    wk = jnp.where(k == 0, 1.0, 2.0).astype(F32) / n
    alt_m = jnp.where(m % 2 == 0, 1.0, -1.0).astype(F32) / n
    inv = jnp.concatenate([cos_i * wk[None, :], jnp.where(k[None, :] == 0, alt_m[:, None], sin_i * (2.0 / n))], axis=1)
    return fwd.astype(BF16), inv.astype(BF16)


def _hyena_filter_consts(length):
    t = jnp.arange(length, dtype=F32)
    tn = t / length
    bands = jnp.arange(1, HY_POS_BANDS + 1, dtype=F32)
    ang = 2.0 * math.pi * tn[:, None] * bands
    feats = jnp.concatenate([tn[:, None], jnp.cos(ang), jnp.sin(ang)], axis=-1)
    feats = jnp.pad(feats, ((0, 0), (0, 128 - feats.shape[1])))
    dist = jnp.abs(t - length // 2) / (length / 2)
    deltas = jnp.abs(jnp.linspace(math.log(HY_DECAY_TARGET) / HY_SLOW_DECAY,
                                  math.log(HY_DECAY_TARGET) / HY_FAST_DECAY, GROUP_W, dtype=F32))
    window = jnp.exp(-dist[:, None] * jnp.tile(deltas, 2))
    return feats, window


def _filter_kernel(feats_ref, win_ref, w1_ref, b1_ref, w2_ref, b2_ref, w3_ref, b3_ref, f_ref, p_ref, qz_ref, p2_ref):
    h = jnp.sin(_dot_f32ish(feats_ref[...], w1_ref[0]) + b1_ref[0])
    h = jnp.sin(_dot_f32ish(h, w2_ref[0]) + b2_ref[0])
    h = (_dot_f32ish(h, w3_ref[0]) + b3_ref[0]) * win_ref[...]
    h = h / jnp.sum(jnp.abs(h), axis=0, keepdims=True)
    h1 = h.astype(BF16)
    h2 = (h - h1.astype(F32)).astype(BF16)
    nf = f_ref.shape[0] // 2
    step = math.gcd(nf, 512)
    for r in range(0, nf, step):
        fc, fs = f_ref[r:r + step, :], f_ref[nf + r:nf + r + step, :]
        p = _dot(fc, h1) + _dot(fc, h2)
        q = _dot(fs, h1) + _dot(fs, h2)
        p_ref[0, r:r + step, :] = p
        if r == 0:
            first = lax.broadcasted_iota(jnp.int32, p.shape, 0) == 0
            qz_ref[0, r:r + step, :] = jnp.where(first, 0.0, q)
            p2_ref[0, r:r + step, :] = jnp.where(first, q, p)
        else:
            qz_ref[0, r:r + step, :] = q
            p2_ref[0, r:r + step, :] = p


def _hyena_filter_spectra(length, fwd, w1, b1, w2, b2, w3, b3):
    depth = w1.shape[0]
    feats, window = _hyena_filter_consts(length)
    nf = fwd.shape[0] // 2
    pad2 = lambda a, r, c: jnp.pad(a, ((0, 0), (0, r - a.shape[1]), (0, c - a.shape[2])))
    w1p, w2p = pad2(w1, 128, 128), pad2(w2, 128, 128)
    w3p = pad2(w3, 128, w3.shape[2])
    b1p = jnp.pad(b1, ((0, 0), (0, 128 - b1.shape[1])))[:, None, :]
    b2p = jnp.pad(b2, ((0, 0), (0, 128 - b2.shape[1])))[:, None, :]
    b3p = b3[:, None, :]
    lay = lambda a: pl.BlockSpec((1,) + a.shape[1:], lambda l, o: (l, 0, 0))
    per_order = lambda rows: pl.BlockSpec((1, rows, GROUP_W), lambda l, o: (l, 0, o))
    out = jax.ShapeDtypeStruct((depth, nf, 2 * GROUP_W), F32)
    return pl.pallas_call(
        _filter_kernel,
        grid=(depth, 2),
        in_specs=[_resident(feats.shape), pl.BlockSpec((length, GROUP_W), lambda l, o: (0, o)),
                  lay(w1p), lay(b1p), lay(w2p), lay(b2p), per_order(128), per_order(1), _resident(fwd.shape)],
        out_specs=[per_order(nf)] * 3,
        out_shape=[out, out, out],
        compiler_params=_cparams("parallel", "parallel"),
        name="hyena_filter_spectra",
    )(feats, window, w1p, b1p, w2p, b2p, w3p, b3p, fwd)


def _hyena_conv_kernel(sig_ref, gate_ref, cws_ref, cbs_ref, cwg_ref, cbg_ref, skip_ref, f_ref, g_ref,
                       p_ref, qz_ref, p2_ref, o_ref, *, sig_conv):
    s = sig_ref[...].astype(F32)
    if sig_conv:
        s = _dwconv3(s, cws_ref[...], cbs_ref[...])
    gate = _dwconv3(gate_ref[...].astype(F32), cwg_ref[...], cbg_ref[...])
    sb = s.astype(BF16)
    nf = f_ref.shape[0] // 2
    a = _dot(f_ref[0:nf, :], sb)
    b = _dot(f_ref[nf:2 * nf, :], sb)
    ay = a * p_ref[0] - b * qz_ref[0]
    by = a * qz_ref[0] + b * p2_ref[0]
    y = _dot(g_ref[:, 0:nf], ay.astype(BF16)) + _dot(g_ref[:, nf:2 * nf], by.astype(BF16))
    o_ref[...] = (gate * (y + skip_ref[...] * s)).astype(o_ref.dtype)


def _hyena_conv(sig, sig_col, gate, gate_col, conv_w, conv_b, skip, fwd, inv, spectra, order, layer,
                *, rows, first_block, n_batch, out_dtype, sig_conv):
    cw = lambda col: pl.BlockSpec((3, GROUP_W), lambda b: (0, col))
    cb = lambda col: pl.BlockSpec((1, GROUP_W), lambda b: (0, col))
    spec = pl.BlockSpec((1, fwd.shape[0] // 2, GROUP_W), lambda b: (layer, 0, order), pipeline_mode=pl.Buffered(1))
    kern = functools.partial(_hyena_conv_kernel, sig_conv=sig_conv)
    return pl.pallas_call(
        kern,
        grid=(n_batch,),
        in_specs=[pl.BlockSpec((rows, GROUP_W), lambda b: (first_block + b, sig_col)),
                  pl.BlockSpec((rows, GROUP_W), lambda b: (first_block + b, gate_col)),
                  cw(sig_col if sig_conv else 0), cb(sig_col if sig_conv else 0), cw(gate_col), cb(gate_col),
                  pl.BlockSpec((1, GROUP_W), lambda b: (0, 0)),
                  _resident(fwd.shape), _resident(inv.shape), spec, spec, spec],
        out_specs=pl.BlockSpec((rows, GROUP_W), lambda b: (b, 0)),
        out_shape=jax.ShapeDtypeStruct((n_batch * rows, GROUP_W), out_dtype),
        compiler_params=_cparams("parallel"),
        name="hyena_long_conv",
    )(sig, gate, conv_w, conv_b, conv_w, conv_b, skip, fwd, inv, *spectra)


def _hyena_mixer(p_hy, conv_w, conv_b, skip, dft, spectra, layer, *, rows, first_block, n_batch):
    fwd, inv = dft
    common = dict(rows=rows, n_batch=n_batch)
    z = _hyena_conv(p_hy, 0, p_hy, 1, conv_w, conv_b, skip[0:1], fwd, inv, spectra, 0, layer,
                    first_block=first_block, out_dtype=F32, sig_conv=True, **common)
    cw = lambda col: pl.BlockSpec((3, GROUP_W), lambda b: (0, col))
    cb = lambda col: pl.BlockSpec((1, GROUP_W), lambda b: (0, col))
    spec = pl.BlockSpec((1, fwd.shape[0] // 2, GROUP_W), lambda b: (layer, 0, 1), pipeline_mode=pl.Buffered(1))
    kern = functools.partial(_hyena_conv_kernel, sig_conv=False)
    return pl.pallas_call(
        kern,
        grid=(n_batch,),
        in_specs=[pl.BlockSpec((rows, GROUP_W), lambda b: (b, 0)),
                  pl.BlockSpec((rows, GROUP_W), lambda b: (first_block + b, 2)),
                  cw(0), cb(0), cw(2), cb(2),
                  pl.BlockSpec((1, GROUP_W), lambda b: (0, 0)),
                  _resident(fwd.shape), _resident(inv.shape), spec, spec, spec],
        out_specs=pl.BlockSpec((rows, GROUP_W), lambda b: (b, 0)),
        out_shape=jax.ShapeDtypeStruct((n_batch * rows, GROUP_W), BF16),
        compiler_params=_cparams("parallel"),
        name="hyena_long_conv_2",
    )(z, p_hy, conv_w, conv_b, conv_w, conv_b, skip[1:2], fwd, inv, *spectra)


def _rope_tables(length, dim, reps):
    rows = length // GRID_W
    row = jnp.repeat(jnp.arange(rows), GRID_W).astype(F32)
    col = jnp.tile(jnp.arange(GRID_W), rows).astype(F32)
    n_freq = dim // 4
    inv = ROPE_THETA ** (-jnp.arange(n_freq, dtype=F32) / n_freq)
    ang = jnp.concatenate([row[:, None] * inv, col[:, None] * inv], axis=-1)
    cos, sin = jnp.cos(ang), jnp.sin(ang)
    return (jnp.tile(jnp.concatenate([cos, cos], axis=-1), (1, reps)),
            jnp.tile(jnp.concatenate([-sin, sin], axis=-1), (1, reps)))


def _in_weight_layout(w_in):
    ml, n_gate = 4 * GROUP_W, 16
    o_gqa = ml + n_gate
    o_hy = o_gqa + 2 * GROUP_W
    o_df = o_hy + 3 * GROUP_W
    rep = lambda start: jnp.concatenate([w_in[..., start + (h // 2) * HEAD_DIM: start + (h // 2 + 1) * HEAD_DIM]
                                         for h in range(N_HEADS)], axis=-1)
    gates = jnp.pad(w_in[..., ml:ml + n_gate], ((0, 0), (0, 0), (0, IN_GATE - n_gate)))
    return jnp.concatenate([w_in[..., 0:ml],
                            w_in[..., o_gqa:o_gqa + GROUP_W], rep(o_gqa + GROUP_W), rep(o_gqa + GROUP_W + 2 * HEAD_DIM),
                            w_in[..., o_hy:o_hy + 3 * GROUP_W],
                            w_in[..., o_df:o_df + 3 * GROUP_W],
                            gates], axis=-1).astype(BF16)


def kernel(x, c, ctx, c_ctx, ada_w, ada_b, norm_g, ffn_w13, ffn_w2, w_in, w_out, ml_gate_b, ml_conv_w, ml_conv_b,
           ml_norm_g, gqa_qk_g, hy_conv_w, hy_conv_b, hy_filt_w1, hy_filt_b1, hy_filt_w2, hy_filt_b2, hy_filt_w3,
           hy_filt_b3, hy_skip, diff_qk_g, diff_lambda, diff_subln_g):
    n_batch, seq, d = x.shape
    n_ctx = ctx.shape[1]
    depth = ada_w.shape[0]
    n_lat = n_batch * seq
    n_all = n_lat + n_batch * n_ctx
    assert n_batch * n_ctx == seq and n_ctx == ML_CHUNK and seq % ROW_TILE == 0

    c_all = jnp.pad(jnp.concatenate([c, c_ctx[None]], axis=0), ((0, 16 - n_batch - 1), (0, 0)))
    mod = _modulation(c_all, ada_w, ada_b)[:, :n_batch + 1].reshape(depth, n_batch + 1, N_MOD, d)

    w13 = ffn_w13.astype(BF16)
    w2 = ffn_w2.astype(BF16)
    w_in_l = _in_weight_layout(w_in)
    w_out_b = w_out.astype(BF16)

    rope_gqa = _rope_tables(seq, HEAD_DIM, N_HEADS)
    rope_diff = _rope_tables(seq, DIFF_SUB, 2 * N_HEADS)
    dft_l = _dft_matrices(seq, 3 * seq // 2)
    dft_c = _dft_matrices(n_ctx, 2 * n_ctx)
    filt = (hy_filt_w1, hy_filt_b1, hy_filt_w2, hy_filt_b2, hy_filt_w3, hy_filt_b3)
    spec_l = _hyena_filter_spectra(seq, dft_l[0], *filt)
    spec_c = _hyena_filter_spectra(n_ctx, dft_c[0], *filt)

    h = (x.reshape(n_lat, d), ctx.reshape(n_batch * n_ctx, d))
    dims = dict(seq=seq, n_batch=n_batch)
    gqa_cfg = [dict(tq=256, unroll=2, lag=2), dict(tq=512, unroll=1, lag=2), dict(tq=512, unroll=2, lag=2),
               dict(tq=256, unroll=4, lag=2)]
    diff_cfg = [dict(tq=256, unroll=2, lag=2), dict(tq=512, unroll=1, lag=2), dict(tq=256, unroll=2, lag=3),
                dict(tq=256, unroll=1, lag=2)]
    ffn_cfg = [dict(tm=512, chunk=512), dict(tm=1024, chunk=256), dict(tm=512, chunk=1024), dict(tm=256, chunk=512)]
    inproj_tm = [512, 1024, 256, 512]
    ml_unroll = [1, 2, 1, 2]
    for l in range(depth):
        need_ctx = l < depth - 1
        lam_init = 0.8 - 0.6 * math.exp(-0.3 * l)
        h = _ffn_half(h, mod[l], norm_g[l, 0], w13, w2, layer=l, half=0, base=0, n_rows=n_all, **ffn_cfg[l], **dims)
        p_ml, p_gqa, p_hy, p_df, p_gate = _in_projection(h, mod[l], norm_g[l, 1], w_in_l, layer=l, tm=inproj_tm[l],
                                                          **dims)

        gate_b = jnp.pad(ml_gate_b[l], (0, IN_GATE - ml_gate_b.shape[1]))[None]
        y_ml = _mlstm_mixer(p_ml, p_gate, ml_conv_w[l], ml_conv_b[l][None], gate_b, ml_norm_g[l][None],
                            n_batch=n_batch, seq=seq, ctx=n_ctx, unroll=ml_unroll[l])
        tile_g = lambda g, reps: jnp.tile(g, reps)[None]
        y_gqa = _gqa_mixer(p_gqa, *rope_gqa, tile_g(gqa_qk_g[l, 0], N_HEADS), tile_g(gqa_qk_g[l, 1], N_HEADS),
                           n_batch=n_batch, seq=seq, ctx=n_ctx, **gqa_cfg[l])
        y_df = _diff_mixer(p_df, *rope_diff, tile_g(diff_qk_g[l, 0], 2 * N_HEADS), tile_g(diff_qk_g[l, 1], 2 * N_HEADS),
                           diff_lambda[l], tile_g(diff_subln_g[l], N_HEADS),
                           lam_init=lam_init, n_batch=n_batch, seq=seq, ctx=n_ctx, **diff_cfg[l])
        y_hy = _hyena_mixer(p_hy, hy_conv_w[l], hy_conv_b[l][None], hy_skip[l], dft_l, spec_l, l,
                            rows=seq, first_block=0, n_batch=n_batch)
        y_hy_c = y_hy if not need_ctx else _hyena_mixer(
            p_hy, hy_conv_w[l], hy_conv_b[l][None], hy_skip[l], dft_c, spec_c, l,
            rows=n_ctx, first_block=n_lat // n_ctx, n_batch=n_batch)
        h = _ffn_half(h, mod[l], norm_g[l, 2], w13, w2, layer=l, half=1, base=6,
                      n_rows=n_all if need_ctx else n_lat,
                      mix=[y_ml, y_gqa, (y_hy, y_hy_c), y_df], w_out=w_out_b, **ffn_cfg[l], **dims)
    return h[:n_lat].reshape(n_batch, seq, d)
```

```python
import functools
import math

import jax
import jax.numpy as jnp
from jax import lax
from jax.experimental import pallas as pl
from jax.experimental.pallas import tpu as pltpu

F32 = jnp.float32
BF16 = jnp.bfloat16

EPS = 1e-6
ROPE_THETA = 10000.0
GRID_W = 64
N_MOD = 9
GROUP_W = 256
HEAD_DIM = 64
N_HEADS = 4
DIFF_SUB = 32
ML_CHUNK = 256
HY_POS_BANDS = 16
HY_DECAY_TARGET = 1e-2
HY_FAST_DECAY = 0.3
HY_SLOW_DECAY = 1.5
ROW_TILE = 512
VMEM_LIMIT = 56 * 1024 * 1024


def _cparams(*sem):
    return pltpu.CompilerParams(dimension_semantics=sem, vmem_limit_bytes=VMEM_LIMIT)


def _resident(shape):
    nd = len(shape)
    return pl.BlockSpec(shape, lambda *_: (0,) * nd, pipeline_mode=pl.Buffered(1))


def _resident_slice(arr, index):
    tail = arr.shape[len(index):]
    return pl.BlockSpec((None,) * len(index) + tail, lambda *_: tuple(index) + (0,) * len(tail),
                        pipeline_mode=pl.Buffered(1))


def _dot(a, b):
    return jnp.dot(a, b, preferred_element_type=F32)


def _dot_nt(a, b):
    return lax.dot_general(a, b, (((1,), (1,)), ((), ())), preferred_element_type=F32)


def _dot_tn(a, b):
    return lax.dot_general(a, b, (((0,), (0,)), ((), ())), preferred_element_type=F32)


def _split3(x):
    x1 = x.astype(BF16)
    r = x - x1.astype(F32)
    x2 = r.astype(BF16)
    x3 = (r - x2.astype(F32)).astype(BF16)
    return x1, x2, x3


def _dot_exact_rhs(a_bf16, x):
    x1, x2, x3 = _split3(x)
    return _dot(a_bf16, x1) + _dot(a_bf16, x2) + _dot(a_bf16, x3)


def _dot_f32ish(a, b):
    a1 = a.astype(BF16)
    a2 = (a - a1.astype(F32)).astype(BF16)
    b1 = b.astype(BF16)
    b2 = (b - b1.astype(F32)).astype(BF16)
    return _dot(a1, b1) + _dot(a1, b2) + _dot(a2, b1)


def _silu(x):
    return x * jax.nn.sigmoid(x)


def _modnorm(x, g, shift, scale):
    y = x * lax.rsqrt(jnp.mean(x * x, axis=-1, keepdims=True) + EPS)
    return (y * g) * (1.0 + scale) + shift


def _lane_iota(shape):
    return lax.broadcasted_iota(jnp.int32, shape, len(shape) - 1)


def _group_ones(n, group):
    r = lax.broadcasted_iota(jnp.int32, (n, n), 0) // group
    c = lax.broadcasted_iota(jnp.int32, (n, n), 1) // group
    return jnp.where(r == c, 1.0, 0.0).astype(BF16)


def _group_rmsnorm(x, gain, ones_g, group):
    x2 = x * x
    hi = x2.astype(BF16)
    lo = (x2 - hi.astype(F32)).astype(BF16)
    ms = (_dot(hi, ones_g) + _dot(lo, ones_g)) * (1.0 / group)
    return x * lax.rsqrt(ms + EPS) * gain


def _rope(x, cos_t, sin_t, half):
    n = x.shape[-1]
    left = pltpu.roll(x, n - half, 1)
    right = pltpu.roll(x, half, 1)
    sw = jnp.where((_lane_iota(x.shape) & (2 * half - 1)) < half, left, right)
    return x * cos_t + sw * sin_t


def _dwconv3(x, w, b):
    rows = x.shape[0]
    row = lax.broadcasted_iota(jnp.int32, x.shape, 0)
    xm = jnp.where(row == 0, 0.0, pltpu.roll(x, 1, 0))
    xp = jnp.where(row == rows - 1, 0.0, pltpu.roll(x, rows - 1, 0))
    return xm * w[0:1] + x * w[1:2] + xp * w[2:3] + b


def _head_mask(shape, h, width, dtype):
    lane = _lane_iota(shape)
    return jnp.where((lane >= h * width) & (lane < (h + 1) * width), 1.0, 0.0).astype(dtype)


def _mod_kernel(c_ref, w_ref, b_ref, o_ref):
    sc = _silu(c_ref[...]).astype(BF16)
    o_ref[0] = _dot(sc, w_ref[0].astype(BF16)) + b_ref[0]


def _modulation(c_all, ada_w, ada_b):
    depth, d, nmod = ada_w.shape
    rows = c_all.shape[0]
    tn = 1536
    return pl.pallas_call(
        _mod_kernel,
        grid=(depth, nmod // tn),
        in_specs=[pl.BlockSpec((rows, d), lambda l, j: (0, 0)),
                  pl.BlockSpec((1, d, tn), lambda l, j: (l, 0, j)),
                  pl.BlockSpec((1, 1, tn), lambda l, j: (l, 0, j))],
        out_specs=pl.BlockSpec((1, rows, tn), lambda l, j: (l, 0, j)),
        out_shape=jax.ShapeDtypeStruct((depth, rows, nmod), F32),
        compiler_params=_cparams("parallel", "parallel"),
        name="adaln_modulation",
    )(c_all, ada_w, ada_b.reshape(depth, 1, nmod))


def _row_specs(width, tile, n_lat_tiles):
    return [pl.BlockSpec((tile, width), lambda i: (jnp.minimum(i, n_lat_tiles - 1), 0)),
            pl.BlockSpec((tile, width), lambda i: (jnp.maximum(i - n_lat_tiles, 0), 0))]


def _pick_rows(lat_ref, ctx_ref, n_lat_tiles):
    return jnp.where(pl.program_id(0) < n_lat_tiles, lat_ref[...], ctx_ref[...])


def _ffn_kernel(*refs, base, hidden, chunk, n_lat_tiles, split_x, n_mix):
    refs = list(refs)
    x = _pick_rows(refs.pop(0), refs.pop(0), n_lat_tiles) if split_x else refs.pop(0)[...]
    mod_ref, g_ref, w13_ref, w2_ref = refs[:4]
    mix_refs, (o_ref, acc_ref) = refs[4:-2], refs[-2:]
    if n_mix:
        wo_ref = mix_refs[-1]
        mixed = None
        for k in range(n_mix):
            y = _pick_rows(mix_refs[2 * k], mix_refs[2 * k + 1], n_lat_tiles)
            part = _dot(y, wo_ref[k * GROUP_W:(k + 1) * GROUP_W, :])
            mixed = part if mixed is None else mixed + part
        x = x + mod_ref[0, 5:6] * mixed
    shift, scale, gate = mod_ref[0, base:base + 1], mod_ref[0, base + 1:base + 2], mod_ref[0, base + 2:base + 3]
    xn = _modnorm(x, g_ref[...], shift, scale).astype(BF16)
    for start in range(0, hidden, chunk):
        size = min(chunk, hidden - start)
        a = _dot(xn, w13_ref[:, start:start + size])
        b = _dot(xn, w13_ref[:, hidden + start:hidden + start + size])
        part = _dot((_silu(a) * b).astype(BF16), w2_ref[start:start + size, :])
        if start == 0:
            acc_ref[...] = part
        else:
            acc_ref[...] += part
    o_ref[...] = x + (0.5 * gate) * acc_ref[...]


def _mod_index(tile, seq, n_batch):
    return lambda i: (jnp.minimum((i * tile) // seq, n_batch), 0, 0)


def _ffn_half(h, mod, g, w13, w2, *, layer, half, base, n_rows, seq, n_batch, mix=None, w_out=None,
              tm=ROW_TILE, chunk=512):
    split_x = isinstance(h, tuple)
    d = w13.shape[-2]
    hidden = w2.shape[-2]
    n_lat_tiles = n_batch * seq // tm
    kern = functools.partial(_ffn_kernel, base=base, hidden=hidden, chunk=chunk, n_lat_tiles=n_lat_tiles,
                             split_x=split_x, n_mix=len(mix) if mix else 0)
    x_specs = _row_specs(d, tm, n_lat_tiles) if split_x else [pl.BlockSpec((tm, d), lambda i: (i, 0))]
    mix_args, mix_specs = [], []
    for pair in mix or ():
        mix_args += list(pair)
        mix_specs += _row_specs(GROUP_W, tm, n_lat_tiles)
    if mix:
        mix_args.append(w_out)
        mix_specs.append(_resident_slice(w_out, (layer,)))
    return pl.pallas_call(
        kern,
        grid=(n_rows // tm,),
        in_specs=x_specs + [pl.BlockSpec((1, N_MOD, d), _mod_index(tm, seq, n_batch)),
                            _resident((1, d)), _resident_slice(w13, (layer, half)),
                            _resident_slice(w2, (layer, half))] + mix_specs,
        out_specs=pl.BlockSpec((tm, d), lambda i: (i, 0)),
        out_shape=jax.ShapeDtypeStruct((n_rows, d), F32),
        scratch_shapes=[pltpu.VMEM((tm, d), F32)],
        compiler_params=_cparams("parallel"),
        name="swiglu_half_step",
    )(*(h if split_x else (h,)), mod, g.reshape(1, d), w13, w2, *mix_args)


IN_ML, IN_GQA, IN_HY, IN_DIFF, IN_GATE = 1024, 768, 768, 768, 128


def _in_weight_pieces():
    ml, n_gate = 4 * GROUP_W, 16
    o_gqa = ml + n_gate
    o_hy = o_gqa + 2 * GROUP_W
    o_df = o_hy + 3 * GROUP_W
    pieces = [(0, 0, ml), (ml, o_gqa, GROUP_W)]
    for part, src in enumerate((o_gqa + GROUP_W, o_gqa + GROUP_W + 2 * HEAD_DIM)):
        for h in range(N_HEADS):
            pieces.append((ml + (part + 1) * GROUP_W + h * HEAD_DIM, src + (h // 2) * HEAD_DIM, HEAD_DIM))
    pieces += [(IN_ML + IN_GQA, o_hy, 3 * GROUP_W), (IN_ML + IN_GQA + IN_HY, o_df, 3 * GROUP_W),
               (IN_ML + IN_GQA + IN_HY + IN_DIFF, ml, n_gate)]
    return pieces


def _inproj_kernel(x_ref, mod_ref, g_ref, w_ref, ml_ref, gqa_ref, hy_ref, df_ref, gate_ref, wb_s):
    @pl.when(pl.program_id(0) == 0)
    def _():
        wb_s[:, IN_ML + IN_GQA + IN_HY + IN_DIFF:] = jnp.zeros((wb_s.shape[0], IN_GATE), BF16)
        for dst, src, width in _in_weight_pieces():
            wb_s[:, dst:dst + width] = w_ref[:, src:src + width].astype(BF16)

    xn = _modnorm(x_ref[...], g_ref[...], mod_ref[0, 3:4], mod_ref[0, 4:5]).astype(BF16)
    off = 0
    for ref, width in ((ml_ref, IN_ML), (gqa_ref, IN_GQA), (hy_ref, IN_HY), (df_ref, IN_DIFF), (gate_ref, IN_GATE)):
        ref[...] = _dot(xn, wb_s[:, off:off + width]).astype(ref.dtype)
        off += width


def _in_projection(h, mod, g, w, *, layer, seq, n_batch, tm=ROW_TILE):
    n_rows, d = h.shape
    widths = (IN_ML, IN_GQA, IN_HY, IN_DIFF, IN_GATE)
    dtypes = (BF16, BF16, BF16, BF16, F32)
    return pl.pallas_call(
        _inproj_kernel,
        grid=(n_rows // tm,),
        in_specs=[pl.BlockSpec((tm, d), lambda i: (i, 0)),
                  pl.BlockSpec((1, N_MOD, d), _mod_index(tm, seq, n_batch)),
                  _resident((1, d)), _resident_slice(w, (layer,))],
        out_specs=[pl.BlockSpec((tm, wd), lambda i: (i, 0)) for wd in widths],
        out_shape=[jax.ShapeDtypeStruct((n_rows, wd), dt) for wd, dt in zip(widths, dtypes)],
        scratch_shapes=[pltpu.VMEM((d, sum(widths)), BF16)],
        compiler_params=_cparams("arbitrary"),
        name="input_projection",
    )(h, mod, g.reshape(1, d), w)


LOG2E = 1.4426950408889634
ATTN_LAG = 2
ATTN_UNROLL = 1
ATTN_TQ = 512


def _ones_lane_variants(v):
    grp = _lane_iota((1, GROUP_W)) // HEAD_DIM
    odd = jnp.where(grp % 2 == 1, 1.0, 0.0).astype(BF16)
    even = jnp.where(grp % 2 == 0, 1.0, 0.0).astype(BF16)
    return v * even + odd, v * odd + even


def _scores_to_probs(qm, k_t):
    s = _dot(qm, k_t)
    return jnp.exp2(s - jnp.max(s, axis=-1, keepdims=True)).astype(BF16)


def _probs_times_v(p, v_aug, sum_lane):
    o = _dot(p, v_aug)
    return o * (1.0 / o[:, sum_lane:sum_lane + 1])


def _head_v(h, va, vb):
    return (va, (h + 1) * HEAD_DIM) if h % 2 == 0 else (vb, (h - 1) * HEAD_DIM)


def _attention_stages(q_s, k_s, y_ref, p_s, acc_s, stages, finish, *, ctx, seq, tq, unroll):
    n_keys = ctx + seq

    def probs(q, st):
        return _scores_to_probs(q * st[0], k_s[:, 0:n_keys])

    def out(p, st):
        return _probs_times_v(p, st[1][0:n_keys, :], st[2]) * st[3]

    p_s[...] = jnp.ones_like(p_s)
    acc_s[...] = jnp.zeros_like(acc_s)
    n, lag = len(stages), p_s.shape[0]

    def tail():
        acc = acc_s[...]
        for c in range(lag):
            acc = acc + out(p_s[c], stages[n - lag + c])
        return finish(acc)

    def body(i, carry):
        q = q_s[pl.ds(pl.multiple_of(ctx + i * tq, tq), tq), :]
        prev = jnp.maximum(i - 1, 0)
        y_ref[pl.ds(pl.multiple_of(prev * tq, tq), tq), :] = tail().astype(y_ref.dtype)
        ps, acc = [], None
        for j in range(n):
            ps.append(probs(q, stages[j]))
            if j >= lag:
                o = out(ps[j - lag], stages[j - lag])
                acc = o if acc is None else acc + o
        for c in range(lag):
            p_s[c] = ps[n - lag + c]
        acc_s[...] = acc
        return carry

    n_tiles = seq // tq
    lax.fori_loop(0, n_tiles, body, 0, unroll=unroll)
    y_ref[(n_tiles - 1) * tq:n_tiles * tq, :] = tail().astype(y_ref.dtype)


def _context_attention(q, k_t, stages, finish):
    acc = None
    for st in stages:
        o = _probs_times_v(_scores_to_probs(q * st[0], k_t), st[1][0:k_t.shape[1], :], st[2]) * st[3]
        acc = o if acc is None else acc + o
    return finish(acc)


def _gqa_kernel(xl_ref, xc_ref, cos_ref, sin_ref, gq_ref, gk_ref, yl_ref, yc_ref, q_s, k_s, va_s, vb_s, p_s, acc_s,
                *, seq, ctx, tq, unroll):
    ones_g = _group_ones(GROUP_W, HEAD_DIM)
    scale = HEAD_DIM ** -0.5 * LOG2E

    def prep(x_ref, rows, dst, rope_rows):
        q = _group_rmsnorm(x_ref[rows, 0:GROUP_W].astype(F32), gq_ref[...], ones_g, HEAD_DIM)
        k = _group_rmsnorm(x_ref[rows, GROUP_W:2 * GROUP_W].astype(F32), gk_ref[...], ones_g, HEAD_DIM)
        if rope_rows is not None:
            cos_t, sin_t = cos_ref[rope_rows, :], sin_ref[rope_rows, :]
            q, k = _rope(q, cos_t, sin_t, HEAD_DIM // 2), _rope(k, cos_t, sin_t, HEAD_DIM // 2)
        q_s[dst, :] = (q * scale).astype(BF16)
        k_s[:, dst] = jnp.transpose(k).astype(BF16)
        va_s[dst, :], vb_s[dst, :] = _ones_lane_variants(x_ref[rows, 2 * GROUP_W:3 * GROUP_W])

    prep(xc_ref, pl.ds(0, ctx), pl.ds(0, ctx), None)

    def prep_body(i, carry):
        r = pl.ds(pl.multiple_of(i * tq, tq), tq)
        prep(xl_ref, r, pl.ds(pl.multiple_of(ctx + i * tq, tq), tq), r)
        return carry

    lax.fori_loop(0, seq // tq, prep_body, 0)

    stages = [(_head_mask((1, GROUP_W), h, HEAD_DIM, BF16), *_head_v(h, va_s, vb_s),
               _head_mask((1, GROUP_W), h, HEAD_DIM, F32)) for h in range(N_HEADS)]
    finish = lambda acc: acc
    yc_ref[...] = _context_attention(q_s[0:ctx, :], k_s[:, 0:ctx], stages, finish).astype(yc_ref.dtype)
    _attention_stages(q_s, k_s, yl_ref, p_s, acc_s, stages, finish, ctx=ctx, seq=seq, tq=tq, unroll=unroll)


def _gqa_mixer(p_gqa, cos_t, sin_t, gq, gk, *, n_batch, seq, ctx, tq=ATTN_TQ, unroll=ATTN_UNROLL, lag=ATTN_LAG):
    lat_blocks = n_batch * seq // ctx
    kern = functools.partial(_gqa_kernel, seq=seq, ctx=ctx, tq=tq, unroll=unroll)
    yl, yc = pl.pallas_call(
        kern,
        grid=(n_batch,),
        in_specs=[pl.BlockSpec((seq, IN_GQA), lambda b: (b, 0)),
                  pl.BlockSpec((ctx, IN_GQA), lambda b: (lat_blocks + b, 0)),
                  _resident(cos_t.shape), _resident(sin_t.shape),
                  _resident((1, GROUP_W)), _resident((1, GROUP_W))],
        out_specs=[pl.BlockSpec((seq, GROUP_W), lambda b: (b, 0)),
                   pl.BlockSpec((ctx, GROUP_W), lambda b: (b, 0))],
        out_shape=[jax.ShapeDtypeStruct((n_batch * seq, GROUP_W), BF16),
                   jax.ShapeDtypeStruct((n_batch * ctx, GROUP_W), BF16)],
        scratch_shapes=[pltpu.VMEM((ctx + seq, GROUP_W), BF16), pltpu.VMEM((GROUP_W, ctx + seq), BF16)]
                       + [pltpu.VMEM((ctx + seq, GROUP_W), BF16)] * 2
                       + [pltpu.VMEM((lag, tq, ctx + seq), BF16), pltpu.VMEM((tq, GROUP_W), F32)],
        compiler_params=_cparams("parallel"),
        name="gqa_mixer",
    )(p_gqa, p_gqa, cos_t, sin_t, gq, gk)
    return yl, yc


def _diff_kernel(xl_ref, xc_ref, cos_ref, sin_ref, gq_ref, gk_ref, lam_ref, sub_ref, yl_ref, yc_ref,
                 q_s, k_s, va_s, vb_s, p_s, acc_s, *, seq, ctx, tq, unroll, lam_init):
    ones_sub = _group_ones(GROUP_W, DIFF_SUB)
    ones_head = _group_ones(GROUP_W, HEAD_DIM)
    scale = DIFF_SUB ** -0.5 * LOG2E
    lp = lam_ref[...]
    lam = (jnp.exp(jnp.sum(lp[0:1] * lp[1:2], axis=-1, keepdims=True))
           - jnp.exp(jnp.sum(lp[2:3] * lp[3:4], axis=-1, keepdims=True)) + lam_init)

    def prep(x_ref, rows, dst, rope_rows):
        q = _group_rmsnorm(x_ref[rows, 0:GROUP_W].astype(F32), gq_ref[...], ones_sub, DIFF_SUB)
        k = _group_rmsnorm(x_ref[rows, GROUP_W:2 * GROUP_W].astype(F32), gk_ref[...], ones_sub, DIFF_SUB)
        if rope_rows is not None:
            cos_t, sin_t = cos_ref[rope_rows, :], sin_ref[rope_rows, :]
            q, k = _rope(q, cos_t, sin_t, DIFF_SUB // 2), _rope(k, cos_t, sin_t, DIFF_SUB // 2)
        q_s[dst, :] = (q * scale).astype(BF16)
        k_s[:, dst] = jnp.transpose(k).astype(BF16)
        va_s[dst, :], vb_s[dst, :] = _ones_lane_variants(x_ref[rows, 2 * GROUP_W:3 * GROUP_W])

    prep(xc_ref, pl.ds(0, ctx), pl.ds(0, ctx), None)

    def prep_body(i, carry):
        r = pl.ds(pl.multiple_of(i * tq, tq), tq)
        prep(xl_ref, r, pl.ds(pl.multiple_of(ctx + i * tq, tq), tq), r)
        return carry

    lax.fori_loop(0, seq // tq, prep_body, 0)

    stages = []
    for h in range(N_HEADS):
        head = _head_mask((1, GROUP_W), h, HEAD_DIM, F32)
        for j, weight in enumerate((head, -lam * head)):
            stages.append((_head_mask((1, GROUP_W), 2 * h + j, DIFF_SUB, BF16), *_head_v(h, va_s, vb_s), weight))
    finish = lambda acc: _group_rmsnorm(acc, sub_ref[...], ones_head, HEAD_DIM) * (1.0 - lam_init)
    yc_ref[...] = _context_attention(q_s[0:ctx, :], k_s[:, 0:ctx], stages, finish).astype(yc_ref.dtype)
    _attention_stages(q_s, k_s, yl_ref, p_s, acc_s, stages, finish, ctx=ctx, seq=seq, tq=tq, unroll=unroll)


def _diff_mixer(p_df, cos_t, sin_t, gq, gk, lam_p, sub_g, *, lam_init, n_batch, seq, ctx,
                tq=ATTN_TQ, unroll=ATTN_UNROLL, lag=ATTN_LAG):
    lat_blocks = n_batch * seq // ctx
    kern = functools.partial(_diff_kernel, seq=seq, ctx=ctx, tq=tq, unroll=unroll, lam_init=lam_init)
    yl, yc = pl.pallas_call(
        kern,
        grid=(n_batch,),
        in_specs=[pl.BlockSpec((seq, IN_DIFF), lambda b: (b, 0)),
                  pl.BlockSpec((ctx, IN_DIFF), lambda b: (lat_blocks + b, 0)),
                  _resident(cos_t.shape), _resident(sin_t.shape),
                  _resident((1, GROUP_W)), _resident((1, GROUP_W)),
                  _resident(lam_p.shape), _resident((1, GROUP_W))],
        out_specs=[pl.BlockSpec((seq, GROUP_W), lambda b: (b, 0)),
                   pl.BlockSpec((ctx, GROUP_W), lambda b: (b, 0))],
        out_shape=[jax.ShapeDtypeStruct((n_batch * seq, GROUP_W), BF16),
                   jax.ShapeDtypeStruct((n_batch * ctx, GROUP_W), BF16)],
        scratch_shapes=[pltpu.VMEM((ctx + seq, GROUP_W), BF16), pltpu.VMEM((GROUP_W, ctx + seq), BF16)]
                       + [pltpu.VMEM((ctx + seq, GROUP_W), BF16)] * 2
                       + [pltpu.VMEM((lag, tq, ctx + seq), BF16), pltpu.VMEM((tq, GROUP_W), F32)],
        compiler_params=_cparams("parallel"),
        name="diff_attention_mixer",
    )(p_df, p_df, cos_t, sin_t, gq, gk, lam_p, sub_g)
    return yl, yc


def _dot_split2(x, w_bf16):
    hi = x.astype(BF16)
    lo = (x - hi.astype(F32)).astype(BF16)
    return _dot(hi, w_bf16) + _dot(lo, w_bf16)


def _mlstm_kernel(xl_ref, xc_ref, gl_ref, gc_ref, cw_ref, cb_ref, gb_ref, ng_ref, yl_ref, yc_ref,
                  q_s, k_s, hsum, c_s, n_s, m_s, *, seq, ctx, unroll):
    t = ML_CHUNK
    n_chunks = seq // t
    ones_head = _group_ones(GROUP_W, HEAD_DIM)
    block_diag = ones_head.astype(F32)
    r_i = lax.broadcasted_iota(jnp.int32, (t, t), 0)
    c_i = lax.broadcasted_iota(jnp.int32, (t, t), 1)
    tril_b = jnp.where(c_i <= r_i, 1.0, 0.0).astype(BF16)
    causal = (c_i <= r_i, c_i >= r_i)
    row_g = lax.broadcasted_iota(jnp.int32, (t, IN_GATE), 0)
    sel_r = lax.broadcasted_iota(jnp.int32, (IN_GATE, GROUP_W), 0)
    sel_c = lax.broadcasted_iota(jnp.int32, (IN_GATE, GROUP_W), 1) // HEAD_DIM
    select = [jnp.where(sel_r == 8 * d + sel_c, 1.0, 0.0).astype(BF16) for d in range(2)]
    q_masks = [_head_mask((1, GROUP_W), h, HEAD_DIM, BF16) for h in range(N_HEADS)]

    for x_ref, rows, off in ((xc_ref, ctx, 0), (xl_ref, seq, ctx)):
        qk = _silu(_dwconv3(x_ref[:, 0:2 * GROUP_W].astype(F32), cw_ref[...], cb_ref[...]))
        q_s[off:off + rows, :] = (qk[:, 0:GROUP_W] * (HEAD_DIM ** -0.5)).astype(BF16)
        k_s[off:off + rows, :] = qk[:, GROUP_W:2 * GROUP_W].astype(BF16)

    hsum[...] = jnp.zeros_like(hsum)
    c_s[...] = jnp.zeros_like(c_s)
    n_s[...] = jnp.zeros_like(n_s)
    m_s[...] = jnp.zeros_like(m_s)

    def running_max(x, d):
        step = 1
        while step < t:
            if d == 0:
                shifted, valid = pltpu.roll(x, step, 0), row_g >= step
            else:
                shifted, valid = pltpu.roll(x, t - step, 0), row_g < t - step
            x = jnp.maximum(x, jnp.where(valid, shifted, -jnp.inf))
            step *= 2
        return x

    def chunk(d, rows, v, g_raw):
        g = g_raw + gb_ref[...]
        logf = pltpu.roll(jax.nn.log_sigmoid(g), IN_GATE - N_HEADS, 1)
        pre = _dot_exact_rhs(tril_b, logf)
        tot = pre[t - 1:t, :]
        cum = pre if d == 0 else tot - pre + logf
        r = g - cum
        m_prev = m_s[d]
        log_inter = cum + m_prev
        m_t = jnp.maximum(log_inter, cum + running_max(r, d))
        a_t = cum - m_t
        log_g = (tot - cum) + g
        m_new = jnp.maximum(tot + m_prev, jnp.max(log_g, axis=0, keepdims=True))
        m_s[d] = m_new
        rowwise = jnp.concatenate([jnp.exp(log_inter - m_t), jnp.exp(-m_t), jnp.exp(log_g - m_new),
                                   jnp.broadcast_to(jnp.exp(tot + m_prev - m_new), (8, IN_GATE))], axis=0)
        full = _dot_split2(rowwise, select[d])
        w_inter, e_m, w_g, w_c = full[0:t], full[t:2 * t], full[2 * t:3 * t], full[3 * t:3 * t + 1]

        q, k = q_s[rows, :], k_s[rows, :]
        r_t = jnp.transpose(r)
        pair = [None, None]
        for h in range(N_HEADS):
            col = 8 * d + h
            w_intra = jnp.exp(jnp.where(causal[d], a_t[:, col:col + 1] + r_t[col:col + 1, :], -jnp.inf))
            s = (_dot_nt(q * q_masks[h], k) * w_intra).astype(BF16)
            o = _dot(s, v * q_masks[h] + q_masks[(h + 2) % N_HEADS])
            pair[h // 2] = o if pair[h // 2] is None else pair[h // 2] + o
        half = GROUP_W // 2
        num_i = jnp.concatenate([pair[0][:, :half], pair[1][:, half:]], axis=1)
        den_i = jnp.concatenate([pair[0][:, half:], pair[1][:, :half]], axis=1)
        ct, n_full = c_s[d], n_s[d]
        inter = _dot(q, ct.astype(BF16))
        qn = _dot((q.astype(F32) * n_full).astype(BF16), ones_head)
        num = w_inter * inter + num_i
        den = w_inter * qn + den_i
        kw = k.astype(F32) * w_g
        c_s[d] = w_c * ct + _dot_tn(kw.astype(BF16), v) * block_diag
        n_s[d] = w_c * n_full + jnp.sum(kw, axis=0, keepdims=True)
        return num / jnp.maximum(jnp.abs(den), e_m)

    assert ctx == t
    v_c = xc_ref[:, 2 * GROUP_W:3 * GROUP_W]
    hsum[0:t, :] = chunk(0, pl.ds(0, t), v_c, gc_ref[...]) + chunk(1, pl.ds(0, t), v_c, gc_ref[...])

    def body(j, carry):
        for d in range(2):
            cidx = j if d == 0 else n_chunks - 1 - j
            lat = pl.ds(pl.multiple_of(cidx * t, t), t)
            rows = pl.ds(pl.multiple_of(ctx + cidx * t, t), t)
            hsum[rows, :] += chunk(d, rows, xl_ref[lat, 2 * GROUP_W:3 * GROUP_W], gl_ref[lat, :])
        return carry

    lax.fori_loop(0, n_chunks, body, 0, unroll=unroll)

    for x_ref, y_ref, rows, off in ((xc_ref, yc_ref, ctx, 0), (xl_ref, yl_ref, seq, ctx)):
        hn = _group_rmsnorm(hsum[off:off + rows, :], ng_ref[...], ones_head, HEAD_DIM)
        y_ref[...] = (hn * jax.nn.sigmoid(x_ref[:, 3 * GROUP_W:4 * GROUP_W].astype(F32))).astype(y_ref.dtype)


def _mlstm_mixer(p_ml, p_gate, conv_w, conv_b, gate_b, norm_g, *, n_batch, seq, ctx, unroll=1):
    lat_blocks = n_batch * seq // ctx
    kern = functools.partial(_mlstm_kernel, seq=seq, ctx=ctx, unroll=unroll)
    yl, yc = pl.pallas_call(
        kern,
        grid=(n_batch,),
        in_specs=[pl.BlockSpec((seq, IN_ML), lambda b: (b, 0)),
                  pl.BlockSpec((ctx, IN_ML), lambda b: (lat_blocks + b, 0)),
                  pl.BlockSpec((seq, IN_GATE), lambda b: (b, 0)),
                  pl.BlockSpec((ctx, IN_GATE), lambda b: (lat_blocks + b, 0)),
                  _resident(conv_w.shape), _resident(conv_b.shape), _resident(gate_b.shape), _resident(norm_g.shape)],
        out_specs=[pl.BlockSpec((seq, GROUP_W), lambda b: (b, 0)),
                   pl.BlockSpec((ctx, GROUP_W), lambda b: (b, 0))],
        out_shape=[jax.ShapeDtypeStruct((n_batch * seq, GROUP_W), BF16),
                   jax.ShapeDtypeStruct((n_batch * ctx, GROUP_W), BF16)],
        scratch_shapes=[pltpu.VMEM((ctx + seq, GROUP_W), BF16), pltpu.VMEM((ctx + seq, GROUP_W), BF16),
                        pltpu.VMEM((ctx + seq, GROUP_W), F32),
                        pltpu.VMEM((2, GROUP_W, GROUP_W), F32), pltpu.VMEM((2, 1, GROUP_W), F32),
                        pltpu.VMEM((2, 1, IN_GATE), F32)],
        compiler_params=_cparams("parallel"),
        name="mlstm_mixer",
    )(p_ml, p_ml, p_gate, p_gate, conv_w, conv_b, gate_b, norm_g)
    return yl, yc


def _phase_tables(idx_a, idx_b, n):
    ang = (2.0 * math.pi / n) * ((idx_a[:, None] * idx_b[None, :]) % n).astype(F32)
    return jnp.cos(ang), jnp.sin(ang)


def _dft_matrices(length, n):
    nf, lo = n // 2, 64
    k = jnp.arange(nf, dtype=jnp.int32)
    k1, k0 = lo * jnp.arange(nf // lo, dtype=jnp.int32), jnp.arange(lo, dtype=jnp.int32)
    j = jnp.arange(length, dtype=jnp.int32)
    m = j + length // 2
    (ca, sa), (cb, sb) = _phase_tables(k1, j, n), _phase_tables(k0, j, n)
    cos_f = (ca[:, None, :] * cb[None, :, :] - sa[:, None, :] * sb[None, :, :]).reshape(nf, length)
    sin_f = (sa[:, None, :] * cb[None, :, :] + ca[:, None, :] * sb[None, :, :]).reshape(nf, length)
    alt = jnp.where(j % 2 == 0, 1.0, -1.0).astype(F32)
    fwd = jnp.concatenate([cos_f, jnp.where(k[:, None] == 0, alt[None, :], sin_f)], axis=0)
    (ca, sa), (cb, sb) = _phase_tables(m, k1, n), _phase_tables(m, k0, n)
    cos_i = (ca[:, :, None] * cb[:, None, :] - sa[:, :, None] * sb[:, None, :]).reshape(length, nf)
    sin_i = (sa[:, :, None] * cb[:, None, :] + ca[:, :, None] * sb[:, None, :]).reshape(length, nf)
    wk = jnp.where(k == 0, 1.0, 2.0).astype(F32) / n
    alt_m = jnp.where(m % 2 == 0, 1.0, -1.0).astype(F32) / n
    inv = jnp.concatenate([cos_i * wk[None, :], jnp.where(k[None, :] == 0, alt_m[:, None], sin_i * (2.0 / n))], axis=1)
    return fwd.astype(BF16), inv.astype(BF16)


def _hyena_filter_consts(length):
    t = jnp.arange(length, dtype=F32)
    tn = t / length
    bands = jnp.arange(1, HY_POS_BANDS + 1, dtype=F32)
    ang = 2.0 * math.pi * tn[:, None] * bands
    feats = jnp.concatenate([tn[:, None], jnp.cos(ang), jnp.sin(ang)], axis=-1)
    feats = jnp.pad(feats, ((0, 0), (0, 128 - feats.shape[1])))
    dist = jnp.abs(t - length // 2) / (length / 2)
    deltas = jnp.abs(jnp.linspace(math.log(HY_DECAY_TARGET) / HY_SLOW_DECAY,
                                  math.log(HY_DECAY_TARGET) / HY_FAST_DECAY, GROUP_W, dtype=F32))
    window = jnp.exp(-dist[:, None] * jnp.tile(deltas, 2))
    return feats, window


def _filter_kernel(feats_ref, win_ref, w1_ref, b1_ref, w2_ref, b2_ref, w3_ref, b3_ref, f_ref, p_ref, qz_ref, p2_ref):
    h = jnp.sin(_dot_f32ish(feats_ref[...], w1_ref[0]) + b1_ref[0])
    h = jnp.sin(_dot_f32ish(h, w2_ref[0]) + b2_ref[0])
    h = (_dot_f32ish(h, w3_ref[0]) + b3_ref[0]) * win_ref[...]
    h = h / jnp.sum(jnp.abs(h), axis=0, keepdims=True)
    h1 = h.astype(BF16)
    h2 = (h - h1.astype(F32)).astype(BF16)
    nf = f_ref.shape[0] // 2
    step = math.gcd(nf, 512)
    for r in range(0, nf, step):
        fc, fs = f_ref[r:r + step, :], f_ref[nf + r:nf + r + step, :]
        p = _dot(fc, h1) + _dot(fc, h2)
        q = _dot(fs, h1) + _dot(fs, h2)
        p_ref[0, r:r + step, :] = p
        if r == 0:
            first = lax.broadcasted_iota(jnp.int32, p.shape, 0) == 0
            qz_ref[0, r:r + step, :] = jnp.where(first, 0.0, q)
            p2_ref[0, r:r + step, :] = jnp.where(first, q, p)
        else:
            qz_ref[0, r:r + step, :] = q
            p2_ref[0, r:r + step, :] = p


def _hyena_filter_spectra(length, fwd, w1, b1, w2, b2, w3, b3):
    depth = w1.shape[0]
    feats, window = _hyena_filter_consts(length)
    nf = fwd.shape[0] // 2
    pad2 = lambda a, r, c: jnp.pad(a, ((0, 0), (0, r - a.shape[1]), (0, c - a.shape[2])))
    w1p, w2p = pad2(w1, 128, 128), pad2(w2, 128, 128)
    w3p = pad2(w3, 128, w3.shape[2])
    b1p = jnp.pad(b1, ((0, 0), (0, 128 - b1.shape[1])))[:, None, :]
    b2p = jnp.pad(b2, ((0, 0), (0, 128 - b2.shape[1])))[:, None, :]
    b3p = b3[:, None, :]
    lay = lambda a: pl.BlockSpec((1,) + a.shape[1:], lambda l, o: (l, 0, 0))
    per_order = lambda rows: pl.BlockSpec((1, rows, GROUP_W), lambda l, o: (l, 0, o))
    out = jax.ShapeDtypeStruct((depth, nf, 2 * GROUP_W), F32)
    return pl.pallas_call(
        _filter_kernel,
        grid=(depth, 2),
        in_specs=[_resident(feats.shape), pl.BlockSpec((length, GROUP_W), lambda l, o: (0, o)),
                  lay(w1p), lay(b1p), lay(w2p), lay(b2p), per_order(128), per_order(1), _resident(fwd.shape)],
        out_specs=[per_order(nf)] * 3,
        out_shape=[out, out, out],
        compiler_params=_cparams("parallel", "parallel"),
        name="hyena_filter_spectra",
    )(feats, window, w1p, b1p, w2p, b2p, w3p, b3p, fwd)


def _hyena_conv_kernel(sig_ref, gate_ref, cws_ref, cbs_ref, cwg_ref, cbg_ref, skip_ref, f_ref, g_ref,
                       p_ref, qz_ref, p2_ref, o_ref, *, sig_conv):
    s = sig_ref[...].astype(F32)
    if sig_conv:
        s = _dwconv3(s, cws_ref[...], cbs_ref[...])
    gate = _dwconv3(gate_ref[...].astype(F32), cwg_ref[...], cbg_ref[...])
    sb = s.astype(BF16)
    nf = f_ref.shape[0] // 2
    a = _dot(f_ref[0:nf, :], sb)
    b = _dot(f_ref[nf:2 * nf, :], sb)
    ay = a * p_ref[0] - b * qz_ref[0]
    by = a * qz_ref[0] + b * p2_ref[0]
    y = _dot(g_ref[:, 0:nf], ay.astype(BF16)) + _dot(g_ref[:, nf:2 * nf], by.astype(BF16))
    o_ref[...] = (gate * (y + skip_ref[...] * s)).astype(o_ref.dtype)


def _hyena_conv(sig, sig_col, gate, gate_col, conv_w, conv_b, skip, fwd, inv, spectra, order, layer,
                *, rows, first_block, n_batch, out_dtype, sig_conv):
    cw = lambda col: pl.BlockSpec((3, GROUP_W), lambda b: (0, col))
    cb = lambda col: pl.BlockSpec((1, GROUP_W), lambda b: (0, col))
    spec = pl.BlockSpec((1, fwd.shape[0] // 2, GROUP_W), lambda b: (layer, 0, order), pipeline_mode=pl.Buffered(1))
    kern = functools.partial(_hyena_conv_kernel, sig_conv=sig_conv)
    return pl.pallas_call(
        kern,
        grid=(n_batch,),
        in_specs=[pl.BlockSpec((rows, GROUP_W), lambda b: (first_block + b, sig_col)),
                  pl.BlockSpec((rows, GROUP_W), lambda b: (first_block + b, gate_col)),
                  cw(sig_col if sig_conv else 0), cb(sig_col if sig_conv else 0), cw(gate_col), cb(gate_col),
                  pl.BlockSpec((1, GROUP_W), lambda b: (0, 0)),
                  _resident(fwd.shape), _resident(inv.shape), spec, spec, spec],
        out_specs=pl.BlockSpec((rows, GROUP_W), lambda b: (b, 0)),
        out_shape=jax.ShapeDtypeStruct((n_batch * rows, GROUP_W), out_dtype),
        compiler_params=_cparams("parallel"),
        name="hyena_long_conv",
    )(sig, gate, conv_w, conv_b, conv_w, conv_b, skip, fwd, inv, *spectra)


def _hyena_mixer(p_hy, conv_w, conv_b, skip, dft, spectra, layer, *, rows, first_block, n_batch):
    fwd, inv = dft
    common = dict(rows=rows, n_batch=n_batch)
    z = _hyena_conv(p_hy, 0, p_hy, 1, conv_w, conv_b, skip[0:1], fwd, inv, spectra, 0, layer,
                    first_block=first_block, out_dtype=F32, sig_conv=True, **common)
    cw = lambda col: pl.BlockSpec((3, GROUP_W), lambda b: (0, col))
    cb = lambda col: pl.BlockSpec((1, GROUP_W), lambda b: (0, col))
    spec = pl.BlockSpec((1, fwd.shape[0] // 2, GROUP_W), lambda b: (layer, 0, 1), pipeline_mode=pl.Buffered(1))
    kern = functools.partial(_hyena_conv_kernel, sig_conv=False)
    return pl.pallas_call(
        kern,
        grid=(n_batch,),
        in_specs=[pl.BlockSpec((rows, GROUP_W), lambda b: (b, 0)),
                  pl.BlockSpec((rows, GROUP_W), lambda b: (first_block + b, 2)),
                  cw(0), cb(0), cw(2), cb(2),
                  pl.BlockSpec((1, GROUP_W), lambda b: (0, 0)),
                  _resident(fwd.shape), _resident(inv.shape), spec, spec, spec],
        out_specs=pl.BlockSpec((rows, GROUP_W), lambda b: (b, 0)),
        out_shape=jax.ShapeDtypeStruct((n_batch * rows, GROUP_W), BF16),
        compiler_params=_cparams("parallel"),
        name="hyena_long_conv_2",
    )(z, p_hy, conv_w, conv_b, conv_w, conv_b, skip[1:2], fwd, inv, *spectra)


def _rope_tables(length, dim, reps):
    rows = length // GRID_W
    row = jnp.repeat(jnp.arange(rows), GRID_W).astype(F32)
    col = jnp.tile(jnp.arange(GRID_W), rows).astype(F32)
    n_freq = dim // 4
    inv = ROPE_THETA ** (-jnp.arange(n_freq, dtype=F32) / n_freq)
    ang = jnp.concatenate([row[:, None] * inv, col[:, None] * inv], axis=-1)
    cos, sin = jnp.cos(ang), jnp.sin(ang)
    return (jnp.tile(jnp.concatenate([cos, cos], axis=-1), (1, reps)),
            jnp.tile(jnp.concatenate([-sin, sin], axis=-1), (1, reps)))


def kernel(x, c, ctx, c_ctx, ada_w, ada_b, norm_g, ffn_w13, ffn_w2, w_in, w_out, ml_gate_b, ml_conv_w, ml_conv_b,
           ml_norm_g, gqa_qk_g, hy_conv_w, hy_conv_b, hy_filt_w1, hy_filt_b1, hy_filt_w2, hy_filt_b2, hy_filt_w3,
           hy_filt_b3, hy_skip, diff_qk_g, diff_lambda, diff_subln_g):
    n_batch, seq, d = x.shape
    n_ctx = ctx.shape[1]
    depth = ada_w.shape[0]
    n_lat = n_batch * seq
    n_all = n_lat + n_batch * n_ctx
    assert n_batch * n_ctx == seq and n_ctx == ML_CHUNK and seq % ROW_TILE == 0

    c_all = jnp.pad(jnp.concatenate([c, c_ctx[None]], axis=0), ((0, 16 - n_batch - 1), (0, 0)))
    mod = _modulation(c_all, ada_w, ada_b)[:, :n_batch + 1].reshape(depth, n_batch + 1, N_MOD, d)

    w13 = ffn_w13.astype(BF16)
    w2 = ffn_w2.astype(BF16)
    w_out_b = w_out.astype(BF16)

    rope_gqa = _rope_tables(seq, HEAD_DIM, N_HEADS)
    rope_diff = _rope_tables(seq, DIFF_SUB, 2 * N_HEADS)
    dft_l = _dft_matrices(seq, 3 * seq // 2)
    dft_c = _dft_matrices(n_ctx, 2 * n_ctx)
    filt = (hy_filt_w1, hy_filt_b1, hy_filt_w2, hy_filt_b2, hy_filt_w3, hy_filt_b3)
    spec_l = _hyena_filter_spectra(seq, dft_l[0], *filt)
    spec_c = _hyena_filter_spectra(n_ctx, dft_c[0], *filt)

    h = (x.reshape(n_lat, d), ctx.reshape(n_batch * n_ctx, d))
    dims = dict(seq=seq, n_batch=n_batch)
    gqa_cfg = [dict(tq=256, unroll=4, lag=2), dict(tq=256, unroll=8, lag=2), dict(tq=512, unroll=2, lag=2),
               dict(tq=512, unroll=4, lag=2)]
    diff_cfg = [dict(tq=256, unroll=2, lag=2), dict(tq=512, unroll=1, lag=2), dict(tq=512, unroll=2, lag=2),
                dict(tq=256, unroll=4, lag=2)]
    ffn_cfg = [dict(tm=1024, chunk=256), dict(tm=1024, chunk=256), dict(tm=1024, chunk=256), dict(tm=512, chunk=512)]
    inproj_tm = [1024, 1024, 512, 1024]
    ml_unroll = [2, 4, 8, 2]
    for l in range(depth):
        need_ctx = l < depth - 1
        lam_init = 0.8 - 0.6 * math.exp(-0.3 * l)
        h = _ffn_half(h, mod[l], norm_g[l, 0], w13, w2, layer=l, half=0, base=0, n_rows=n_all, **ffn_cfg[l], **dims)
        p_ml, p_gqa, p_hy, p_df, p_gate = _in_projection(h, mod[l], norm_g[l, 1], w_in, layer=l, tm=inproj_tm[l],
                                                          **dims)

        gate_b = jnp.pad(ml_gate_b[l], (0, IN_GATE - ml_gate_b.shape[1]))[None]
        y_ml = _mlstm_mixer(p_ml, p_gate, ml_conv_w[l], ml_conv_b[l][None], gate_b, ml_norm_g[l][None],
                            n_batch=n_batch, seq=seq, ctx=n_ctx, unroll=ml_unroll[l])
        tile_g = lambda g, reps: jnp.tile(g, reps)[None]
        y_gqa = _gqa_mixer(p_gqa, *rope_gqa, tile_g(gqa_qk_g[l, 0], N_HEADS), tile_g(gqa_qk_g[l, 1], N_HEADS),
                           n_batch=n_batch, seq=seq, ctx=n_ctx, **gqa_cfg[l])
        y_df = _diff_mixer(p_df, *rope_diff, tile_g(diff_qk_g[l, 0], 2 * N_HEADS), tile_g(diff_qk_g[l, 1], 2 * N_HEADS),
                           diff_lambda[l], tile_g(diff_subln_g[l], N_HEADS),
                           lam_init=lam_init, n_batch=n_batch, seq=seq, ctx=n_ctx, **diff_cfg[l])
        y_hy = _hyena_mixer(p_hy, hy_conv_w[l], hy_conv_b[l][None], hy_skip[l], dft_l, spec_l, l,
                            rows=seq, first_block=0, n_batch=n_batch)
        y_hy_c = y_hy if not need_ctx else _hyena_mixer(
            p_hy, hy_conv_w[l], hy_conv_b[l][None], hy_skip[l], dft_c, spec_c, l,
            rows=n_ctx, first_block=n_lat // n_ctx, n_batch=n_batch)
        h = _ffn_half(h, mod[l], norm_g[l, 2], w13, w2, layer=l, half=1, base=6,
                      n_rows=n_all if need_ctx else n_lat,
                      mix=[y_ml, y_gqa, (y_hy, y_hy_c), y_df], w_out=w_out_b, **ffn_cfg[l], **dims)
    return h[:n_lat].reshape(n_batch, seq, d)
```

```python
import functools
import math

import jax
import jax.numpy as jnp
from jax import lax
from jax.experimental import pallas as pl
from jax.experimental.pallas import tpu as pltpu

F32 = jnp.float32
BF16 = jnp.bfloat16

EPS = 1e-6
ROPE_THETA = 10000.0
GRID_W = 64
N_MOD = 9
GROUP_W = 256
HEAD_DIM = 64
N_HEADS = 4
DIFF_SUB = 32
ML_CHUNK = 256
HY_POS_BANDS = 16
HY_DECAY_TARGET = 1e-2
HY_FAST_DECAY = 0.3
HY_SLOW_DECAY = 1.5
ROW_TILE = 512
VMEM_LIMIT = 56 * 1024 * 1024


def _cparams(*sem):
    return pltpu.CompilerParams(dimension_semantics=sem, vmem_limit_bytes=VMEM_LIMIT)


def _resident(shape):
    nd = len(shape)
    return pl.BlockSpec(shape, lambda *_: (0,) * nd, pipeline_mode=pl.Buffered(1))


def _resident_slice(arr, index):
    tail = arr.shape[len(index):]
    return pl.BlockSpec((None,) * len(index) + tail, lambda *_: tuple(index) + (0,) * len(tail),
                        pipeline_mode=pl.Buffered(1))


def _dot(a, b):
    return jnp.dot(a, b, preferred_element_type=F32)


def _dot_nt(a, b):
    return lax.dot_general(a, b, (((1,), (1,)), ((), ())), preferred_element_type=F32)


def _dot_tn(a, b):
    return lax.dot_general(a, b, (((0,), (0,)), ((), ())), preferred_element_type=F32)


def _split3(x):
    x1 = x.astype(BF16)
    r = x - x1.astype(F32)
    x2 = r.astype(BF16)
    x3 = (r - x2.astype(F32)).astype(BF16)
    return x1, x2, x3


def _dot_exact_rhs(a_bf16, x):
    x1, x2, x3 = _split3(x)
    return _dot(a_bf16, x1) + _dot(a_bf16, x2) + _dot(a_bf16, x3)


def _dot_f32ish(a, b):
    a1 = a.astype(BF16)
    a2 = (a - a1.astype(F32)).astype(BF16)
    b1 = b.astype(BF16)
    b2 = (b - b1.astype(F32)).astype(BF16)
    return _dot(a1, b1) + _dot(a1, b2) + _dot(a2, b1)


def _silu(x):
    return x * jax.nn.sigmoid(x)


def _modnorm(x, g, shift, scale):
    y = x * lax.rsqrt(jnp.mean(x * x, axis=-1, keepdims=True) + EPS)
    return (y * g) * (1.0 + scale) + shift


def _lane_iota(shape):
    return lax.broadcasted_iota(jnp.int32, shape, len(shape) - 1)


def _group_ones(n, group):
    r = lax.broadcasted_iota(jnp.int32, (n, n), 0) // group
    c = lax.broadcasted_iota(jnp.int32, (n, n), 1) // group
    return jnp.where(r == c, 1.0, 0.0).astype(BF16)


def _group_rmsnorm(x, gain, ones_g, group):
    x2 = x * x
    hi = x2.astype(BF16)
    lo = (x2 - hi.astype(F32)).astype(BF16)
    ms = (_dot(hi, ones_g) + _dot(lo, ones_g)) * (1.0 / group)
    return x * lax.rsqrt(ms + EPS) * gain


def _rope(x, cos_t, sin_t, half):
    n = x.shape[-1]
    left = pltpu.roll(x, n - half, 1)
    right = pltpu.roll(x, half, 1)
    sw = jnp.where((_lane_iota(x.shape) & (2 * half - 1)) < half, left, right)
    return x * cos_t + sw * sin_t


def _dwconv3(x, w, b):
    rows = x.shape[0]
    row = lax.broadcasted_iota(jnp.int32, x.shape, 0)
    xm = jnp.where(row == 0, 0.0, pltpu.roll(x, 1, 0))
    xp = jnp.where(row == rows - 1, 0.0, pltpu.roll(x, rows - 1, 0))
    return xm * w[0:1] + x * w[1:2] + xp * w[2:3] + b


def _head_mask(shape, h, width, dtype):
    lane = _lane_iota(shape)
    return jnp.where((lane >= h * width) & (lane < (h + 1) * width), 1.0, 0.0).astype(dtype)


def _mod_kernel(c_ref, w_ref, b_ref, o_ref):
    sc = _silu(c_ref[...]).astype(BF16)
    o_ref[0] = _dot(sc, w_ref[0].astype(BF16)) + b_ref[0]


def _modulation(c_all, ada_w, ada_b):
    depth, d, nmod = ada_w.shape
    rows = c_all.shape[0]
    tn = 1536
    return pl.pallas_call(
        _mod_kernel,
        grid=(depth, nmod // tn),
        in_specs=[pl.BlockSpec((rows, d), lambda l, j: (0, 0)),
                  pl.BlockSpec((1, d, tn), lambda l, j: (l, 0, j)),
                  pl.BlockSpec((1, 1, tn), lambda l, j: (l, 0, j))],
        out_specs=pl.BlockSpec((1, rows, tn), lambda l, j: (l, 0, j)),
        out_shape=jax.ShapeDtypeStruct((depth, rows, nmod), F32),
        compiler_params=_cparams("parallel", "parallel"),
        name="adaln_modulation",
    )(c_all, ada_w, ada_b.reshape(depth, 1, nmod))


def _row_specs(width, tile, n_lat_tiles):
    return [pl.BlockSpec((tile, width), lambda i: (jnp.minimum(i, n_lat_tiles - 1), 0)),
            pl.BlockSpec((tile, width), lambda i: (jnp.maximum(i - n_lat_tiles, 0), 0))]


def _pick_rows(lat_ref, ctx_ref, n_lat_tiles):
    return jnp.where(pl.program_id(0) < n_lat_tiles, lat_ref[...], ctx_ref[...])


def _ffn_kernel(*refs, base, hidden, chunk, n_lat_tiles, split_x, n_mix):
    refs = list(refs)
    x = _pick_rows(refs.pop(0), refs.pop(0), n_lat_tiles) if split_x else refs.pop(0)[...]
    mod_ref, g_ref, w13_ref, w2_ref = refs[:4]
    mix_refs, (o_ref, acc_ref) = refs[4:-2], refs[-2:]
    if n_mix:
        wo_ref = mix_refs[-1]
        mixed = None
        for k in range(n_mix):
            y = _pick_rows(mix_refs[2 * k], mix_refs[2 * k + 1], n_lat_tiles)
            part = _dot(y, wo_ref[k * GROUP_W:(k + 1) * GROUP_W, :])
            mixed = part if mixed is None else mixed + part
        x = x + mod_ref[0, 5:6] * mixed
    shift, scale, gate = mod_ref[0, base:base + 1], mod_ref[0, base + 1:base + 2], mod_ref[0, base + 2:base + 3]
    xn = _modnorm(x, g_ref[...], shift, scale).astype(BF16)
    for start in range(0, hidden, chunk):
        size = min(chunk, hidden - start)
        a = _dot(xn, w13_ref[:, start:start + size])
        b = _dot(xn, w13_ref[:, hidden + start:hidden + start + size])
        part = _dot((_silu(a) * b).astype(BF16), w2_ref[start:start + size, :])
        if start == 0:
            acc_ref[...] = part
        else:
            acc_ref[...] += part
    o_ref[...] = x + (0.5 * gate) * acc_ref[...]


def _mod_index(tile, seq, n_batch):
    return lambda i: (jnp.minimum((i * tile) // seq, n_batch), 0, 0)


def _ffn_half(h, mod, g, w13, w2, *, layer, half, base, n_rows, seq, n_batch, mix=None, w_out=None,
              tm=ROW_TILE, chunk=512):
    split_x = isinstance(h, tuple)
    d = w13.shape[-2]
    hidden = w2.shape[-2]
    n_lat_tiles = n_batch * seq // tm
    kern = functools.partial(_ffn_kernel, base=base, hidden=hidden, chunk=chunk, n_lat_tiles=n_lat_tiles,
                             split_x=split_x, n_mix=len(mix) if mix else 0)
    x_specs = _row_specs(d, tm, n_lat_tiles) if split_x else [pl.BlockSpec((tm, d), lambda i: (i, 0))]
    mix_args, mix_specs = [], []
    for pair in mix or ():
        mix_args += list(pair)
        mix_specs += _row_specs(GROUP_W, tm, n_lat_tiles)
    if mix:
        mix_args.append(w_out)
        mix_specs.append(_resident_slice(w_out, (layer,)))
    return pl.pallas_call(
        kern,
        grid=(n_rows // tm,),
        in_specs=x_specs + [pl.BlockSpec((1, N_MOD, d), _mod_index(tm, seq, n_batch)),
                            _resident((1, d)), _resident_slice(w13, (layer, half)),
                            _resident_slice(w2, (layer, half))] + mix_specs,
        out_specs=pl.BlockSpec((tm, d), lambda i: (i, 0)),
        out_shape=jax.ShapeDtypeStruct((n_rows, d), F32),
        scratch_shapes=[pltpu.VMEM((tm, d), F32)],
        compiler_params=_cparams("parallel"),
        name="swiglu_half_step",
    )(*(h if split_x else (h,)), mod, g.reshape(1, d), w13, w2, *mix_args)


IN_ML, IN_GQA, IN_HY, IN_DIFF, IN_GATE = 1024, 768, 768, 768, 128


def _in_weight_pieces():
    ml, n_gate = 4 * GROUP_W, 16
    o_gqa = ml + n_gate
    o_hy = o_gqa + 2 * GROUP_W
    o_df = o_hy + 3 * GROUP_W
    pieces = [(0, 0, ml), (ml, o_gqa, GROUP_W)]
    for part, src in enumerate((o_gqa + GROUP_W, o_gqa + GROUP_W + 2 * HEAD_DIM)):
        for h in range(N_HEADS):
            pieces.append((ml + (part + 1) * GROUP_W + h * HEAD_DIM, src + (h // 2) * HEAD_DIM, HEAD_DIM))
    pieces += [(IN_ML + IN_GQA, o_hy, 3 * GROUP_W), (IN_ML + IN_GQA + IN_HY, o_df, 3 * GROUP_W),
               (IN_ML + IN_GQA + IN_HY + IN_DIFF, ml, n_gate)]
    return pieces


def _inproj_kernel(x_ref, mod_ref, g_ref, w_ref, ml_ref, gqa_ref, hy_ref, df_ref, gate_ref, wb_s):
    @pl.when(pl.program_id(0) == 0)
    def _():
        wb_s[:, IN_ML + IN_GQA + IN_HY + IN_DIFF:] = jnp.zeros((wb_s.shape[0], IN_GATE), BF16)
        for dst, src, width in _in_weight_pieces():
            wb_s[:, dst:dst + width] = w_ref[:, src:src + width].astype(BF16)

    xn = _modnorm(x_ref[...], g_ref[...], mod_ref[0, 3:4], mod_ref[0, 4:5]).astype(BF16)
    off = 0
    for ref, width in ((ml_ref, IN_ML), (gqa_ref, IN_GQA), (hy_ref, IN_HY), (df_ref, IN_DIFF), (gate_ref, IN_GATE)):
        ref[...] = _dot(xn, wb_s[:, off:off + width]).astype(ref.dtype)
        off += width


def _in_projection(h, mod, g, w, *, layer, seq, n_batch, tm=ROW_TILE):
    n_rows, d = h.shape
    widths = (IN_ML, IN_GQA, IN_HY, IN_DIFF, IN_GATE)
    dtypes = (BF16, BF16, BF16, BF16, F32)
    return pl.pallas_call(
        _inproj_kernel,
        grid=(n_rows // tm,),
        in_specs=[pl.BlockSpec((tm, d), lambda i: (i, 0)),
                  pl.BlockSpec((1, N_MOD, d), _mod_index(tm, seq, n_batch)),
                  _resident((1, d)), _resident_slice(w, (layer,))],
        out_specs=[pl.BlockSpec((tm, wd), lambda i: (i, 0)) for wd in widths],
        out_shape=[jax.ShapeDtypeStruct((n_rows, wd), dt) for wd, dt in zip(widths, dtypes)],
        scratch_shapes=[pltpu.VMEM((d, sum(widths)), BF16)],
        compiler_params=_cparams("arbitrary"),
        name="input_projection",
    )(h, mod, g.reshape(1, d), w)


LOG2E = 1.4426950408889634
ATTN_LAG = 2
ATTN_UNROLL = 1
ATTN_TQ = 512


def _ones_lane_variants(v):
    grp = _lane_iota((1, GROUP_W)) // HEAD_DIM
    odd = jnp.where(grp % 2 == 1, 1.0, 0.0).astype(BF16)
    even = jnp.where(grp % 2 == 0, 1.0, 0.0).astype(BF16)
    return v * even + odd, v * odd + even


def _scores_to_probs(qm, k_t):
    s = _dot(qm, k_t)
    return jnp.exp2(s - jnp.max(s, axis=-1, keepdims=True)).astype(BF16)


def _probs_times_v(p, v_aug, sum_lane):
    o = _dot(p, v_aug)
    return o * (1.0 / o[:, sum_lane:sum_lane + 1])


def _head_v(h, va, vb):
    return (va, (h + 1) * HEAD_DIM) if h % 2 == 0 else (vb, (h - 1) * HEAD_DIM)


def _attention_stages(q_s, k_s, y_ref, p_s, acc_s, stages, finish, *, ctx, seq, tq, unroll):
    n_keys = ctx + seq

    def probs(q, st):
        return _scores_to_probs(q * st[0], k_s[:, 0:n_keys])

    def out(p, st):
        return _probs_times_v(p, st[1][0:n_keys, :], st[2]) * st[3]

    p_s[...] = jnp.ones_like(p_s)
    acc_s[...] = jnp.zeros_like(acc_s)
    n, lag = len(stages), p_s.shape[0]

    def tail():
        acc = acc_s[...]
        for c in range(lag):
            acc = acc + out(p_s[c], stages[n - lag + c])
        return finish(acc)

    def body(i, carry):
        q = q_s[pl.ds(pl.multiple_of(ctx + i * tq, tq), tq), :]
        prev = jnp.maximum(i - 1, 0)
        y_ref[pl.ds(pl.multiple_of(prev * tq, tq), tq), :] = tail().astype(y_ref.dtype)
        ps, acc = [], None
        for j in range(n):
            ps.append(probs(q, stages[j]))
            if j >= lag:
                o = out(ps[j - lag], stages[j - lag])
                acc = o if acc is None else acc + o
        for c in range(lag):
            p_s[c] = ps[n - lag + c]
        acc_s[...] = acc
        return carry

    n_tiles = seq // tq
    lax.fori_loop(0, n_tiles, body, 0, unroll=unroll)
    y_ref[(n_tiles - 1) * tq:n_tiles * tq, :] = tail().astype(y_ref.dtype)


def _context_attention(q, k_t, stages, finish):
    acc = None
    for st in stages:
        o = _probs_times_v(_scores_to_probs(q * st[0], k_t), st[1][0:k_t.shape[1], :], st[2]) * st[3]
        acc = o if acc is None else acc + o
    return finish(acc)


def _gqa_kernel(xl_ref, xc_ref, cos_ref, sin_ref, gq_ref, gk_ref, yl_ref, yc_ref, q_s, k_s, va_s, vb_s, p_s, acc_s,
                *, seq, ctx, tq, unroll):
    ones_g = _group_ones(GROUP_W, HEAD_DIM)
    scale = HEAD_DIM ** -0.5 * LOG2E

    def prep(x_ref, rows, dst, rope_rows):
        q = _group_rmsnorm(x_ref[rows, 0:GROUP_W].astype(F32), gq_ref[...], ones_g, HEAD_DIM)
        k = _group_rmsnorm(x_ref[rows, GROUP_W:2 * GROUP_W].astype(F32), gk_ref[...], ones_g, HEAD_DIM)
        if rope_rows is not None:
            cos_t, sin_t = cos_ref[rope_rows, :], sin_ref[rope_rows, :]
            q, k = _rope(q, cos_t, sin_t, HEAD_DIM // 2), _rope(k, cos_t, sin_t, HEAD_DIM // 2)
        q_s[dst, :] = (q * scale).astype(BF16)
        k_s[:, dst] = jnp.transpose(k).astype(BF16)
        va_s[dst, :], vb_s[dst, :] = _ones_lane_variants(x_ref[rows, 2 * GROUP_W:3 * GROUP_W])

    prep(xc_ref, pl.ds(0, ctx), pl.ds(0, ctx), None)

    def prep_body(i, carry):
        r = pl.ds(pl.multiple_of(i * tq, tq), tq)
        prep(xl_ref, r, pl.ds(pl.multiple_of(ctx + i * tq, tq), tq), r)
        return carry

    lax.fori_loop(0, seq // tq, prep_body, 0)

    stages = [(_head_mask((1, GROUP_W), h, HEAD_DIM, BF16), *_head_v(h, va_s, vb_s),
               _head_mask((1, GROUP_W), h, HEAD_DIM, F32)) for h in range(N_HEADS)]
    finish = lambda acc: acc
    yc_ref[...] = _context_attention(q_s[0:ctx, :], k_s[:, 0:ctx], stages, finish).astype(yc_ref.dtype)
    _attention_stages(q_s, k_s, yl_ref, p_s, acc_s, stages, finish, ctx=ctx, seq=seq, tq=tq, unroll=unroll)


def _gqa_mixer(p_gqa, cos_t, sin_t, gq, gk, *, n_batch, seq, ctx, tq=ATTN_TQ, unroll=ATTN_UNROLL, lag=ATTN_LAG):
    lat_blocks = n_batch * seq // ctx
    kern = functools.partial(_gqa_kernel, seq=seq, ctx=ctx, tq=tq, unroll=unroll)
    yl, yc = pl.pallas_call(
        kern,
        grid=(n_batch,),
        in_specs=[pl.BlockSpec((seq, IN_GQA), lambda b: (b, 0)),
                  pl.BlockSpec((ctx, IN_GQA), lambda b: (lat_blocks + b, 0)),
                  _resident(cos_t.shape), _resident(sin_t.shape),
                  _resident((1, GROUP_W)), _resident((1, GROUP_W))],
        out_specs=[pl.BlockSpec((seq, GROUP_W), lambda b: (b, 0)),
                   pl.BlockSpec((ctx, GROUP_W), lambda b: (b, 0))],
        out_shape=[jax.ShapeDtypeStruct((n_batch * seq, GROUP_W), BF16),
                   jax.ShapeDtypeStruct((n_batch * ctx, GROUP_W), BF16)],
        scratch_shapes=[pltpu.VMEM((ctx + seq, GROUP_W), BF16), pltpu.VMEM((GROUP_W, ctx + seq), BF16)]
                       + [pltpu.VMEM((ctx + seq, GROUP_W), BF16)] * 2
                       + [pltpu.VMEM((lag, tq, ctx + seq), BF16), pltpu.VMEM((tq, GROUP_W), F32)],
        compiler_params=_cparams("parallel"),
        name="gqa_mixer",
    )(p_gqa, p_gqa, cos_t, sin_t, gq, gk)
    return yl, yc


def _diff_kernel(xl_ref, xc_ref, cos_ref, sin_ref, gq_ref, gk_ref, lam_ref, sub_ref, yl_ref, yc_ref,
                 q_s, k_s, va_s, vb_s, p_s, acc_s, *, seq, ctx, tq, unroll, lam_init):
    ones_sub = _group_ones(GROUP_W, DIFF_SUB)
    ones_head = _group_ones(GROUP_W, HEAD_DIM)
    scale = DIFF_SUB ** -0.5 * LOG2E
    lp = lam_ref[...]
    lam = (jnp.exp(jnp.sum(lp[0:1] * lp[1:2], axis=-1, keepdims=True))
           - jnp.exp(jnp.sum(lp[2:3] * lp[3:4], axis=-1, keepdims=True)) + lam_init)

    def prep(x_ref, rows, dst, rope_rows):
        q = _group_rmsnorm(x_ref[rows, 0:GROUP_W].astype(F32), gq_ref[...], ones_sub, DIFF_SUB)
        k = _group_rmsnorm(x_ref[rows, GROUP_W:2 * GROUP_W].astype(F32), gk_ref[...], ones_sub, DIFF_SUB)
        if rope_rows is not None:
            cos_t, sin_t = cos_ref[rope_rows, :], sin_ref[rope_rows, :]
            q, k = _rope(q, cos_t, sin_t, DIFF_SUB // 2), _rope(k, cos_t, sin_t, DIFF_SUB // 2)
        q_s[dst, :] = (q * scale).astype(BF16)
        k_s[:, dst] = jnp.transpose(k).astype(BF16)
        va_s[dst, :], vb_s[dst, :] = _ones_lane_variants(x_ref[rows, 2 * GROUP_W:3 * GROUP_W])

    prep(xc_ref, pl.ds(0, ctx), pl.ds(0, ctx), None)

    def prep_body(i, carry):
        r = pl.ds(pl.multiple_of(i * tq, tq), tq)
        prep(xl_ref, r, pl.ds(pl.multiple_of(ctx + i * tq, tq), tq), r)
        return carry

    lax.fori_loop(0, seq // tq, prep_body, 0)

    stages = []
    for h in range(N_HEADS):
        head = _head_mask((1, GROUP_W), h, HEAD_DIM, F32)
        for j, weight in enumerate((head, -lam * head)):
            stages.append((_head_mask((1, GROUP_W), 2 * h + j, DIFF_SUB, BF16), *_head_v(h, va_s, vb_s), weight))
    finish = lambda acc: _group_rmsnorm(acc, sub_ref[...], ones_head, HEAD_DIM) * (1.0 - lam_init)
    yc_ref[...] = _context_attention(q_s[0:ctx, :], k_s[:, 0:ctx], stages, finish).astype(yc_ref.dtype)
    _attention_stages(q_s, k_s, yl_ref, p_s, acc_s, stages, finish, ctx=ctx, seq=seq, tq=tq, unroll=unroll)


def _diff_mixer(p_df, cos_t, sin_t, gq, gk, lam_p, sub_g, *, lam_init, n_batch, seq, ctx,
                tq=ATTN_TQ, unroll=ATTN_UNROLL, lag=ATTN_LAG):
    lat_blocks = n_batch * seq // ctx
    kern = functools.partial(_diff_kernel, seq=seq, ctx=ctx, tq=tq, unroll=unroll, lam_init=lam_init)
    yl, yc = pl.pallas_call(
        kern,
        grid=(n_batch,),
        in_specs=[pl.BlockSpec((seq, IN_DIFF), lambda b: (b, 0)),
                  pl.BlockSpec((ctx, IN_DIFF), lambda b: (lat_blocks + b, 0)),
                  _resident(cos_t.shape), _resident(sin_t.shape),
                  _resident((1, GROUP_W)), _resident((1, GROUP_W)),
                  _resident(lam_p.shape), _resident((1, GROUP_W))],
        out_specs=[pl.BlockSpec((seq, GROUP_W), lambda b: (b, 0)),
                   pl.BlockSpec((ctx, GROUP_W), lambda b: (b, 0))],
        out_shape=[jax.ShapeDtypeStruct((n_batch * seq, GROUP_W), BF16),
                   jax.ShapeDtypeStruct((n_batch * ctx, GROUP_W), BF16)],
        scratch_shapes=[pltpu.VMEM((ctx + seq, GROUP_W), BF16), pltpu.VMEM((GROUP_W, ctx + seq), BF16)]
                       + [pltpu.VMEM((ctx + seq, GROUP_W), BF16)] * 2
                       + [pltpu.VMEM((lag, tq, ctx + seq), BF16), pltpu.VMEM((tq, GROUP_W), F32)],
        compiler_params=_cparams("parallel"),
        name="diff_attention_mixer",
    )(p_df, p_df, cos_t, sin_t, gq, gk, lam_p, sub_g)
    return yl, yc


def _dot_split2(x, w_bf16):
    hi = x.astype(BF16)
    lo = (x - hi.astype(F32)).astype(BF16)
    return _dot(hi, w_bf16) + _dot(lo, w_bf16)


def _mlstm_kernel(xl_ref, xc_ref, gl_ref, gc_ref, cw_ref, cb_ref, gb_ref, ng_ref, yl_ref, yc_ref,
                  q_s, k_s, hsum, c_s, n_s, m_s, *, seq, ctx, unroll):
    t = ML_CHUNK
    n_chunks = seq // t
    ones_head = _group_ones(GROUP_W, HEAD_DIM)
    block_diag = ones_head.astype(F32)
    r_i = lax.broadcasted_iota(jnp.int32, (t, t), 0)
    c_i = lax.broadcasted_iota(jnp.int32, (t, t), 1)
    tril_b = jnp.where(c_i <= r_i, 1.0, 0.0).astype(BF16)
    causal = (c_i <= r_i, c_i >= r_i)
    row_g = lax.broadcasted_iota(jnp.int32, (t, IN_GATE), 0)
    sel_r = lax.broadcasted_iota(jnp.int32, (IN_GATE, GROUP_W), 0)
    sel_c = lax.broadcasted_iota(jnp.int32, (IN_GATE, GROUP_W), 1) // HEAD_DIM
    select = [jnp.where(sel_r == 8 * d + sel_c, 1.0, 0.0).astype(BF16) for d in range(2)]
    q_masks = [_head_mask((1, GROUP_W), h, HEAD_DIM, BF16) for h in range(N_HEADS)]

    for x_ref, rows, off in ((xc_ref, ctx, 0), (xl_ref, seq, ctx)):
        qk = _silu(_dwconv3(x_ref[:, 0:2 * GROUP_W].astype(F32), cw_ref[...], cb_ref[...]))
        q_s[off:off + rows, :] = (qk[:, 0:GROUP_W] * (HEAD_DIM ** -0.5)).astype(BF16)
        k_s[off:off + rows, :] = qk[:, GROUP_W:2 * GROUP_W].astype(BF16)

    hsum[...] = jnp.zeros_like(hsum)
    c_s[...] = jnp.zeros_like(c_s)
    n_s[...] = jnp.zeros_like(n_s)
    m_s[...] = jnp.zeros_like(m_s)

    def running_max(x, d):
        step = 1
        while step < t:
            if d == 0:
                shifted, valid = pltpu.roll(x, step, 0), row_g >= step
            else:
                shifted, valid = pltpu.roll(x, t - step, 0), row_g < t - step
            x = jnp.maximum(x, jnp.where(valid, shifted, -jnp.inf))
            step *= 2
        return x

    def chunk(d, rows, v, g_raw):
        g = g_raw + gb_ref[...]
        logf = pltpu.roll(jax.nn.log_sigmoid(g), IN_GATE - N_HEADS, 1)
        pre = _dot_exact_rhs(tril_b, logf)
        tot = pre[t - 1:t, :]
        cum = pre if d == 0 else tot - pre + logf
        r = g - cum
        m_prev = m_s[d]
        log_inter = cum + m_prev
        m_t = jnp.maximum(log_inter, cum + running_max(r, d))
        a_t = cum - m_t
        log_g = (tot - cum) + g
        m_new = jnp.maximum(tot + m_prev, jnp.max(log_g, axis=0, keepdims=True))
        m_s[d] = m_new
        rowwise = jnp.concatenate([jnp.exp(log_inter - m_t), jnp.exp(-m_t), jnp.exp(log_g - m_new),
                                   jnp.broadcast_to(jnp.exp(tot + m_prev - m_new), (8, IN_GATE))], axis=0)
        full = _dot_split2(rowwise, select[d])
        w_inter, e_m, w_g, w_c = full[0:t], full[t:2 * t], full[2 * t:3 * t], full[3 * t:3 * t + 1]

        q, k = q_s[rows, :], k_s[rows, :]
        r_t = jnp.transpose(r)
        pair = [None, None]
        for h in range(N_HEADS):
            col = 8 * d + h
            w_intra = jnp.exp(jnp.where(causal[d], a_t[:, col:col + 1] + r_t[col:col + 1, :], -jnp.inf))
            s = (_dot_nt(q * q_masks[h], k) * w_intra).astype(BF16)
            o = _dot(s, v * q_masks[h] + q_masks[(h + 2) % N_HEADS])
            pair[h // 2] = o if pair[h // 2] is None else pair[h // 2] + o
        half = GROUP_W // 2
        num_i = jnp.concatenate([pair[0][:, :half], pair[1][:, half:]], axis=1)
        den_i = jnp.concatenate([pair[0][:, half:], pair[1][:, :half]], axis=1)
        ct, n_full = c_s[d], n_s[d]
        inter = _dot(q, ct.astype(BF16))
        qn = _dot((q.astype(F32) * n_full).astype(BF16), ones_head)
        num = w_inter * inter + num_i
        den = w_inter * qn + den_i
        kw = k.astype(F32) * w_g
        c_s[d] = w_c * ct + _dot_tn(kw.astype(BF16), v) * block_diag
        n_s[d] = w_c * n_full + jnp.sum(kw, axis=0, keepdims=True)
        return num / jnp.maximum(jnp.abs(den), e_m)

    assert ctx == t
    v_c = xc_ref[:, 2 * GROUP_W:3 * GROUP_W]
    hsum[0:t, :] = chunk(0, pl.ds(0, t), v_c, gc_ref[...]) + chunk(1, pl.ds(0, t), v_c, gc_ref[...])

    def body(j, carry):
        for d in range(2):
            cidx = j if d == 0 else n_chunks - 1 - j
            lat = pl.ds(pl.multiple_of(cidx * t, t), t)
            rows = pl.ds(pl.multiple_of(ctx + cidx * t, t), t)
            hsum[rows, :] += chunk(d, rows, xl_ref[lat, 2 * GROUP_W:3 * GROUP_W], gl_ref[lat, :])
        return carry

    lax.fori_loop(0, n_chunks, body, 0, unroll=unroll)

    for x_ref, y_ref, rows, off in ((xc_ref, yc_ref, ctx, 0), (xl_ref, yl_ref, seq, ctx)):
        hn = _group_rmsnorm(hsum[off:off + rows, :], ng_ref[...], ones_head, HEAD_DIM)
        y_ref[...] = (hn * jax.nn.sigmoid(x_ref[:, 3 * GROUP_W:4 * GROUP_W].astype(F32))).astype(y_ref.dtype)


def _mlstm_mixer(p_ml, p_gate, conv_w, conv_b, gate_b, norm_g, *, n_batch, seq, ctx, unroll=1):
    lat_blocks = n_batch * seq // ctx
    kern = functools.partial(_mlstm_kernel, seq=seq, ctx=ctx, unroll=unroll)
    yl, yc = pl.pallas_call(
        kern,
        grid=(n_batch,),
        in_specs=[pl.BlockSpec((seq, IN_ML), lambda b: (b, 0)),
                  pl.BlockSpec((ctx, IN_ML), lambda b: (lat_blocks + b, 0)),
                  pl.BlockSpec((seq, IN_GATE), lambda b: (b, 0)),
                  pl.BlockSpec((ctx, IN_GATE), lambda b: (lat_blocks + b, 0)),
                  _resident(conv_w.shape), _resident(conv_b.shape), _resident(gate_b.shape), _resident(norm_g.shape)],
        out_specs=[pl.BlockSpec((seq, GROUP_W), lambda b: (b, 0)),
                   pl.BlockSpec((ctx, GROUP_W), lambda b: (b, 0))],
        out_shape=[jax.ShapeDtypeStruct((n_batch * seq, GROUP_W), BF16),
                   jax.ShapeDtypeStruct((n_batch * ctx, GROUP_W), BF16)],
        scratch_shapes=[pltpu.VMEM((ctx + seq, GROUP_W), BF16), pltpu.VMEM((ctx + seq, GROUP_W), BF16),
                        pltpu.VMEM((ctx + seq, GROUP_W), F32),
                        pltpu.VMEM((2, GROUP_W, GROUP_W), F32), pltpu.VMEM((2, 1, GROUP_W), F32),
                        pltpu.VMEM((2, 1, IN_GATE), F32)],
        compiler_params=_cparams("parallel"),
        name="mlstm_mixer",
    )(p_ml, p_ml, p_gate, p_gate, conv_w, conv_b, gate_b, norm_g)
    return yl, yc


def _phase_tables(idx_a, idx_b, n):
    ang = (2.0 * math.pi / n) * ((idx_a[:, None] * idx_b[None, :]) % n).astype(F32)
    return jnp.cos(ang), jnp.sin(ang)


def _dft_matrix(length, n):
    nf, lo = n // 2, 64
    k = jnp.arange(nf, dtype=jnp.int32)
    k1, k0 = lo * jnp.arange(nf // lo, dtype=jnp.int32), jnp.arange(lo, dtype=jnp.int32)
    j = jnp.arange(length, dtype=jnp.int32)
    (ca, sa), (cb, sb) = _phase_tables(k1, j, n), _phase_tables(k0, j, n)
    cos_f = (ca[:, None, :] * cb[None, :, :] - sa[:, None, :] * sb[None, :, :]).reshape(nf, length)
    sin_f = (sa[:, None, :] * cb[None, :, :] + ca[:, None, :] * sb[None, :, :]).reshape(nf, length)
    alt = jnp.where(j % 2 == 0, 1.0, -1.0).astype(F32)
    return jnp.concatenate([cos_f, jnp.where(k[:, None] == 0, alt[None, :], sin_f)], axis=0).astype(BF16)


def _hyena_filter_consts(length, n):
    t = jnp.arange(length, dtype=F32)
    tn = t / length
    bands = jnp.arange(1, HY_POS_BANDS + 1, dtype=F32)
    ang = 2.0 * math.pi * tn[:, None] * bands
    feats = jnp.concatenate([tn[:, None], jnp.cos(ang), jnp.sin(ang)], axis=-1)
    feats = jnp.pad(feats, ((0, 0), (0, 128 - feats.shape[1])))
    dist = jnp.abs(t - length // 2) / (length / 2)
    deltas = jnp.abs(jnp.linspace(math.log(HY_DECAY_TARGET) / HY_SLOW_DECAY,
                                  math.log(HY_DECAY_TARGET) / HY_FAST_DECAY, GROUP_W, dtype=F32))
    window = jnp.exp(-dist[:, None] * jnp.tile(deltas, 2))
    k = jnp.arange(n // 2, dtype=jnp.int32)
    phi = (2.0 * math.pi / n) * ((k * (length // 2)) % n).astype(F32)
    shift = jnp.stack([jnp.cos(phi), jnp.sin(phi)], axis=-1) * (2.0 / n)
    return feats, window, shift


def _filter_kernel(feats_ref, win_ref, shift_ref, w1lo_ref, w1hi_ref, b1_ref, w2_ref, b2_ref, w3t_ref, w3b_ref, b3_ref,
                   f_ref, al_ref, be_ref, ga_ref, hid_s, *, nyquist_sign):
    half = feats_ref.shape[0] // 2

    @pl.when(pl.program_id(1) == 0)
    def _():
        pre = (_dot_f32ish(feats_ref[0:half, :], w1lo_ref[0]) + _dot_f32ish(feats_ref[half:2 * half, :], w1hi_ref[0]))
        h = jnp.sin(pre + b1_ref[0])
        hid_s[...] = jnp.sin(_dot_f32ish(h, w2_ref[0]) + b2_ref[0])

    hid = hid_s[...]
    h = jnp.concatenate([_dot_f32ish(hid, w3t_ref[0]), _dot_f32ish(hid, w3b_ref[0])], axis=0)
    h = (h + b3_ref[0]) * win_ref[...]
    h = h / jnp.sum(jnp.abs(h), axis=0, keepdims=True)
    h1 = h.astype(BF16)
    h2 = (h - h1.astype(F32)).astype(BF16)
    nf = f_ref.shape[0] // 2
    step = math.gcd(nf, 512)
    for r in range(0, nf, step):
        fc, fs = f_ref[r:r + step, :], f_ref[nf + r:nf + r + step, :]
        p = _dot(fc, h1) + _dot(fc, h2)
        q = _dot(fs, h1) + _dot(fs, h2)
        cs, sn = shift_ref[r:r + step, 0:1], shift_ref[r:r + step, 1:2]
        alpha = p * cs + q * sn
        beta = p * sn - q * cs
        if r == 0:
            first = lax.broadcasted_iota(jnp.int32, p.shape, 0) == 0
            al_ref[0, r:r + step, :] = jnp.where(first, 0.5 * alpha, alpha)
            be_ref[0, r:r + step, :] = jnp.where(first, 0.0, beta)
            ga_ref[0, r:r + step, :] = jnp.where(first, (0.5 * nyquist_sign) * q * cs, alpha)
        else:
            al_ref[0, r:r + step, :] = alpha
            be_ref[0, r:r + step, :] = beta
            ga_ref[0, r:r + step, :] = alpha


def _hyena_filter_spectra(length, fwd, w1, b1, w2, b2, w3, b3):
    depth, hid = w1.shape[0], w1.shape[2]
    n = fwd.shape[0]
    feats, window, shift = _hyena_filter_consts(length, n)
    nf = n // 2
    place = lambda a, r0, c0, rows, cols: jnp.pad(a, ((0, 0), (r0, rows - r0 - a.shape[1]), (c0, cols - c0 - a.shape[2])))
    w1lo, w1hi = place(w1, 0, 0, 128, 128), place(w1, 0, hid, 128, 128)
    w2bd = place(w2, 0, 0, 128, 128) + place(w2, hid, hid, 128, 128)
    w3t, w3b = place(w3, 0, 0, 128, w3.shape[2]), place(w3, hid, 0, 128, w3.shape[2])
    b1p, b2p = jnp.tile(b1, (1, 2))[:, None, :], jnp.tile(b2, (1, 2))[:, None, :]
    b3p = b3[:, None, :]
    lay = lambda a: pl.BlockSpec((1,) + a.shape[1:], lambda l, o: (l, 0, 0))
    per_order = lambda rows: pl.BlockSpec((1, rows, GROUP_W), lambda l, o: (l, 0, o))
    out = jax.ShapeDtypeStruct((depth, nf, 2 * GROUP_W), F32)
    kern = functools.partial(_filter_kernel, nyquist_sign=1.0 if (length // 2) % 2 == 0 else -1.0)
    return pl.pallas_call(
        kern,
        grid=(depth, 2),
        in_specs=[_resident(feats.shape), pl.BlockSpec((length, GROUP_W), lambda l, o: (0, o)), _resident(shift.shape),
                  lay(w1lo), lay(w1hi), lay(b1p), lay(w2bd), lay(b2p), per_order(128), per_order(128), per_order(1),
                  _resident(fwd.shape)],
        out_specs=[per_order(nf)] * 3,
        out_shape=[out, out, out],
        scratch_shapes=[pltpu.VMEM((length // 2, 128), F32)],
        compiler_params=_cparams("parallel", "arbitrary"),
        name="hyena_filter_spectra",
    )(feats, window, shift, w1lo, w1hi, b1p, w2bd, b2p, w3t, w3b, b3p, fwd)


def _hyena_conv_kernel(sig_ref, gate_ref, cws_ref, cbs_ref, cwg_ref, cbg_ref, skip_ref, f_ref,
                       al_ref, be_ref, ga_ref, o_ref, *, sig_conv):
    s = sig_ref[...].astype(F32)
    if sig_conv:
        s = _dwconv3(s, cws_ref[...], cbs_ref[...])
    gate = _dwconv3(gate_ref[...].astype(F32), cwg_ref[...], cbg_ref[...])
    sb = s.astype(BF16)
    nf = f_ref.shape[0] // 2
    a = _dot(f_ref[0:nf, :], sb)
    b = _dot(f_ref[nf:2 * nf, :], sb)
    ay = a * al_ref[0] + b * be_ref[0]
    by = b * ga_ref[0] - a * be_ref[0]
    y = _dot_tn(f_ref[0:nf, :], ay.astype(BF16)) + _dot_tn(f_ref[nf:2 * nf, :], by.astype(BF16))
    o_ref[...] = (gate * (y + skip_ref[...] * s)).astype(o_ref.dtype)


def _hyena_conv(sig, sig_block, sig_col, p_hy, gate_col, conv_w, conv_b, skip, fwd, spectra, order, layer,
                *, rows, first_block, n_batch, out_dtype, sig_conv):
    cw = lambda col: pl.BlockSpec((3, GROUP_W), lambda b: (0, col))
    cb = lambda col: pl.BlockSpec((1, GROUP_W), lambda b: (0, col))
    spec = pl.BlockSpec((1, fwd.shape[0] // 2, GROUP_W), lambda b: (layer, 0, order), pipeline_mode=pl.Buffered(1))
    kern = functools.partial(_hyena_conv_kernel, sig_conv=sig_conv)
    return pl.pallas_call(
        kern,
        grid=(n_batch,),
        in_specs=[pl.BlockSpec((rows, GROUP_W), lambda b: (sig_block + b, sig_col)),
                  pl.BlockSpec((rows, GROUP_W), lambda b: (first_block + b, gate_col)),
                  cw(sig_col), cb(sig_col), cw(gate_col), cb(gate_col),
                  pl.BlockSpec((1, GROUP_W), lambda b: (0, 0)),
                  _resident(fwd.shape), spec, spec, spec],
        out_specs=pl.BlockSpec((rows, GROUP_W), lambda b: (b, 0)),
        out_shape=jax.ShapeDtypeStruct((n_batch * rows, GROUP_W), out_dtype),
        compiler_params=_cparams("parallel"),
        name="hyena_long_conv",
    )(sig, p_hy, conv_w, conv_b, conv_w, conv_b, skip, fwd, *spectra)


def _hyena_mixer(p_hy, conv_w, conv_b, skip, fwd, spectra, layer, *, rows, first_block, n_batch):
    common = dict(rows=rows, first_block=first_block, n_batch=n_batch)
    z = _hyena_conv(p_hy, first_block, 0, p_hy, 1, conv_w, conv_b, skip[0:1], fwd, spectra, 0, layer,
                    out_dtype=F32, sig_conv=True, **common)
    return _hyena_conv(z, 0, 0, p_hy, 2, conv_w, conv_b, skip[1:2], fwd, spectra, 1, layer,
                       out_dtype=BF16, sig_conv=False, **common)


def _rope_tables(length, dim, reps):
    rows = length // GRID_W
    row = jnp.repeat(jnp.arange(rows), GRID_W).astype(F32)
    col = jnp.tile(jnp.arange(GRID_W), rows).astype(F32)
    n_freq = dim // 4
    inv = ROPE_THETA ** (-jnp.arange(n_freq, dtype=F32) / n_freq)
    ang = jnp.concatenate([row[:, None] * inv, col[:, None] * inv], axis=-1)
    cos, sin = jnp.cos(ang), jnp.sin(ang)
    return (jnp.tile(jnp.concatenate([cos, cos], axis=-1), (1, reps)),
            jnp.tile(jnp.concatenate([-sin, sin], axis=-1), (1, reps)))


def kernel(x, c, ctx, c_ctx, ada_w, ada_b, norm_g, ffn_w13, ffn_w2, w_in, w_out, ml_gate_b, ml_conv_w, ml_conv_b,
           ml_norm_g, gqa_qk_g, hy_conv_w, hy_conv_b, hy_filt_w1, hy_filt_b1, hy_filt_w2, hy_filt_b2, hy_filt_w3,
           hy_filt_b3, hy_skip, diff_qk_g, diff_lambda, diff_subln_g):
    n_batch, seq, d = x.shape
    n_ctx = ctx.shape[1]
    depth = ada_w.shape[0]
    n_lat = n_batch * seq
    n_all = n_lat + n_batch * n_ctx
    assert n_batch * n_ctx == seq and n_ctx == ML_CHUNK and seq % ROW_TILE == 0

    c_all = jnp.pad(jnp.concatenate([c, c_ctx[None]], axis=0), ((0, 16 - n_batch - 1), (0, 0)))
    mod = _modulation(c_all, ada_w, ada_b)[:, :n_batch + 1].reshape(depth, n_batch + 1, N_MOD, d)

    w13 = ffn_w13.astype(BF16)
    w2 = ffn_w2.astype(BF16)
    w_out_b = w_out.astype(BF16)

    rope_gqa = _rope_tables(seq, HEAD_DIM, N_HEADS)
    rope_diff = _rope_tables(seq, DIFF_SUB, 2 * N_HEADS)
    dft_l = _dft_matrix(seq, 3 * seq // 2)
    dft_c = _dft_matrix(n_ctx, 2 * n_ctx)
    filt = (hy_filt_w1, hy_filt_b1, hy_filt_w2, hy_filt_b2, hy_filt_w3, hy_filt_b3)
    spec_l = _hyena_filter_spectra(seq, dft_l, *filt)
    spec_c = _hyena_filter_spectra(n_ctx, dft_c, *filt)

    h = (x.reshape(n_lat, d), ctx.reshape(n_batch * n_ctx, d))
    dims = dict(seq=seq, n_batch=n_batch)
    gqa_cfg = dict(tq=512, unroll=4, lag=2)
    gqa_lag = [2, 2, 1, 3]
    diff_cfg = dict(tq=256, unroll=2, lag=2)
    ffn_cfg = dict(tm=1024, chunk=256)
    for l in range(depth):
        need_ctx = l < depth - 1
        lam_init = 0.8 - 0.6 * math.exp(-0.3 * l)
        h = _ffn_half(h, mod[l], norm_g[l, 0], w13, w2, layer=l, half=0, base=0, n_rows=n_all, **ffn_cfg, **dims)
        p_ml, p_gqa, p_hy, p_df, p_gate = _in_projection(h, mod[l], norm_g[l, 1], w_in, layer=l, tm=1024, **dims)

        gate_b = jnp.pad(ml_gate_b[l], (0, IN_GATE - ml_gate_b.shape[1]))[None]
        y_ml = _mlstm_mixer(p_ml, p_gate, ml_conv_w[l], ml_conv_b[l][None], gate_b, ml_norm_g[l][None],
                            n_batch=n_batch, seq=seq, ctx=n_ctx, unroll=8)
        tile_g = lambda g, reps: jnp.tile(g, reps)[None]
        y_gqa = _gqa_mixer(p_gqa, *rope_gqa, tile_g(gqa_qk_g[l, 0], N_HEADS), tile_g(gqa_qk_g[l, 1], N_HEADS),
                           n_batch=n_batch, seq=seq, ctx=n_ctx, **{**gqa_cfg, 'lag': gqa_lag[l]})
        y_df = _diff_mixer(p_df, *rope_diff, tile_g(diff_qk_g[l, 0], 2 * N_HEADS), tile_g(diff_qk_g[l, 1], 2 * N_HEADS),
                           diff_lambda[l], tile_g(diff_subln_g[l], N_HEADS),
                           lam_init=lam_init, n_batch=n_batch, seq=seq, ctx=n_ctx, **diff_cfg)
        y_hy = _hyena_mixer(p_hy, hy_conv_w[l], hy_conv_b[l][None], hy_skip[l], dft_l, spec_l, l,
                            rows=seq, first_block=0, n_batch=n_batch)
        y_hy_c = y_hy if not need_ctx else _hyena_mixer(
            p_hy, hy_conv_w[l], hy_conv_b[l][None], hy_skip[l], dft_c, spec_c, l,
            rows=n_ctx, first_block=n_lat // n_ctx, n_batch=n_batch)
        h = _ffn_half(h, mod[l], norm_g[l, 2], w13, w2, layer=l, half=1, base=6,
                      n_rows=n_all if need_ctx else n_lat,
                      mix=[y_ml, y_gqa, (y_hy, y_hy_c), y_df], w_out=w_out_b, **ffn_cfg, **dims)
    return h[:n_lat].reshape(n_batch, seq, d)
```

```python
import functools
import math

import jax
import jax.numpy as jnp
from jax import lax
from jax.experimental import pallas as pl
from jax.experimental.pallas import tpu as pltpu

F32 = jnp.float32
BF16 = jnp.bfloat16

EPS = 1e-6
ROPE_THETA = 10000.0
GRID_W = 64
N_MOD = 9
GROUP_W = 256
HEAD_DIM = 64
N_HEADS = 4
DIFF_SUB = 32
ML_CHUNK = 256
HY_POS_BANDS = 16
HY_DECAY_TARGET = 1e-2
HY_FAST_DECAY = 0.3
HY_SLOW_DECAY = 1.5
ROW_TILE = 512
VMEM_LIMIT = 56 * 1024 * 1024


def _cparams(*sem):
    return pltpu.CompilerParams(dimension_semantics=sem, vmem_limit_bytes=VMEM_LIMIT)


def _resident(shape):
    nd = len(shape)
    return pl.BlockSpec(shape, lambda *_: (0,) * nd, pipeline_mode=pl.Buffered(1))


def _resident_slice(arr, index):
    tail = arr.shape[len(index):]
    return pl.BlockSpec((None,) * len(index) + tail, lambda *_: tuple(index) + (0,) * len(tail),
                        pipeline_mode=pl.Buffered(1))


def _dot(a, b):
    return jnp.dot(a, b, preferred_element_type=F32)


def _dot_nt(a, b):
    return lax.dot_general(a, b, (((1,), (1,)), ((), ())), preferred_element_type=F32)


def _dot_tn(a, b):
    return lax.dot_general(a, b, (((0,), (0,)), ((), ())), preferred_element_type=F32)


def _split3(x):
    x1 = x.astype(BF16)
    r = x - x1.astype(F32)
    x2 = r.astype(BF16)
    x3 = (r - x2.astype(F32)).astype(BF16)
    return x1, x2, x3


def _dot_exact_rhs(a_bf16, x):
    x1, x2, x3 = _split3(x)
    return _dot(a_bf16, x1) + _dot(a_bf16, x2) + _dot(a_bf16, x3)


def _dot_f32ish(a, b):
    a1 = a.astype(BF16)
    a2 = (a - a1.astype(F32)).astype(BF16)
    b1 = b.astype(BF16)
    b2 = (b - b1.astype(F32)).astype(BF16)
    return _dot(a1, b1) + _dot(a1, b2) + _dot(a2, b1)


def _silu(x):
    return x * jax.nn.sigmoid(x)


def _modnorm(x, g, shift, scale):
    y = x * lax.rsqrt(jnp.mean(x * x, axis=-1, keepdims=True) + EPS)
    return (y * g) * (1.0 + scale) + shift


def _lane_iota(shape):
    return lax.broadcasted_iota(jnp.int32, shape, len(shape) - 1)


def _group_ones(n, group):
    r = lax.broadcasted_iota(jnp.int32, (n, n), 0) // group
    c = lax.broadcasted_iota(jnp.int32, (n, n), 1) // group
    return jnp.where(r == c, 1.0, 0.0).astype(BF16)


def _group_rmsnorm(x, gain, ones_g, group):
    x2 = x * x
    hi = x2.astype(BF16)
    lo = (x2 - hi.astype(F32)).astype(BF16)
    ms = (_dot(hi, ones_g) + _dot(lo, ones_g)) * (1.0 / group)
    return x * lax.rsqrt(ms + EPS) * gain


def _rope(x, cos_t, sin_t, half):
    n = x.shape[-1]
    left = pltpu.roll(x, n - half, 1)
    right = pltpu.roll(x, half, 1)
    sw = jnp.where((_lane_iota(x.shape) & (2 * half - 1)) < half, left, right)
    return x * cos_t + sw * sin_t


def _dwconv3(x, w, b):
    rows = x.shape[0]
    row = lax.broadcasted_iota(jnp.int32, x.shape, 0)
    xm = jnp.where(row == 0, 0.0, pltpu.roll(x, 1, 0))
    xp = jnp.where(row == rows - 1, 0.0, pltpu.roll(x, rows - 1, 0))
    return xm * w[0:1] + x * w[1:2] + xp * w[2:3] + b


def _head_mask(shape, h, width, dtype):
    lane = _lane_iota(shape)
    return jnp.where((lane >= h * width) & (lane < (h + 1) * width), 1.0, 0.0).astype(dtype)


def _mod_kernel(c_ref, w_ref, b_ref, o_ref):
    sc = _silu(c_ref[...]).astype(BF16)
    o_ref[0] = _dot(sc, w_ref[0].astype(BF16)) + b_ref[0]


def _modulation(c_all, ada_w, ada_b):
    depth, d, nmod = ada_w.shape
    rows = c_all.shape[0]
    tn = 1536
    return pl.pallas_call(
        _mod_kernel,
        grid=(depth, nmod // tn),
        in_specs=[pl.BlockSpec((rows, d), lambda l, j: (0, 0)),
                  pl.BlockSpec((1, d, tn), lambda l, j: (l, 0, j)),
                  pl.BlockSpec((1, 1, tn), lambda l, j: (l, 0, j))],
        out_specs=pl.BlockSpec((1, rows, tn), lambda l, j: (l, 0, j)),
        out_shape=jax.ShapeDtypeStruct((depth, rows, nmod), F32),
        compiler_params=_cparams("parallel", "parallel"),
        name="adaln_modulation",
    )(c_all, ada_w, ada_b.reshape(depth, 1, nmod))


def _row_specs(width, tile, n_lat_tiles):
    return [pl.BlockSpec((tile, width), lambda i: (jnp.minimum(i, n_lat_tiles - 1), 0)),
            pl.BlockSpec((tile, width), lambda i: (jnp.maximum(i - n_lat_tiles, 0), 0))]


def _pick_rows(lat_ref, ctx_ref, n_lat_tiles):
    return jnp.where(pl.program_id(0) < n_lat_tiles, lat_ref[...], ctx_ref[...])


def _ffn_kernel(*refs, base, hidden, chunk, n_lat_tiles, split_x, n_mix):
    refs = list(refs)
    x = _pick_rows(refs.pop(0), refs.pop(0), n_lat_tiles) if split_x else refs.pop(0)[...]
    mod_ref, g_ref, w13_ref, w2_ref = refs[:4]
    mix_refs, (o_ref, acc_ref) = refs[4:-2], refs[-2:]
    if n_mix:
        wo_ref = mix_refs[-1]
        mixed = None
        for k in range(n_mix):
            y = _pick_rows(mix_refs[2 * k], mix_refs[2 * k + 1], n_lat_tiles)
            part = _dot(y, wo_ref[k * GROUP_W:(k + 1) * GROUP_W, :])
            mixed = part if mixed is None else mixed + part
        x = x + mod_ref[0, 5:6] * mixed
    shift, scale, gate = mod_ref[0, base:base + 1], mod_ref[0, base + 1:base + 2], mod_ref[0, base + 2:base + 3]
    xn = _modnorm(x, g_ref[...], shift, scale).astype(BF16)
    for start in range(0, hidden, chunk):
        size = min(chunk, hidden - start)
        a = _dot(xn, w13_ref[:, start:start + size])
        b = _dot(xn, w13_ref[:, hidden + start:hidden + start + size])
        part = _dot((_silu(a) * b).astype(BF16), w2_ref[start:start + size, :])
        if start == 0:
            acc_ref[...] = part
        else:
            acc_ref[...] += part
    o_ref[...] = x + (0.5 * gate) * acc_ref[...]


def _mod_index(tile, seq, n_batch):
    return lambda i: (jnp.minimum((i * tile) // seq, n_batch), 0, 0)


def _ffn_half(h, mod, g, w13, w2, *, layer, half, base, n_rows, seq, n_batch, mix=None, w_out=None,
              tm=ROW_TILE, chunk=512):
    split_x = isinstance(h, tuple)
    d = w13.shape[-2]
    hidden = w2.shape[-2]
    n_lat_tiles = n_batch * seq // tm
    kern = functools.partial(_ffn_kernel, base=base, hidden=hidden, chunk=chunk, n_lat_tiles=n_lat_tiles,
                             split_x=split_x, n_mix=len(mix) if mix else 0)
    x_specs = _row_specs(d, tm, n_lat_tiles) if split_x else [pl.BlockSpec((tm, d), lambda i: (i, 0))]
    mix_args, mix_specs = [], []
    for pair in mix or ():
        mix_args += list(pair)
        mix_specs += _row_specs(GROUP_W, tm, n_lat_tiles)
    if mix:
        mix_args.append(w_out)
        mix_specs.append(_resident_slice(w_out, (layer,)))
    return pl.pallas_call(
        kern,
        grid=(n_rows // tm,),
        in_specs=x_specs + [pl.BlockSpec((1, N_MOD, d), _mod_index(tm, seq, n_batch)),
                            _resident((1, d)), _resident_slice(w13, (layer, half)),
                            _resident_slice(w2, (layer, half))] + mix_specs,
        out_specs=pl.BlockSpec((tm, d), lambda i: (i, 0)),
        out_shape=jax.ShapeDtypeStruct((n_rows, d), F32),
        scratch_shapes=[pltpu.VMEM((tm, d), F32)],
        compiler_params=_cparams("parallel"),
        name="swiglu_half_step",
    )(*(h if split_x else (h,)), mod, g.reshape(1, d), w13, w2, *mix_args)


IN_ML, IN_GQA, IN_HY, IN_DIFF, IN_GATE = 1024, 768, 768, 768, 128


def _in_weight_pieces():
    ml, n_gate = 4 * GROUP_W, 16
    o_gqa = ml + n_gate
    o_hy = o_gqa + 2 * GROUP_W
    o_df = o_hy + 3 * GROUP_W
    pieces = [(0, 0, ml), (ml, o_gqa, GROUP_W)]
    for part, src in enumerate((o_gqa + GROUP_W, o_gqa + GROUP_W + 2 * HEAD_DIM)):
        for h in range(N_HEADS):
            pieces.append((ml + (part + 1) * GROUP_W + h * HEAD_DIM, src + (h // 2) * HEAD_DIM, HEAD_DIM))
    pieces += [(IN_ML + IN_GQA, o_hy, 3 * GROUP_W), (IN_ML + IN_GQA + IN_HY, o_df, 3 * GROUP_W),
               (IN_ML + IN_GQA + IN_HY + IN_DIFF, ml, n_gate)]
    return pieces


def _inproj_kernel(x_ref, mod_ref, g_ref, w_ref, ml_ref, gqa_ref, hy_ref, df_ref, gate_ref, wb_s):
    @pl.when(pl.program_id(0) == 0)
    def _():
        wb_s[:, IN_ML + IN_GQA + IN_HY + IN_DIFF:] = jnp.zeros((wb_s.shape[0], IN_GATE), BF16)
        for dst, src, width in _in_weight_pieces():
            wb_s[:, dst:dst + width] = w_ref[:, src:src + width].astype(BF16)

    xn = _modnorm(x_ref[...], g_ref[...], mod_ref[0, 3:4], mod_ref[0, 4:5]).astype(BF16)
    off = 0
    for ref, width in ((ml_ref, IN_ML), (gqa_ref, IN_GQA), (hy_ref, IN_HY), (df_ref, IN_DIFF), (gate_ref, IN_GATE)):
        ref[...] = _dot(xn, wb_s[:, off:off + width]).astype(ref.dtype)
        off += width


def _in_projection(h, mod, g, w, *, layer, seq, n_batch, tm=ROW_TILE):
    n_rows, d = h.shape
    widths = (IN_ML, IN_GQA, IN_HY, IN_DIFF, IN_GATE)
    dtypes = (BF16, BF16, BF16, BF16, F32)
    return pl.pallas_call(
        _inproj_kernel,
        grid=(n_rows // tm,),
        in_specs=[pl.BlockSpec((tm, d), lambda i: (i, 0)),
                  pl.BlockSpec((1, N_MOD, d), _mod_index(tm, seq, n_batch)),
                  _resident((1, d)), _resident_slice(w, (layer,))],
        out_specs=[pl.BlockSpec((tm, wd), lambda i: (i, 0)) for wd in widths],
        out_shape=[jax.ShapeDtypeStruct((n_rows, wd), dt) for wd, dt in zip(widths, dtypes)],
        scratch_shapes=[pltpu.VMEM((d, sum(widths)), BF16)],
        compiler_params=_cparams("arbitrary"),
        name="input_projection",
    )(h, mod, g.reshape(1, d), w)


LOG2E = 1.4426950408889634
ATTN_LAG = 2
ATTN_UNROLL = 1
ATTN_TQ = 512


def _ones_lane_variants(v):
    grp = _lane_iota((1, GROUP_W)) // HEAD_DIM
    odd = jnp.where(grp % 2 == 1, 1.0, 0.0).astype(BF16)
    even = jnp.where(grp % 2 == 0, 1.0, 0.0).astype(BF16)
    return v * even + odd, v * odd + even


def _scores_to_probs(qm, k_t):
    s = _dot(qm, k_t)
    return jnp.exp2(s - jnp.max(s, axis=-1, keepdims=True)).astype(BF16)


def _probs_times_v(p, v_aug, sum_lane):
    o = _dot(p, v_aug)
    return o * (1.0 / o[:, sum_lane:sum_lane + 1])


def _head_v(h, va, vb):
    return (va, (h + 1) * HEAD_DIM) if h % 2 == 0 else (vb, (h - 1) * HEAD_DIM)


def _attention_stages(q_s, k_s, y_ref, p_s, acc_s, stages, finish, *, ctx, seq, tq, unroll):
    n_keys = ctx + seq

    def probs(q, st):
        return _scores_to_probs(q * st[0], k_s[:, 0:n_keys])

    def out(p, st):
        return _probs_times_v(p, st[1][0:n_keys, :], st[2]) * st[3]

    p_s[...] = jnp.ones_like(p_s)
    acc_s[...] = jnp.zeros_like(acc_s)
    n, lag = len(stages), p_s.shape[0]

    def tail():
        acc = acc_s[...]
        for c in range(lag):
            acc = acc + out(p_s[c], stages[n - lag + c])
        return finish(acc)

    def body(i, carry):
        q = q_s[pl.ds(pl.multiple_of(ctx + i * tq, tq), tq), :]
        prev = jnp.maximum(i - 1, 0)
        y_ref[pl.ds(pl.multiple_of(prev * tq, tq), tq), :] = tail().astype(y_ref.dtype)
        ps, acc = [], None
        for j in range(n):
            ps.append(probs(q, stages[j]))
            if j >= lag:
                o = out(ps[j - lag], stages[j - lag])
                acc = o if acc is None else acc + o
        for c in range(lag):
            p_s[c] = ps[n - lag + c]
        acc_s[...] = acc
        return carry

    n_tiles = seq // tq
    lax.fori_loop(0, n_tiles, body, 0, unroll=unroll)
    y_ref[(n_tiles - 1) * tq:n_tiles * tq, :] = tail().astype(y_ref.dtype)


def _attention_stacked(q_s, k_s, y_ref, p_s, stages, finish, *, ctx, seq, tq, unroll):
    n_keys = ctx + seq
    groups = []
    for st_idx, st in enumerate(stages):
        for g in groups:
            if g[0] is st[1]:
                g[1].append(st_idx)
                break
        else:
            groups.append((st[1], [st_idx]))
    order = [j for _, js in groups for j in js]

    def probs(q):
        qm = jnp.concatenate([q * stages[j][0] for j in order], axis=0)
        return _scores_to_probs(qm, k_s[:, 0:n_keys])

    def outs(p):
        acc, pos = None, 0
        for v_ref, js in groups:
            o = _dot(p[pos:pos + len(js) * tq], v_ref[0:n_keys, :])
            for a, j in enumerate(js):
                oj = o[a * tq:(a + 1) * tq]
                oj = oj * (1.0 / oj[:, stages[j][2]:stages[j][2] + 1]) * stages[j][3]
                acc = oj if acc is None else acc + oj
            pos += len(js) * tq
        return finish(acc)

    p_s[...] = jnp.ones_like(p_s)

    def body(i, carry):
        q = q_s[pl.ds(pl.multiple_of(ctx + i * tq, tq), tq), :]
        prev = jnp.maximum(i - 1, 0)
        y_ref[pl.ds(pl.multiple_of(prev * tq, tq), tq), :] = outs(p_s[...]).astype(y_ref.dtype)
        p_s[...] = probs(q)
        return carry

    n_tiles = seq // tq
    lax.fori_loop(0, n_tiles, body, 0, unroll=unroll)
    y_ref[(n_tiles - 1) * tq:n_tiles * tq, :] = outs(p_s[...]).astype(y_ref.dtype)


def _context_attention(q, k_t, stages, finish):
    acc = None
    for st in stages:
        o = _probs_times_v(_scores_to_probs(q * st[0], k_t), st[1][0:k_t.shape[1], :], st[2]) * st[3]
        acc = o if acc is None else acc + o
    return finish(acc)


def _gqa_kernel(xl_ref, xc_ref, cos_ref, sin_ref, gq_ref, gk_ref, yl_ref, yc_ref, q_s, k_s, va_s, vb_s, p_s, acc_s,
                *, seq, ctx, tq, unroll, stacked):
    ones_g = _group_ones(GROUP_W, HEAD_DIM)
    scale = HEAD_DIM ** -0.5 * LOG2E

    def prep(x_ref, rows, dst, rope_rows):
        q = _group_rmsnorm(x_ref[rows, 0:GROUP_W].astype(F32), gq_ref[...], ones_g, HEAD_DIM)
        k = _group_rmsnorm(x_ref[rows, GROUP_W:2 * GROUP_W].astype(F32), gk_ref[...], ones_g, HEAD_DIM)
        if rope_rows is not None:
            cos_t, sin_t = cos_ref[rope_rows, :], sin_ref[rope_rows, :]
            q, k = _rope(q, cos_t, sin_t, HEAD_DIM // 2), _rope(k, cos_t, sin_t, HEAD_DIM // 2)
        q_s[dst, :] = (q * scale).astype(BF16)
        k_s[:, dst] = jnp.transpose(k).astype(BF16)
        va_s[dst, :], vb_s[dst, :] = _ones_lane_variants(x_ref[rows, 2 * GROUP_W:3 * GROUP_W])

    prep(xc_ref, pl.ds(0, ctx), pl.ds(0, ctx), None)

    def prep_body(i, carry):
        r = pl.ds(pl.multiple_of(i * tq, tq), tq)
        prep(xl_ref, r, pl.ds(pl.multiple_of(ctx + i * tq, tq), tq), r)
        return carry

    lax.fori_loop(0, seq // tq, prep_body, 0)

    stages = [(_head_mask((1, GROUP_W), h, HEAD_DIM, BF16), *_head_v(h, va_s, vb_s),
               _head_mask((1, GROUP_W), h, HEAD_DIM, F32)) for h in range(N_HEADS)]
    finish = lambda acc: acc
    yc_ref[...] = _context_attention(q_s[0:ctx, :], k_s[:, 0:ctx], stages, finish).astype(yc_ref.dtype)
    if stacked:
        _attention_stacked(q_s, k_s, yl_ref, p_s, stages, finish, ctx=ctx, seq=seq, tq=tq, unroll=unroll)
    else:
        _attention_stages(q_s, k_s, yl_ref, p_s, acc_s, stages, finish, ctx=ctx, seq=seq, tq=tq, unroll=unroll)


def _gqa_mixer(p_gqa, cos_t, sin_t, gq, gk, *, n_batch, seq, ctx, tq=ATTN_TQ, unroll=ATTN_UNROLL, lag=ATTN_LAG,
               stacked=False):
    lat_blocks = n_batch * seq // ctx
    kern = functools.partial(_gqa_kernel, seq=seq, ctx=ctx, tq=tq, unroll=unroll, stacked=stacked)
    yl, yc = pl.pallas_call(
        kern,
        grid=(n_batch,),
        in_specs=[pl.BlockSpec((seq, IN_GQA), lambda b: (b, 0)),
                  pl.BlockSpec((ctx, IN_GQA), lambda b: (lat_blocks + b, 0)),
                  _resident(cos_t.shape), _resident(sin_t.shape),
                  _resident((1, GROUP_W)), _resident((1, GROUP_W))],
        out_specs=[pl.BlockSpec((seq, GROUP_W), lambda b: (b, 0)),
                   pl.BlockSpec((ctx, GROUP_W), lambda b: (b, 0))],
        out_shape=[jax.ShapeDtypeStruct((n_batch * seq, GROUP_W), BF16),
                   jax.ShapeDtypeStruct((n_batch * ctx, GROUP_W), BF16)],
        scratch_shapes=[pltpu.VMEM((ctx + seq, GROUP_W), BF16), pltpu.VMEM((GROUP_W, ctx + seq), BF16)]
                       + [pltpu.VMEM((ctx + seq, GROUP_W), BF16)] * 2
                       + [pltpu.VMEM((N_HEADS * tq, ctx + seq) if stacked else (lag, tq, ctx + seq), BF16),
                          pltpu.VMEM((tq, GROUP_W), F32)],
        compiler_params=_cparams("parallel"),
        name="gqa_mixer",
    )(p_gqa, p_gqa, cos_t, sin_t, gq, gk)
    return yl, yc


def _diff_kernel(xl_ref, xc_ref, cos_ref, sin_ref, gq_ref, gk_ref, lam_ref, sub_ref, yl_ref, yc_ref,
                 q_s, k_s, va_s, vb_s, p_s, acc_s, *, seq, ctx, tq, unroll, stacked, lam_init):
    ones_sub = _group_ones(GROUP_W, DIFF_SUB)
    ones_head = _group_ones(GROUP_W, HEAD_DIM)
    scale = DIFF_SUB ** -0.5 * LOG2E
    lp = lam_ref[...]
    lam = (jnp.exp(jnp.sum(lp[0:1] * lp[1:2], axis=-1, keepdims=True))
           - jnp.exp(jnp.sum(lp[2:3] * lp[3:4], axis=-1, keepdims=True)) + lam_init)

    def prep(x_ref, rows, dst, rope_rows):
        q = _group_rmsnorm(x_ref[rows, 0:GROUP_W].astype(F32), gq_ref[...], ones_sub, DIFF_SUB)
        k = _group_rmsnorm(x_ref[rows, GROUP_W:2 * GROUP_W].astype(F32), gk_ref[...], ones_sub, DIFF_SUB)
        if rope_rows is not None:
            cos_t, sin_t = cos_ref[rope_rows, :], sin_ref[rope_rows, :]
            q, k = _rope(q, cos_t, sin_t, DIFF_SUB // 2), _rope(k, cos_t, sin_t, DIFF_SUB // 2)
        q_s[dst, :] = (q * scale).astype(BF16)
        k_s[:, dst] = jnp.transpose(k).astype(BF16)
        va_s[dst, :], vb_s[dst, :] = _ones_lane_variants(x_ref[rows, 2 * GROUP_W:3 * GROUP_W])

    prep(xc_ref, pl.ds(0, ctx), pl.ds(0, ctx), None)

    def prep_body(i, carry):
        r = pl.ds(pl.multiple_of(i * tq, tq), tq)
        prep(xl_ref, r, pl.ds(pl.multiple_of(ctx + i * tq, tq), tq), r)
        return carry

    lax.fori_loop(0, seq // tq, prep_body, 0)

    stages = []
    for h in range(N_HEADS):
        head = _head_mask((1, GROUP_W), h, HEAD_DIM, F32)
        for j, weight in enumerate((head, -lam * head)):
            stages.append((_head_mask((1, GROUP_W), 2 * h + j, DIFF_SUB, BF16), *_head_v(h, va_s, vb_s), weight))
    finish = lambda acc: _group_rmsnorm(acc, sub_ref[...], ones_head, HEAD_DIM) * (1.0 - lam_init)
    yc_ref[...] = _context_attention(q_s[0:ctx, :], k_s[:, 0:ctx], stages, finish).astype(yc_ref.dtype)
    if stacked:
        _attention_stacked(q_s, k_s, yl_ref, p_s, stages, finish, ctx=ctx, seq=seq, tq=tq, unroll=unroll)
    else:
        _attention_stages(q_s, k_s, yl_ref, p_s, acc_s, stages, finish, ctx=ctx, seq=seq, tq=tq, unroll=unroll)


def _diff_mixer(p_df, cos_t, sin_t, gq, gk, lam_p, sub_g, *, lam_init, n_batch, seq, ctx,
                tq=ATTN_TQ, unroll=ATTN_UNROLL, lag=ATTN_LAG, stacked=False):
    lat_blocks = n_batch * seq // ctx
    kern = functools.partial(_diff_kernel, seq=seq, ctx=ctx, tq=tq, unroll=unroll, stacked=stacked,
                             lam_init=lam_init)
    yl, yc = pl.pallas_call(
        kern,
        grid=(n_batch,),
        in_specs=[pl.BlockSpec((seq, IN_DIFF), lambda b: (b, 0)),
                  pl.BlockSpec((ctx, IN_DIFF), lambda b: (lat_blocks + b, 0)),
                  _resident(cos_t.shape), _resident(sin_t.shape),
                  _resident((1, GROUP_W)), _resident((1, GROUP_W)),
                  _resident(lam_p.shape), _resident((1, GROUP_W))],
        out_specs=[pl.BlockSpec((seq, GROUP_W), lambda b: (b, 0)),
                   pl.BlockSpec((ctx, GROUP_W), lambda b: (b, 0))],
        out_shape=[jax.ShapeDtypeStruct((n_batch * seq, GROUP_W), BF16),
                   jax.ShapeDtypeStruct((n_batch * ctx, GROUP_W), BF16)],
        scratch_shapes=[pltpu.VMEM((ctx + seq, GROUP_W), BF16), pltpu.VMEM((GROUP_W, ctx + seq), BF16)]
                       + [pltpu.VMEM((ctx + seq, GROUP_W), BF16)] * 2
                       + [pltpu.VMEM((2 * N_HEADS * tq, ctx + seq) if stacked else (lag, tq, ctx + seq), BF16),
                          pltpu.VMEM((tq, GROUP_W), F32)],
        compiler_params=_cparams("parallel"),
        name="diff_attention_mixer",
    )(p_df, p_df, cos_t, sin_t, gq, gk, lam_p, sub_g)
    return yl, yc


def _dot_split2(x, w_bf16):
    hi = x.astype(BF16)
    lo = (x - hi.astype(F32)).astype(BF16)
    return _dot(hi, w_bf16) + _dot(lo, w_bf16)


def _mlstm_kernel(xl_ref, xc_ref, gl_ref, gc_ref, cw_ref, cb_ref, gb_ref, ng_ref, yl_ref, yc_ref,
                  q_s, k_s, hsum, c_s, n_s, m_s, *, seq, ctx, unroll):
    t = ML_CHUNK
    n_chunks = seq // t
    ones_head = _group_ones(GROUP_W, HEAD_DIM)
    block_diag = ones_head.astype(F32)
    r_i = lax.broadcasted_iota(jnp.int32, (t, t), 0)
    c_i = lax.broadcasted_iota(jnp.int32, (t, t), 1)
    tril_b = jnp.where(c_i <= r_i, 1.0, 0.0).astype(BF16)
    causal = (c_i <= r_i, c_i >= r_i)
    row_g = lax.broadcasted_iota(jnp.int32, (t, IN_GATE), 0)
    sel_r = lax.broadcasted_iota(jnp.int32, (IN_GATE, GROUP_W), 0)
    sel_c = lax.broadcasted_iota(jnp.int32, (IN_GATE, GROUP_W), 1) // HEAD_DIM
    select = [jnp.where(sel_r == 8 * d + sel_c, 1.0, 0.0).astype(BF16) for d in range(2)]
    q_masks = [_head_mask((1, GROUP_W), h, HEAD_DIM, BF16) for h in range(N_HEADS)]

    for x_ref, rows, off in ((xc_ref, ctx, 0), (xl_ref, seq, ctx)):
        qk = _silu(_dwconv3(x_ref[:, 0:2 * GROUP_W].astype(F32), cw_ref[...], cb_ref[...]))
        q_s[off:off + rows, :] = (qk[:, 0:GROUP_W] * (HEAD_DIM ** -0.5)).astype(BF16)
        k_s[off:off + rows, :] = qk[:, GROUP_W:2 * GROUP_W].astype(BF16)

    hsum[...] = jnp.zeros_like(hsum)
    c_s[...] = jnp.zeros_like(c_s)
    n_s[...] = jnp.zeros_like(n_s)
    m_s[...] = jnp.zeros_like(m_s)

    def running_max(x, d):
        step = 1
        while step < t:
            if d == 0:
                shifted, valid = pltpu.roll(x, step, 0), row_g >= step
            else:
                shifted, valid = pltpu.roll(x, t - step, 0), row_g < t - step
            x = jnp.maximum(x, jnp.where(valid, shifted, -jnp.inf))
            step *= 2
        return x

    def chunk(d, rows, v, g_raw):
        g = g_raw + gb_ref[...]
        logf = pltpu.roll(jax.nn.log_sigmoid(g), IN_GATE - N_HEADS, 1)
        pre = _dot_exact_rhs(tril_b, logf)
        tot = pre[t - 1:t, :]
        cum = pre if d == 0 else tot - pre + logf
        r = g - cum
        m_prev = m_s[d]
        log_inter = cum + m_prev
        m_t = jnp.maximum(log_inter, cum + running_max(r, d))
        a_t = cum - m_t
        log_g = (tot - cum) + g
        m_new = jnp.maximum(tot + m_prev, jnp.max(log_g, axis=0, keepdims=True))
        m_s[d] = m_new
        rowwise = jnp.concatenate([jnp.exp(log_inter - m_t), jnp.exp(-m_t), jnp.exp(log_g - m_new),
                                   jnp.broadcast_to(jnp.exp(tot + m_prev - m_new), (8, IN_GATE))], axis=0)
        full = _dot_split2(rowwise, select[d])
        w_inter, e_m, w_g, w_c = full[0:t], full[t:2 * t], full[2 * t:3 * t], full[3 * t:3 * t + 1]

        q, k = q_s[rows, :], k_s[rows, :]
        r_t = jnp.transpose(r * LOG2E)
        a_2 = a_t * LOG2E
        pair = [None, None]
        for h in range(N_HEADS):
            col = 8 * d + h
            w_intra = jnp.exp2(jnp.where(causal[d], a_2[:, col:col + 1] + r_t[col:col + 1, :], -jnp.inf))
            s = (_dot_nt(q * q_masks[h], k) * w_intra).astype(BF16)
            o = _dot(s, v * q_masks[h] + q_masks[(h + 2) % N_HEADS])
            pair[h // 2] = o if pair[h // 2] is None else pair[h // 2] + o
        half = GROUP_W // 2
        num_i = jnp.concatenate([pair[0][:, :half], pair[1][:, half:]], axis=1)
        den_i = jnp.concatenate([pair[0][:, half:], pair[1][:, :half]], axis=1)
        ct, n_full = c_s[d], n_s[d]
        inter = _dot(q, ct.astype(BF16))
        qn = _dot((q.astype(F32) * n_full).astype(BF16), ones_head)
        num = w_inter * inter + num_i
        den = w_inter * qn + den_i
        kw = k.astype(F32) * w_g
        c_s[d] = w_c * ct + _dot_tn(kw.astype(BF16), v) * block_diag
        n_s[d] = w_c * n_full + jnp.sum(kw, axis=0, keepdims=True)
        return num / jnp.maximum(jnp.abs(den), e_m)

    assert ctx == t
    v_c = xc_ref[:, 2 * GROUP_W:3 * GROUP_W]
    hsum[0:t, :] = chunk(0, pl.ds(0, t), v_c, gc_ref[...]) + chunk(1, pl.ds(0, t), v_c, gc_ref[...])

    def body(j, carry):
        for d in range(2):
            cidx = j if d == 0 else n_chunks - 1 - j
            lat = pl.ds(pl.multiple_of(cidx * t, t), t)
            rows = pl.ds(pl.multiple_of(ctx + cidx * t, t), t)
            hsum[rows, :] += chunk(d, rows, xl_ref[lat, 2 * GROUP_W:3 * GROUP_W], gl_ref[lat, :])
        return carry

    lax.fori_loop(0, n_chunks, body, 0, unroll=unroll)

    for x_ref, y_ref, rows, off in ((xc_ref, yc_ref, ctx, 0), (xl_ref, yl_ref, seq, ctx)):
        hn = _group_rmsnorm(hsum[off:off + rows, :], ng_ref[...], ones_head, HEAD_DIM)
        y_ref[...] = (hn * jax.nn.sigmoid(x_ref[:, 3 * GROUP_W:4 * GROUP_W].astype(F32))).astype(y_ref.dtype)


def _mlstm_mixer(p_ml, p_gate, conv_w, conv_b, gate_b, norm_g, *, n_batch, seq, ctx, unroll=1):
    lat_blocks = n_batch * seq // ctx
    kern = functools.partial(_mlstm_kernel, seq=seq, ctx=ctx, unroll=unroll)
    yl, yc = pl.pallas_call(
        kern,
        grid=(n_batch,),
        in_specs=[pl.BlockSpec((seq, IN_ML), lambda b: (b, 0)),
                  pl.BlockSpec((ctx, IN_ML), lambda b: (lat_blocks + b, 0)),
                  pl.BlockSpec((seq, IN_GATE), lambda b: (b, 0)),
                  pl.BlockSpec((ctx, IN_GATE), lambda b: (lat_blocks + b, 0)),
                  _resident(conv_w.shape), _resident(conv_b.shape), _resident(gate_b.shape), _resident(norm_g.shape)],
        out_specs=[pl.BlockSpec((seq, GROUP_W), lambda b: (b, 0)),
                   pl.BlockSpec((ctx, GROUP_W), lambda b: (b, 0))],
        out_shape=[jax.ShapeDtypeStruct((n_batch * seq, GROUP_W), BF16),
                   jax.ShapeDtypeStruct((n_batch * ctx, GROUP_W), BF16)],
        scratch_shapes=[pltpu.VMEM((ctx + seq, GROUP_W), BF16), pltpu.VMEM((ctx + seq, GROUP_W), BF16),
                        pltpu.VMEM((ctx + seq, GROUP_W), F32),
                        pltpu.VMEM((2, GROUP_W, GROUP_W), F32), pltpu.VMEM((2, 1, GROUP_W), F32),
                        pltpu.VMEM((2, 1, IN_GATE), F32)],
        compiler_params=_cparams("parallel"),
        name="mlstm_mixer",
    )(p_ml, p_ml, p_gate, p_gate, conv_w, conv_b, gate_b, norm_g)
    return yl, yc


def _phase_tables(idx_a, idx_b, n):
    ang = (2.0 * math.pi / n) * ((idx_a[:, None] * idx_b[None, :]) % n).astype(F32)
    return jnp.cos(ang), jnp.sin(ang)


def _dft_matrix(length, n):
    nf, lo = n // 2, 64
    k = jnp.arange(nf, dtype=jnp.int32)
    k1, k0 = lo * jnp.arange(nf // lo, dtype=jnp.int32), jnp.arange(lo, dtype=jnp.int32)
    j = jnp.arange(length, dtype=jnp.int32)
    (ca, sa), (cb, sb) = _phase_tables(k1, j, n), _phase_tables(k0, j, n)
    cos_f = (ca[:, None, :] * cb[None, :, :] - sa[:, None, :] * sb[None, :, :]).reshape(nf, length)
    sin_f = (sa[:, None, :] * cb[None, :, :] + ca[:, None, :] * sb[None, :, :]).reshape(nf, length)
    alt = jnp.where(j % 2 == 0, 1.0, -1.0).astype(F32)
    return jnp.concatenate([cos_f, jnp.where(k[:, None] == 0, alt[None, :], sin_f)], axis=0).astype(BF16)


def _hyena_filter_consts(length, n):
    t = jnp.arange(length, dtype=F32)
    tn = t / length
    bands = jnp.arange(1, HY_POS_BANDS + 1, dtype=F32)
    ang = 2.0 * math.pi * tn[:, None] * bands
    feats = jnp.concatenate([tn[:, None], jnp.cos(ang), jnp.sin(ang)], axis=-1)
    feats = jnp.pad(feats, ((0, 0), (0, 128 - feats.shape[1])))
    dist = jnp.abs(t - length // 2) / (length / 2)
    deltas = jnp.abs(jnp.linspace(math.log(HY_DECAY_TARGET) / HY_SLOW_DECAY,
                                  math.log(HY_DECAY_TARGET) / HY_FAST_DECAY, GROUP_W, dtype=F32))
    window = jnp.exp(-dist[:, None] * jnp.tile(deltas, 2))
    k = jnp.arange(n // 2, dtype=jnp.int32)
    phi = (2.0 * math.pi / n) * ((k * (length // 2)) % n).astype(F32)
    shift = jnp.stack([jnp.cos(phi), jnp.sin(phi)], axis=-1) * (2.0 / n)
    return feats, window, shift


def _filter_kernel(feats_ref, win_ref, shift_ref, w1lo_ref, w1hi_ref, b1_ref, w2_ref, b2_ref, w3t_ref, w3b_ref, b3_ref,
                   f_ref, al_ref, be_ref, ga_ref, hid_s, *, nyquist_sign):
    half = feats_ref.shape[0] // 2

    @pl.when(pl.program_id(1) == 0)
    def _():
        pre = (_dot_f32ish(feats_ref[0:half, :], w1lo_ref[0]) + _dot_f32ish(feats_ref[half:2 * half, :], w1hi_ref[0]))
        h = jnp.sin(pre + b1_ref[0])
        hid_s[...] = jnp.sin(_dot_f32ish(h, w2_ref[0]) + b2_ref[0])

    hid = hid_s[...]
    h = jnp.concatenate([_dot_f32ish(hid, w3t_ref[0]), _dot_f32ish(hid, w3b_ref[0])], axis=0)
    h = (h + b3_ref[0]) * win_ref[...]
    h = h / jnp.sum(jnp.abs(h), axis=0, keepdims=True)
    h1 = h.astype(BF16)
    h2 = (h - h1.astype(F32)).astype(BF16)
    nf = f_ref.shape[0] // 2
    step = math.gcd(nf, 512)
    for r in range(0, nf, step):
        fc, fs = f_ref[r:r + step, :], f_ref[nf + r:nf + r + step, :]
        p = _dot(fc, h1) + _dot(fc, h2)
        q = _dot(fs, h1) + _dot(fs, h2)
        cs, sn = shift_ref[r:r + step, 0:1], shift_ref[r:r + step, 1:2]
        alpha = p * cs + q * sn
        beta = p * sn - q * cs
        if r == 0:
            first = lax.broadcasted_iota(jnp.int32, p.shape, 0) == 0
            al_ref[0, r:r + step, :] = jnp.where(first, 0.5 * alpha, alpha)
            be_ref[0, r:r + step, :] = jnp.where(first, 0.0, beta)
            ga_ref[0, r:r + step, :] = jnp.where(first, (0.5 * nyquist_sign) * q * cs, alpha)
        else:
            al_ref[0, r:r + step, :] = alpha
            be_ref[0, r:r + step, :] = beta
            ga_ref[0, r:r + step, :] = alpha


def _hyena_filter_spectra(length, fwd, w1, b1, w2, b2, w3, b3):
    depth, hid = w1.shape[0], w1.shape[2]
    n = fwd.shape[0]
    feats, window, shift = _hyena_filter_consts(length, n)
    nf = n // 2
    place = lambda a, r0, c0, rows, cols: jnp.pad(a, ((0, 0), (r0, rows - r0 - a.shape[1]), (c0, cols - c0 - a.shape[2])))
    w1lo, w1hi = place(w1, 0, 0, 128, 128), place(w1, 0, hid, 128, 128)
    w2bd = place(w2, 0, 0, 128, 128) + place(w2, hid, hid, 128, 128)
    w3t, w3b = place(w3, 0, 0, 128, w3.shape[2]), place(w3, hid, 0, 128, w3.shape[2])
    b1p, b2p = jnp.tile(b1, (1, 2))[:, None, :], jnp.tile(b2, (1, 2))[:, None, :]
    b3p = b3[:, None, :]
    lay = lambda a: pl.BlockSpec((1,) + a.shape[1:], lambda l, o: (l, 0, 0))
    per_order = lambda rows: pl.BlockSpec((1, rows, GROUP_W), lambda l, o: (l, 0, o))
    out = jax.ShapeDtypeStruct((depth, nf, 2 * GROUP_W), F32)
    kern = functools.partial(_filter_kernel, nyquist_sign=1.0 if (length // 2) % 2 == 0 else -1.0)
    return pl.pallas_call(
        kern,
        grid=(depth, 2),
        in_specs=[_resident(feats.shape), pl.BlockSpec((length, GROUP_W), lambda l, o: (0, o)), _resident(shift.shape),
                  lay(w1lo), lay(w1hi), lay(b1p), lay(w2bd), lay(b2p), per_order(128), per_order(128), per_order(1),
                  _resident(fwd.shape)],
        out_specs=[per_order(nf)] * 3,
        out_shape=[out, out, out],
        scratch_shapes=[pltpu.VMEM((length // 2, 128), F32)],
        compiler_params=_cparams("parallel", "arbitrary"),
        name="hyena_filter_spectra",
    )(feats, window, shift, w1lo, w1hi, b1p, w2bd, b2p, w3t, w3b, b3p, fwd)


def _hyena_conv_kernel(sig_ref, gate_ref, cws_ref, cbs_ref, cwg_ref, cbg_ref, skip_ref, f_ref,
                       al_ref, be_ref, ga_ref, o_ref, *, sig_conv):
    s = sig_ref[...].astype(F32)
    if sig_conv:
        s = _dwconv3(s, cws_ref[...], cbs_ref[...])
    gate = _dwconv3(gate_ref[...].astype(F32), cwg_ref[...], cbg_ref[...])
    sb = s.astype(BF16)
    nf = f_ref.shape[0] // 2
    a = _dot(f_ref[0:nf, :], sb)
    b = _dot(f_ref[nf:2 * nf, :], sb)
    ay = a * al_ref[0] + b * be_ref[0]
    by = b * ga_ref[0] - a * be_ref[0]
    y = _dot_tn(f_ref[0:nf, :], ay.astype(BF16)) + _dot_tn(f_ref[nf:2 * nf, :], by.astype(BF16))
    o_ref[...] = (gate * (y + skip_ref[...] * s)).astype(o_ref.dtype)


def _hyena_conv(sig, sig_block, sig_col, p_hy, gate_col, conv_w, conv_b, skip, fwd, spectra, order, layer,
                *, rows, first_block, n_batch, out_dtype, sig_conv):
    cw = lambda col: pl.BlockSpec((3, GROUP_W), lambda b: (0, col))
    cb = lambda col: pl.BlockSpec((1, GROUP_W), lambda b: (0, col))
    spec = pl.BlockSpec((1, fwd.shape[0] // 2, GROUP_W), lambda b: (layer, 0, order), pipeline_mode=pl.Buffered(1))
    kern = functools.partial(_hyena_conv_kernel, sig_conv=sig_conv)
    return pl.pallas_call(
        kern,
        grid=(n_batch,),
        in_specs=[pl.BlockSpec((rows, GROUP_W), lambda b: (sig_block + b, sig_col)),
                  pl.BlockSpec((rows, GROUP_W), lambda b: (first_block + b, gate_col)),
                  cw(sig_col), cb(sig_col), cw(gate_col), cb(gate_col),
                  pl.BlockSpec((1, GROUP_W), lambda b: (0, 0)),
                  _resident(fwd.shape), spec, spec, spec],
        out_specs=pl.BlockSpec((rows, GROUP_W), lambda b: (b, 0)),
        out_shape=jax.ShapeDtypeStruct((n_batch * rows, GROUP_W), out_dtype),
        compiler_params=_cparams("parallel"),
        name="hyena_long_conv",
    )(sig, p_hy, conv_w, conv_b, conv_w, conv_b, skip, fwd, *spectra)


def _hyena_mixer(p_hy, conv_w, conv_b, skip, fwd, spectra, layer, *, rows, first_block, n_batch):
    common = dict(rows=rows, first_block=first_block, n_batch=n_batch)
    z = _hyena_conv(p_hy, first_block, 0, p_hy, 1, conv_w, conv_b, skip[0:1], fwd, spectra, 0, layer,
                    out_dtype=F32, sig_conv=True, **common)
    return _hyena_conv(z, 0, 0, p_hy, 2, conv_w, conv_b, skip[1:2], fwd, spectra, 1, layer,
                       out_dtype=BF16, sig_conv=False, **common)


def _rope_tables(length, dim, reps):
    rows = length // GRID_W
    row = jnp.repeat(jnp.arange(rows), GRID_W).astype(F32)
    col = jnp.tile(jnp.arange(GRID_W), rows).astype(F32)
    n_freq = dim // 4
    inv = ROPE_THETA ** (-jnp.arange(n_freq, dtype=F32) / n_freq)
    ang = jnp.concatenate([row[:, None] * inv, col[:, None] * inv], axis=-1)
    cos, sin = jnp.cos(ang), jnp.sin(ang)
    return (jnp.tile(jnp.concatenate([cos, cos], axis=-1), (1, reps)),
            jnp.tile(jnp.concatenate([-sin, sin], axis=-1), (1, reps)))


def kernel(x, c, ctx, c_ctx, ada_w, ada_b, norm_g, ffn_w13, ffn_w2, w_in, w_out, ml_gate_b, ml_conv_w, ml_conv_b,
           ml_norm_g, gqa_qk_g, hy_conv_w, hy_conv_b, hy_filt_w1, hy_filt_b1, hy_filt_w2, hy_filt_b2, hy_filt_w3,
           hy_filt_b3, hy_skip, diff_qk_g, diff_lambda, diff_subln_g):
    n_batch, seq, d = x.shape
    n_ctx = ctx.shape[1]
    depth = ada_w.shape[0]
    n_lat = n_batch * seq
    n_all = n_lat + n_batch * n_ctx
    assert n_batch * n_ctx == seq and n_ctx == ML_CHUNK and seq % ROW_TILE == 0

    c_all = jnp.pad(jnp.concatenate([c, c_ctx[None]], axis=0), ((0, 16 - n_batch - 1), (0, 0)))
    mod = _modulation(c_all, ada_w, ada_b)[:, :n_batch + 1].reshape(depth, n_batch + 1, N_MOD, d)

    w13 = ffn_w13.astype(BF16)
    w2 = ffn_w2.astype(BF16)
    w_out_b = w_out.astype(BF16)

    rope_gqa = _rope_tables(seq, HEAD_DIM, N_HEADS)
    rope_diff = _rope_tables(seq, DIFF_SUB, 2 * N_HEADS)
    dft_l = _dft_matrix(seq, 3 * seq // 2)
    dft_c = _dft_matrix(n_ctx, 2 * n_ctx)
    filt = (hy_filt_w1, hy_filt_b1, hy_filt_w2, hy_filt_b2, hy_filt_w3, hy_filt_b3)
    spec_l = _hyena_filter_spectra(seq, dft_l, *filt)
    spec_c = _hyena_filter_spectra(n_ctx, dft_c, *filt)

    h = (x.reshape(n_lat, d), ctx.reshape(n_batch * n_ctx, d))
    dims = dict(seq=seq, n_batch=n_batch)
    gqa_cfg = [dict(tq=256, unroll=4, stacked=True), dict(tq=256, unroll=8, stacked=True),
               dict(tq=512, unroll=4, lag=1), dict(tq=512, unroll=4, lag=2)]
    diff_cfg = [dict(tq=128, unroll=4, stacked=True), dict(tq=128, unroll=2, stacked=True),
                dict(tq=256, unroll=2, lag=1), dict(tq=256, unroll=2, lag=2)]
    ffn_cfg = dict(tm=1024, chunk=256)
    for l in range(depth):
        need_ctx = l < depth - 1
        lam_init = 0.8 - 0.6 * math.exp(-0.3 * l)
        h = _ffn_half(h, mod[l], norm_g[l, 0], w13, w2, layer=l, half=0, base=0, n_rows=n_all, **ffn_cfg, **dims)
        p_ml, p_gqa, p_hy, p_df, p_gate = _in_projection(h, mod[l], norm_g[l, 1], w_in, layer=l, tm=1024, **dims)

        gate_b = jnp.pad(ml_gate_b[l], (0, IN_GATE - ml_gate_b.shape[1]))[None]
        y_ml = _mlstm_mixer(p_ml, p_gate, ml_conv_w[l], ml_conv_b[l][None], gate_b, ml_norm_g[l][None],
                            n_batch=n_batch, seq=seq, ctx=n_ctx, unroll=8)
        tile_g = lambda g, reps: jnp.tile(g, reps)[None]
        y_gqa = _gqa_mixer(p_gqa, *rope_gqa, tile_g(gqa_qk_g[l, 0], N_HEADS), tile_g(gqa_qk_g[l, 1], N_HEADS),
                           n_batch=n_batch, seq=seq, ctx=n_ctx, **gqa_cfg[l])
        y_df = _diff_mixer(p_df, *rope_diff, tile_g(diff_qk_g[l, 0], 2 * N_HEADS), tile_g(diff_qk_g[l, 1], 2 * N_HEADS),
                           diff_lambda[l], tile_g(diff_subln_g[l], N_HEADS),
                           lam_init=lam_init, n_batch=n_batch, seq=seq, ctx=n_ctx, **diff_cfg[l])
        y_hy = _hyena_mixer(p_hy, hy_conv_w[l], hy_conv_b[l][None], hy_skip[l], dft_l, spec_l, l,
                            rows=seq, first_block=0, n_batch=n_batch)
        y_hy_c = y_hy if not need_ctx else _hyena_mixer(
            p_hy, hy_conv_w[l], hy_conv_b[l][None], hy_skip[l], dft_c, spec_c, l,
            rows=n_ctx, first_block=n_lat // n_ctx, n_batch=n_batch)
        h = _ffn_half(h, mod[l], norm_g[l, 2], w13, w2, layer=l, half=1, base=6,
                      n_rows=n_all if need_ctx else n_lat,
                      mix=[y_ml, y_gqa, (y_hy, y_hy_c), y_df], w_out=w_out_b, **ffn_cfg, **dims)
    return h[:n_lat].reshape(n_batch, seq, d)
```

```python
import functools
import math

import jax
import jax.numpy as jnp
from jax import lax
from jax.experimental import pallas as pl
from jax.experimental.pallas import tpu as pltpu

F32 = jnp.float32
BF16 = jnp.bfloat16

EPS = 1e-6
ROPE_THETA = 10000.0
GRID_W = 64
N_MOD = 9
GROUP_W = 256
HEAD_DIM = 64
N_HEADS = 4
DIFF_SUB = 32
ML_CHUNK = 256
HY_POS_BANDS = 16
HY_DECAY_TARGET = 1e-2
HY_FAST_DECAY = 0.3
HY_SLOW_DECAY = 1.5
ROW_TILE = 1024
FFN_CHUNK = 256
VMEM_LIMIT = 56 * 1024 * 1024


def _cparams(*sem):
    return pltpu.CompilerParams(dimension_semantics=sem, vmem_limit_bytes=VMEM_LIMIT)


def _resident(shape):
    nd = len(shape)
    return pl.BlockSpec(shape, lambda *_: (0,) * nd, pipeline_mode=pl.Buffered(1))


def _resident_slice(arr, index):
    tail = arr.shape[len(index):]
    return pl.BlockSpec((None,) * len(index) + tail, lambda *_: tuple(index) + (0,) * len(tail),
                        pipeline_mode=pl.Buffered(1))


def _dot(a, b):
    return jnp.dot(a, b, preferred_element_type=F32)


def _dot_nt(a, b):
    return lax.dot_general(a, b, (((1,), (1,)), ((), ())), preferred_element_type=F32)


def _dot_tn(a, b):
    return lax.dot_general(a, b, (((0,), (0,)), ((), ())), preferred_element_type=F32)


def _split3(x):
    x1 = x.astype(BF16)
    r = x - x1.astype(F32)
    x2 = r.astype(BF16)
    x3 = (r - x2.astype(F32)).astype(BF16)
    return x1, x2, x3


def _dot_exact_rhs(a_bf16, x):
    x1, x2, x3 = _split3(x)
    return _dot(a_bf16, x1) + _dot(a_bf16, x2) + _dot(a_bf16, x3)


def _dot_f32ish(a, b):
    a1 = a.astype(BF16)
    a2 = (a - a1.astype(F32)).astype(BF16)
    b1 = b.astype(BF16)
    b2 = (b - b1.astype(F32)).astype(BF16)
    return _dot(a1, b1) + _dot(a1, b2) + _dot(a2, b1)


def _silu(x):
    return x * jax.nn.sigmoid(x)


def _modnorm(x, g, shift, scale):
    y = x * lax.rsqrt(jnp.mean(x * x, axis=-1, keepdims=True) + EPS)
    return (y * g) * (1.0 + scale) + shift


def _lane_iota(shape):
    return lax.broadcasted_iota(jnp.int32, shape, len(shape) - 1)


def _group_ones(n, group):
    r = lax.broadcasted_iota(jnp.int32, (n, n), 0) // group
    c = lax.broadcasted_iota(jnp.int32, (n, n), 1) // group
    return jnp.where(r == c, 1.0, 0.0).astype(BF16)


def _group_rmsnorm(x, gain, ones_g, group):
    x2 = x * x
    hi = x2.astype(BF16)
    lo = (x2 - hi.astype(F32)).astype(BF16)
    ms = (_dot(hi, ones_g) + _dot(lo, ones_g)) * (1.0 / group)
    return x * lax.rsqrt(ms + EPS) * gain


def _rope(x, cos_t, sin_t, half):
    n = x.shape[-1]
    left = pltpu.roll(x, n - half, 1)
    right = pltpu.roll(x, half, 1)
    sw = jnp.where((_lane_iota(x.shape) & (2 * half - 1)) < half, left, right)
    return x * cos_t + sw * sin_t


def _dwconv3(x, w, b):
    rows = x.shape[0]
    row = lax.broadcasted_iota(jnp.int32, x.shape, 0)
    xm = jnp.where(row == 0, 0.0, pltpu.roll(x, 1, 0))
    xp = jnp.where(row == rows - 1, 0.0, pltpu.roll(x, rows - 1, 0))
    return xm * w[0:1] + x * w[1:2] + xp * w[2:3] + b


def _head_mask(shape, h, width, dtype):
    lane = _lane_iota(shape)
    return jnp.where((lane >= h * width) & (lane < (h + 1) * width), 1.0, 0.0).astype(dtype)


def _mod_kernel(c_ref, w_ref, b_ref, o_ref):
    sc = _silu(c_ref[...]).astype(BF16)
    o_ref[0] = _dot(sc, w_ref[0].astype(BF16)) + b_ref[0]


def _modulation(c_all, ada_w, ada_b):
    depth, d, nmod = ada_w.shape
    rows = c_all.shape[0]
    tn = 1536
    return pl.pallas_call(
        _mod_kernel,
        grid=(depth, nmod // tn),
        in_specs=[pl.BlockSpec((rows, d), lambda l, j: (0, 0)),
                  pl.BlockSpec((1, d, tn), lambda l, j: (l, 0, j)),
                  pl.BlockSpec((1, 1, tn), lambda l, j: (l, 0, j))],
        out_specs=pl.BlockSpec((1, rows, tn), lambda l, j: (l, 0, j)),
        out_shape=jax.ShapeDtypeStruct((depth, rows, nmod), F32),
        compiler_params=_cparams("parallel", "parallel"),
        name="adaln_modulation",
    )(c_all, ada_w, ada_b.reshape(depth, 1, nmod))


def _row_specs(width, tile, n_lat_tiles):
    return [pl.BlockSpec((tile, width), lambda i: (jnp.minimum(i, n_lat_tiles - 1), 0)),
            pl.BlockSpec((tile, width), lambda i: (jnp.maximum(i - n_lat_tiles, 0), 0))]


def _pick_rows(lat_ref, ctx_ref, n_lat_tiles):
    return jnp.where(pl.program_id(0) < n_lat_tiles, lat_ref[...], ctx_ref[...])


def _ffn_kernel(*refs, base, hidden, chunk, n_lat_tiles, split_x, n_mix):
    refs = list(refs)
    x = _pick_rows(refs.pop(0), refs.pop(0), n_lat_tiles) if split_x else refs.pop(0)[...]
    mod_ref, g_ref, w13_ref, w2_ref = refs[:4]
    mix_refs, (o_ref, acc_ref) = refs[4:-2], refs[-2:]
    if n_mix:
        wo_ref = mix_refs[-1]
        mixed = None
        for k in range(n_mix):
            y = _pick_rows(mix_refs[2 * k], mix_refs[2 * k + 1], n_lat_tiles)
            part = _dot(y, wo_ref[k * GROUP_W:(k + 1) * GROUP_W, :])
            mixed = part if mixed is None else mixed + part
        x = x + mod_ref[0, 5:6] * mixed
    shift, scale, gate = mod_ref[0, base:base + 1], mod_ref[0, base + 1:base + 2], mod_ref[0, base + 2:base + 3]
    xn = _modnorm(x, g_ref[...], shift, scale).astype(BF16)
    for start in range(0, hidden, chunk):
        size = min(chunk, hidden - start)
        a = _dot(xn, w13_ref[:, start:start + size])
        b = _dot(xn, w13_ref[:, hidden + start:hidden + start + size])
        part = _dot((_silu(a) * b).astype(BF16), w2_ref[start:start + size, :])
        if start == 0:
            acc_ref[...] = part
        else:
            acc_ref[...] += part
    o_ref[...] = x + (0.5 * gate) * acc_ref[...]


def _mod_index(tile, seq, n_batch):
    return lambda i: (jnp.minimum((i * tile) // seq, n_batch), 0, 0)


def _ffn_half(h, mod, g, w13, w2, *, layer, half, base, n_rows, seq, n_batch, mix=None, w_out=None):
    split_x = isinstance(h, tuple)
    d = w13.shape[-2]
    hidden = w2.shape[-2]
    tm = ROW_TILE
    n_lat_tiles = n_batch * seq // tm
    kern = functools.partial(_ffn_kernel, base=base, hidden=hidden, chunk=FFN_CHUNK, n_lat_tiles=n_lat_tiles,
                             split_x=split_x, n_mix=len(mix) if mix else 0)
    x_specs = _row_specs(d, tm, n_lat_tiles) if split_x else [pl.BlockSpec((tm, d), lambda i: (i, 0))]
    mix_args, mix_specs = [], []
    for pair in mix or ():
        mix_args += list(pair)
        mix_specs += _row_specs(GROUP_W, tm, n_lat_tiles)
    if mix:
        mix_args.append(w_out)
        mix_specs.append(_resident_slice(w_out, (layer,)))
    return pl.pallas_call(
        kern,
        grid=(n_rows // tm,),
        in_specs=x_specs + [pl.BlockSpec((1, N_MOD, d), _mod_index(tm, seq, n_batch)),
                            _resident((1, d)), _resident_slice(w13, (layer, half)),
                            _resident_slice(w2, (layer, half))] + mix_specs,
        out_specs=pl.BlockSpec((tm, d), lambda i: (i, 0)),
        out_shape=jax.ShapeDtypeStruct((n_rows, d), F32),
        scratch_shapes=[pltpu.VMEM((tm, d), F32)],
        compiler_params=_cparams("parallel"),
        name="swiglu_half_step",
    )(*(h if split_x else (h,)), mod, g.reshape(1, d), w13, w2, *mix_args)


IN_ML, IN_GQA, IN_HY, IN_DIFF, IN_GATE = 1024, 768, 768, 768, 128


def _in_weight_pieces():
    ml, n_gate = 4 * GROUP_W, 16
    o_gqa = ml + n_gate
    o_hy = o_gqa + 2 * GROUP_W
    o_df = o_hy + 3 * GROUP_W
    pieces = [(0, 0, ml), (ml, o_gqa, GROUP_W)]
    for part, src in enumerate((o_gqa + GROUP_W, o_gqa + GROUP_W + 2 * HEAD_DIM)):
        for h in range(N_HEADS):
            pieces.append((ml + (part + 1) * GROUP_W + h * HEAD_DIM, src + (h // 2) * HEAD_DIM, HEAD_DIM))
    pieces += [(IN_ML + IN_GQA, o_hy, 3 * GROUP_W), (IN_ML + IN_GQA + IN_HY, o_df, 3 * GROUP_W),
               (IN_ML + IN_GQA + IN_HY + IN_DIFF, ml, n_gate)]
    return pieces


def _inproj_kernel(x_ref, mod_ref, g_ref, w_ref, ml_ref, gqa_ref, hy_ref, df_ref, gate_ref, wb_s):
    @pl.when(pl.program_id(0) == 0)
    def _():
        wb_s[:, IN_ML + IN_GQA + IN_HY + IN_DIFF:] = jnp.zeros((wb_s.shape[0], IN_GATE), BF16)
        for dst, src, width in _in_weight_pieces():
            wb_s[:, dst:dst + width] = w_ref[:, src:src + width].astype(BF16)

    xn = _modnorm(x_ref[...], g_ref[...], mod_ref[0, 3:4], mod_ref[0, 4:5]).astype(BF16)
    off = 0
    for ref, width in ((ml_ref, IN_ML), (gqa_ref, IN_GQA), (hy_ref, IN_HY), (df_ref, IN_DIFF), (gate_ref, IN_GATE)):
        ref[...] = _dot(xn, wb_s[:, off:off + width]).astype(ref.dtype)
        off += width


def _in_projection(h, mod, g, w, *, layer, seq, n_batch):
    n_rows, d = h.shape
    tm = ROW_TILE
    widths = (IN_ML, IN_GQA, IN_HY, IN_DIFF, IN_GATE)
    dtypes = (BF16, BF16, BF16, BF16, F32)
    return pl.pallas_call(
        _inproj_kernel,
        grid=(n_rows // tm,),
        in_specs=[pl.BlockSpec((tm, d), lambda i: (i, 0)),
                  pl.BlockSpec((1, N_MOD, d), _mod_index(tm, seq, n_batch)),
                  _resident((1, d)), _resident_slice(w, (layer,))],
        out_specs=[pl.BlockSpec((tm, wd), lambda i: (i, 0)) for wd in widths],
        out_shape=[jax.ShapeDtypeStruct((n_rows, wd), dt) for wd, dt in zip(widths, dtypes)],
        scratch_shapes=[pltpu.VMEM((d, sum(widths)), BF16)],
        compiler_params=_cparams("arbitrary"),
        name="input_projection",
    )(h, mod, g.reshape(1, d), w)


LOG2E = 1.4426950408889634
GQA_TILING = dict(tq=512, unroll=4, lag=1)
DIFF_TILING = dict(tq=256, unroll=2, lag=2)


def _ones_lane_variants(v):
    grp = _lane_iota((1, GROUP_W)) // HEAD_DIM
    odd = jnp.where(grp % 2 == 1, 1.0, 0.0).astype(BF16)
    even = jnp.where(grp % 2 == 0, 1.0, 0.0).astype(BF16)
    return v * even + odd, v * odd + even


def _scores_to_probs(qm, k_t):
    s = _dot(qm, k_t)
    return jnp.exp2(s - jnp.max(s, axis=-1, keepdims=True)).astype(BF16)


def _probs_times_v(p, v_aug, sum_lane):
    o = _dot(p, v_aug)
    return o * (1.0 / o[:, sum_lane:sum_lane + 1])


def _head_v(h, va, vb):
    return (va, (h + 1) * HEAD_DIM) if h % 2 == 0 else (vb, (h - 1) * HEAD_DIM)


def _attention_stages(q_s, k_s, y_ref, p_s, acc_s, stages, finish, *, ctx, seq, tq, unroll):
    n_keys = ctx + seq

    def probs(q, st):
        return _scores_to_probs(q * st[0], k_s[:, 0:n_keys])

    def out(p, st):
        return _probs_times_v(p, st[1][0:n_keys, :], st[2]) * st[3]

    p_s[...] = jnp.ones_like(p_s)
    acc_s[...] = jnp.zeros_like(acc_s)
    n, lag = len(stages), p_s.shape[0]

    def tail():
        acc = acc_s[...]
        for c in range(lag):
            acc = acc + out(p_s[c], stages[n - lag + c])
        return finish(acc)

    def body(i, carry):
        q = q_s[pl.ds(pl.multiple_of(ctx + i * tq, tq), tq), :]
        prev = jnp.maximum(i - 1, 0)
        y_ref[pl.ds(pl.multiple_of(prev * tq, tq), tq), :] = tail().astype(y_ref.dtype)
        ps, acc = [], None
        for j in range(n):
            ps.append(probs(q, stages[j]))
            if j >= lag:
                o = out(ps[j - lag], stages[j - lag])
                acc = o if acc is None else acc + o
        for c in range(lag):
            p_s[c] = ps[n - lag + c]
        acc_s[...] = acc
        return carry

    n_tiles = seq // tq
    lax.fori_loop(0, n_tiles, body, 0, unroll=unroll)
    y_ref[(n_tiles - 1) * tq:n_tiles * tq, :] = tail().astype(y_ref.dtype)


def _context_attention(q, k_t, stages, finish):
    acc = None
    for st in stages:
        o = _probs_times_v(_scores_to_probs(q * st[0], k_t), st[1][0:k_t.shape[1], :], st[2]) * st[3]
        acc = o if acc is None else acc + o
    return finish(acc)


def _gqa_kernel(xl_ref, xc_ref, cos_ref, sin_ref, gq_ref, gk_ref, yl_ref, yc_ref, q_s, k_s, va_s, vb_s, p_s, acc_s,
                *, seq, ctx, tq, unroll):
    ones_g = _group_ones(GROUP_W, HEAD_DIM)
    scale = HEAD_DIM ** -0.5 * LOG2E

    def prep(x_ref, rows, dst, rope_rows):
        q = _group_rmsnorm(x_ref[rows, 0:GROUP_W].astype(F32), gq_ref[...], ones_g, HEAD_DIM)
        k = _group_rmsnorm(x_ref[rows, GROUP_W:2 * GROUP_W].astype(F32), gk_ref[...], ones_g, HEAD_DIM)
        if rope_rows is not None:
            cos_t, sin_t = cos_ref[rope_rows, :], sin_ref[rope_rows, :]
            q, k = _rope(q, cos_t, sin_t, HEAD_DIM // 2), _rope(k, cos_t, sin_t, HEAD_DIM // 2)
        q_s[dst, :] = (q * scale).astype(BF16)
        k_s[:, dst] = jnp.transpose(k).astype(BF16)
        va_s[dst, :], vb_s[dst, :] = _ones_lane_variants(x_ref[rows, 2 * GROUP_W:3 * GROUP_W])

    prep(xc_ref, pl.ds(0, ctx), pl.ds(0, ctx), None)

    def prep_body(i, carry):
        r = pl.ds(pl.multiple_of(i * tq, tq), tq)
        prep(xl_ref, r, pl.ds(pl.multiple_of(ctx + i * tq, tq), tq), r)
        return carry

    lax.fori_loop(0, seq // tq, prep_body, 0)

    stages = [(_head_mask((1, GROUP_W), h, HEAD_DIM, BF16), *_head_v(h, va_s, vb_s),
               _head_mask((1, GROUP_W), h, HEAD_DIM, F32)) for h in range(N_HEADS)]
    finish = lambda acc: acc
    yc_ref[...] = _context_attention(q_s[0:ctx, :], k_s[:, 0:ctx], stages, finish).astype(yc_ref.dtype)
    _attention_stages(q_s, k_s, yl_ref, p_s, acc_s, stages, finish, ctx=ctx, seq=seq, tq=tq, unroll=unroll)


def _gqa_mixer(p_gqa, cos_t, sin_t, gq, gk, *, n_batch, seq, ctx):
    tq, unroll, lag = (GQA_TILING[k] for k in ("tq", "unroll", "lag"))
    lat_blocks = n_batch * seq // ctx
    kern = functools.partial(_gqa_kernel, seq=seq, ctx=ctx, tq=tq, unroll=unroll)
    yl, yc = pl.pallas_call(
        kern,
        grid=(n_batch,),
        in_specs=[pl.BlockSpec((seq, IN_GQA), lambda b: (b, 0)),
                  pl.BlockSpec((ctx, IN_GQA), lambda b: (lat_blocks + b, 0)),
                  _resident(cos_t.shape), _resident(sin_t.shape),
                  _resident((1, GROUP_W)), _resident((1, GROUP_W))],
        out_specs=[pl.BlockSpec((seq, GROUP_W), lambda b: (b, 0)),
                   pl.BlockSpec((ctx, GROUP_W), lambda b: (b, 0))],
        out_shape=[jax.ShapeDtypeStruct((n_batch * seq, GROUP_W), BF16),
                   jax.ShapeDtypeStruct((n_batch * ctx, GROUP_W), BF16)],
        scratch_shapes=[pltpu.VMEM((ctx + seq, GROUP_W), BF16), pltpu.VMEM((GROUP_W, ctx + seq), BF16)]
                       + [pltpu.VMEM((ctx + seq, GROUP_W), BF16)] * 2
                       + [pltpu.VMEM((lag, tq, ctx + seq), BF16), pltpu.VMEM((tq, GROUP_W), F32)],
        compiler_params=_cparams("parallel"),
        name="gqa_mixer",
    )(p_gqa, p_gqa, cos_t, sin_t, gq, gk)
    return yl, yc


def _diff_kernel(xl_ref, xc_ref, cos_ref, sin_ref, gq_ref, gk_ref, lam_ref, sub_ref, yl_ref, yc_ref,
                 q_s, k_s, va_s, vb_s, p_s, acc_s, *, seq, ctx, tq, unroll, lam_init):
    ones_sub = _group_ones(GROUP_W, DIFF_SUB)
    ones_head = _group_ones(GROUP_W, HEAD_DIM)
    scale = DIFF_SUB ** -0.5 * LOG2E
    lp = lam_ref[...]
    lam = (jnp.exp(jnp.sum(lp[0:1] * lp[1:2], axis=-1, keepdims=True))
           - jnp.exp(jnp.sum(lp[2:3] * lp[3:4], axis=-1, keepdims=True)) + lam_init)

    def prep(x_ref, rows, dst, rope_rows):
        q = _group_rmsnorm(x_ref[rows, 0:GROUP_W].astype(F32), gq_ref[...], ones_sub, DIFF_SUB)
        k = _group_rmsnorm(x_ref[rows, GROUP_W:2 * GROUP_W].astype(F32), gk_ref[...], ones_sub, DIFF_SUB)
        if rope_rows is not None:
            cos_t, sin_t = cos_ref[rope_rows, :], sin_ref[rope_rows, :]
            q, k = _rope(q, cos_t, sin_t, DIFF_SUB // 2), _rope(k, cos_t, sin_t, DIFF_SUB // 2)
        q_s[dst, :] = (q * scale).astype(BF16)
        k_s[:, dst] = jnp.transpose(k).astype(BF16)
        va_s[dst, :], vb_s[dst, :] = _ones_lane_variants(x_ref[rows, 2 * GROUP_W:3 * GROUP_W])

    prep(xc_ref, pl.ds(0, ctx), pl.ds(0, ctx), None)

    def prep_body(i, carry):
        r = pl.ds(pl.multiple_of(i * tq, tq), tq)
        prep(xl_ref, r, pl.ds(pl.multiple_of(ctx + i * tq, tq), tq), r)
        return carry

    lax.fori_loop(0, seq // tq, prep_body, 0)

    stages = []
    for h in range(N_HEADS):
        head = _head_mask((1, GROUP_W), h, HEAD_DIM, F32)
        for j, weight in enumerate((head, -lam * head)):
            stages.append((_head_mask((1, GROUP_W), 2 * h + j, DIFF_SUB, BF16), *_head_v(h, va_s, vb_s), weight))
    finish = lambda acc: _group_rmsnorm(acc, sub_ref[...], ones_head, HEAD_DIM) * (1.0 - lam_init)
    yc_ref[...] = _context_attention(q_s[0:ctx, :], k_s[:, 0:ctx], stages, finish).astype(yc_ref.dtype)
    _attention_stages(q_s, k_s, yl_ref, p_s, acc_s, stages, finish, ctx=ctx, seq=seq, tq=tq, unroll=unroll)


def _diff_mixer(p_df, cos_t, sin_t, gq, gk, lam_p, sub_g, *, lam_init, n_batch, seq, ctx):
    tq, unroll, lag = (DIFF_TILING[k] for k in ("tq", "unroll", "lag"))
    lat_blocks = n_batch * seq // ctx
    kern = functools.partial(_diff_kernel, seq=seq, ctx=ctx, tq=tq, unroll=unroll, lam_init=lam_init)
    yl, yc = pl.pallas_call(
        kern,
        grid=(n_batch,),
        in_specs=[pl.BlockSpec((seq, IN_DIFF), lambda b: (b, 0)),
                  pl.BlockSpec((ctx, IN_DIFF), lambda b: (lat_blocks + b, 0)),
                  _resident(cos_t.shape), _resident(sin_t.shape),
                  _resident((1, GROUP_W)), _resident((1, GROUP_W)),
                  _resident(lam_p.shape), _resident((1, GROUP_W))],
        out_specs=[pl.BlockSpec((seq, GROUP_W), lambda b: (b, 0)),
                   pl.BlockSpec((ctx, GROUP_W), lambda b: (b, 0))],
        out_shape=[jax.ShapeDtypeStruct((n_batch * seq, GROUP_W), BF16),
                   jax.ShapeDtypeStruct((n_batch * ctx, GROUP_W), BF16)],
        scratch_shapes=[pltpu.VMEM((ctx + seq, GROUP_W), BF16), pltpu.VMEM((GROUP_W, ctx + seq), BF16)]
                       + [pltpu.VMEM((ctx + seq, GROUP_W), BF16)] * 2
                       + [pltpu.VMEM((lag, tq, ctx + seq), BF16), pltpu.VMEM((tq, GROUP_W), F32)],
        compiler_params=_cparams("parallel"),
        name="diff_attention_mixer",
    )(p_df, p_df, cos_t, sin_t, gq, gk, lam_p, sub_g)
    return yl, yc


def _dot_split2(x, w_bf16):
    hi = x.astype(BF16)
    lo = (x - hi.astype(F32)).astype(BF16)
    return _dot(hi, w_bf16) + _dot(lo, w_bf16)


def _mlstm_kernel(xl_ref, xc_ref, gl_ref, gc_ref, cw_ref, cb_ref, gb_ref, ng_ref, yl_ref, yc_ref,
                  q_s, k_s, hsum, c_s, n_s, m_s, *, seq, ctx):
    t = ML_CHUNK
    n_chunks = seq // t
    ones_head = _group_ones(GROUP_W, HEAD_DIM)
    block_diag = ones_head.astype(F32)
    r_i = lax.broadcasted_iota(jnp.int32, (t, t), 0)
    c_i = lax.broadcasted_iota(jnp.int32, (t, t), 1)
    tril_b = jnp.where(c_i <= r_i, 1.0, 0.0).astype(BF16)
    causal = (c_i <= r_i, c_i >= r_i)
    row_g = lax.broadcasted_iota(jnp.int32, (t, IN_GATE), 0)
    sel_r = lax.broadcasted_iota(jnp.int32, (IN_GATE, GROUP_W), 0)
    sel_c = lax.broadcasted_iota(jnp.int32, (IN_GATE, GROUP_W), 1) // HEAD_DIM
    select = [jnp.where(sel_r == 8 * d + sel_c, 1.0, 0.0).astype(BF16) for d in range(2)]
    q_masks = [_head_mask((1, GROUP_W), h, HEAD_DIM, BF16) for h in range(N_HEADS)]

    for x_ref, rows, off in ((xc_ref, ctx, 0), (xl_ref, seq, ctx)):
        qk = _silu(_dwconv3(x_ref[:, 0:2 * GROUP_W].astype(F32), cw_ref[...], cb_ref[...]))
        q_s[off:off + rows, :] = (qk[:, 0:GROUP_W] * (HEAD_DIM ** -0.5)).astype(BF16)
        k_s[off:off + rows, :] = qk[:, GROUP_W:2 * GROUP_W].astype(BF16)

    hsum[...] = jnp.zeros_like(hsum)
    c_s[...] = jnp.zeros_like(c_s)
    n_s[...] = jnp.zeros_like(n_s)
    m_s[...] = jnp.zeros_like(m_s)

    def running_max(x, d):
        step = 1
        while step < t:
            if d == 0:
                shifted, valid = pltpu.roll(x, step, 0), row_g >= step
            else:
                shifted, valid = pltpu.roll(x, t - step, 0), row_g < t - step
            x = jnp.maximum(x, jnp.where(valid, shifted, -jnp.inf))
            step *= 2
        return x

    def chunk(d, rows, v, g_raw):
        g = g_raw + gb_ref[...]
        logf = pltpu.roll(jax.nn.log_sigmoid(g), IN_GATE - N_HEADS, 1)
        pre = _dot_exact_rhs(tril_b, logf)
        tot = pre[t - 1:t, :]
        cum = pre if d == 0 else tot - pre + logf
        r = g - cum
        m_prev = m_s[d]
        log_inter = cum + m_prev
        m_t = jnp.maximum(log_inter, cum + running_max(r, d))
        a_t = cum - m_t
        log_g = (tot - cum) + g
        m_new = jnp.maximum(tot + m_prev, jnp.max(log_g, axis=0, keepdims=True))
        m_s[d] = m_new
        rowwise = jnp.concatenate([jnp.exp(log_inter - m_t), jnp.exp(-m_t), jnp.exp(log_g - m_new),
                                   jnp.broadcast_to(jnp.exp(tot + m_prev - m_new), (8, IN_GATE))], axis=0)
        full = _dot_split2(rowwise, select[d])
        w_inter, e_m, w_g, w_c = full[0:t], full[t:2 * t], full[2 * t:3 * t], full[3 * t:3 * t + 1]

        q, k = q_s[rows, :], k_s[rows, :]
        r_t = jnp.transpose(r * LOG2E)
        a_2 = a_t * LOG2E
        pair = [None, None]
        for h in range(N_HEADS):
            col = 8 * d + h
            w_intra = jnp.exp2(jnp.where(causal[d], a_2[:, col:col + 1] + r_t[col:col + 1, :], -jnp.inf))
            s = (_dot_nt(q * q_masks[h], k) * w_intra).astype(BF16)
            o = _dot(s, v * q_masks[h] + q_masks[(h + 2) % N_HEADS])
            pair[h // 2] = o if pair[h // 2] is None else pair[h // 2] + o
        half = GROUP_W // 2
        num_i = jnp.concatenate([pair[0][:, :half], pair[1][:, half:]], axis=1)
        den_i = jnp.concatenate([pair[0][:, half:], pair[1][:, :half]], axis=1)
        ct, n_full = c_s[d], n_s[d]
        inter = _dot(q, ct.astype(BF16))
        qn = _dot((q.astype(F32) * n_full).astype(BF16), ones_head)
        num = w_inter * inter + num_i
        den = w_inter * qn + den_i
        kw = k.astype(F32) * w_g
        c_s[d] = w_c * ct + _dot_tn(kw.astype(BF16), v) * block_diag
        n_s[d] = w_c * n_full + jnp.sum(kw, axis=0, keepdims=True)
        return num / jnp.maximum(jnp.abs(den), e_m)

    assert ctx == t
    v_c = xc_ref[:, 2 * GROUP_W:3 * GROUP_W]
    hsum[0:t, :] = chunk(0, pl.ds(0, t), v_c, gc_ref[...]) + chunk(1, pl.ds(0, t), v_c, gc_ref[...])

    def body(j, carry):
        for d in range(2):
            cidx = j if d == 0 else n_chunks - 1 - j
            lat = pl.ds(pl.multiple_of(cidx * t, t), t)
            rows = pl.ds(pl.multiple_of(ctx + cidx * t, t), t)
            hsum[rows, :] += chunk(d, rows, xl_ref[lat, 2 * GROUP_W:3 * GROUP_W], gl_ref[lat, :])
        return carry

    lax.fori_loop(0, n_chunks, body, 0, unroll=True)

    for x_ref, y_ref, rows, off in ((xc_ref, yc_ref, ctx, 0), (xl_ref, yl_ref, seq, ctx)):
        hn = _group_rmsnorm(hsum[off:off + rows, :], ng_ref[...], ones_head, HEAD_DIM)
        y_ref[...] = (hn * jax.nn.sigmoid(x_ref[:, 3 * GROUP_W:4 * GROUP_W].astype(F32))).astype(y_ref.dtype)


def _mlstm_mixer(p_ml, p_gate, conv_w, conv_b, gate_b, norm_g, *, n_batch, seq, ctx):
    lat_blocks = n_batch * seq // ctx
    kern = functools.partial(_mlstm_kernel, seq=seq, ctx=ctx)
    yl, yc = pl.pallas_call(
        kern,
        grid=(n_batch,),
        in_specs=[pl.BlockSpec((seq, IN_ML), lambda b: (b, 0)),
                  pl.BlockSpec((ctx, IN_ML), lambda b: (lat_blocks + b, 0)),
                  pl.BlockSpec((seq, IN_GATE), lambda b: (b, 0)),
                  pl.BlockSpec((ctx, IN_GATE), lambda b: (lat_blocks + b, 0)),
                  _resident(conv_w.shape), _resident(conv_b.shape), _resident(gate_b.shape), _resident(norm_g.shape)],
        out_specs=[pl.BlockSpec((seq, GROUP_W), lambda b: (b, 0)),
                   pl.BlockSpec((ctx, GROUP_W), lambda b: (b, 0))],
        out_shape=[jax.ShapeDtypeStruct((n_batch * seq, GROUP_W), BF16),
                   jax.ShapeDtypeStruct((n_batch * ctx, GROUP_W), BF16)],
        scratch_shapes=[pltpu.VMEM((ctx + seq, GROUP_W), BF16), pltpu.VMEM((ctx + seq, GROUP_W), BF16),
                        pltpu.VMEM((ctx + seq, GROUP_W), F32),
                        pltpu.VMEM((2, GROUP_W, GROUP_W), F32), pltpu.VMEM((2, 1, GROUP_W), F32),
                        pltpu.VMEM((2, 1, IN_GATE), F32)],
        compiler_params=_cparams("parallel"),
        name="mlstm_mixer",
    )(p_ml, p_ml, p_gate, p_gate, conv_w, conv_b, gate_b, norm_g)
    return yl, yc


def _phase_tables(idx_a, idx_b, n):
    ang = (2.0 * math.pi / n) * ((idx_a[:, None] * idx_b[None, :]) % n).astype(F32)
    return jnp.cos(ang), jnp.sin(ang)


def _dft_matrix(length, n):
    nf, lo = n // 2, 64
    k = jnp.arange(nf, dtype=jnp.int32)
    k1, k0 = lo * jnp.arange(nf // lo, dtype=jnp.int32), jnp.arange(lo, dtype=jnp.int32)
    j = jnp.arange(length, dtype=jnp.int32)
    (ca, sa), (cb, sb) = _phase_tables(k1, j, n), _phase_tables(k0, j, n)
    cos_f = (ca[:, None, :] * cb[None, :, :] - sa[:, None, :] * sb[None, :, :]).reshape(nf, length)
    sin_f = (sa[:, None, :] * cb[None, :, :] + ca[:, None, :] * sb[None, :, :]).reshape(nf, length)
    alt = jnp.where(j % 2 == 0, 1.0, -1.0).astype(F32)
    return jnp.concatenate([cos_f, jnp.where(k[:, None] == 0, alt[None, :], sin_f)], axis=0).astype(BF16)


def _hyena_filter_consts(length, n):
    t = jnp.arange(length, dtype=F32)
    tn = t / length
    bands = jnp.arange(1, HY_POS_BANDS + 1, dtype=F32)
    ang = 2.0 * math.pi * tn[:, None] * bands
    feats = jnp.concatenate([tn[:, None], jnp.cos(ang), jnp.sin(ang)], axis=-1)
    feats = jnp.pad(feats, ((0, 0), (0, 128 - feats.shape[1])))
    dist = jnp.abs(t - length // 2) / (length / 2)
    deltas = jnp.abs(jnp.linspace(math.log(HY_DECAY_TARGET) / HY_SLOW_DECAY,
                                  math.log(HY_DECAY_TARGET) / HY_FAST_DECAY, GROUP_W, dtype=F32))
    window = jnp.exp(-dist[:, None] * jnp.tile(deltas, 2))
    k = jnp.arange(n // 2, dtype=jnp.int32)
    phi = (2.0 * math.pi / n) * ((k * (length // 2)) % n).astype(F32)
    shift = jnp.stack([jnp.cos(phi), jnp.sin(phi)], axis=-1) * (2.0 / n)
    return feats, window, shift


def _filter_kernel(feats_ref, win_ref, shift_ref, w1lo_ref, w1hi_ref, b1_ref, w2_ref, b2_ref, w3t_ref, w3b_ref, b3_ref,
                   f_ref, al_ref, be_ref, ga_ref, hid_s, *, nyquist_sign):
    half = feats_ref.shape[0] // 2

    @pl.when(pl.program_id(1) == 0)
    def _():
        pre = (_dot_f32ish(feats_ref[0:half, :], w1lo_ref[0]) + _dot_f32ish(feats_ref[half:2 * half, :], w1hi_ref[0]))
        h = jnp.sin(pre + b1_ref[0])
        hid_s[...] = jnp.sin(_dot_f32ish(h, w2_ref[0]) + b2_ref[0])

    hid = hid_s[...]
    h = jnp.concatenate([_dot_f32ish(hid, w3t_ref[0]), _dot_f32ish(hid, w3b_ref[0])], axis=0)
    h = (h + b3_ref[0]) * win_ref[...]
    h = h / jnp.sum(jnp.abs(h), axis=0, keepdims=True)
    h1 = h.astype(BF16)
    h2 = (h - h1.astype(F32)).astype(BF16)
    nf = f_ref.shape[0] // 2
    step = math.gcd(nf, 512)
    for r in range(0, nf, step):
        fc, fs = f_ref[r:r + step, :], f_ref[nf + r:nf + r + step, :]
        p = _dot(fc, h1) + _dot(fc, h2)
        q = _dot(fs, h1) + _dot(fs, h2)
        cs, sn = shift_ref[r:r + step, 0:1], shift_ref[r:r + step, 1:2]
        alpha = p * cs + q * sn
        beta = p * sn - q * cs
        if r == 0:
            first = lax.broadcasted_iota(jnp.int32, p.shape, 0) == 0
            al_ref[0, r:r + step, :] = jnp.where(first, 0.5 * alpha, alpha)
            be_ref[0, r:r + step, :] = jnp.where(first, 0.0, beta)
            ga_ref[0, r:r + step, :] = jnp.where(first, (0.5 * nyquist_sign) * q * cs, alpha)
        else:
            al_ref[0, r:r + step, :] = alpha
            be_ref[0, r:r + step, :] = beta
            ga_ref[0, r:r + step, :] = alpha


def _hyena_filter_spectra(length, fwd, w1, b1, w2, b2, w3, b3):
    depth, hid = w1.shape[0], w1.shape[2]
    n = fwd.shape[0]
    feats, window, shift = _hyena_filter_consts(length, n)
    nf = n // 2
    place = lambda a, r0, c0, rows, cols: jnp.pad(a, ((0, 0), (r0, rows - r0 - a.shape[1]), (c0, cols - c0 - a.shape[2])))
    w1lo, w1hi = place(w1, 0, 0, 128, 128), place(w1, 0, hid, 128, 128)
    w2bd = place(w2, 0, 0, 128, 128) + place(w2, hid, hid, 128, 128)
    w3t, w3b = place(w3, 0, 0, 128, w3.shape[2]), place(w3, hid, 0, 128, w3.shape[2])
    b1p, b2p = jnp.tile(b1, (1, 2))[:, None, :], jnp.tile(b2, (1, 2))[:, None, :]
    b3p = b3[:, None, :]
    lay = lambda a: pl.BlockSpec((1,) + a.shape[1:], lambda l, o: (l, 0, 0))
    per_order = lambda rows: pl.BlockSpec((1, rows, GROUP_W), lambda l, o: (l, 0, o))
    out = jax.ShapeDtypeStruct((depth, nf, 2 * GROUP_W), F32)
    kern = functools.partial(_filter_kernel, nyquist_sign=1.0 if (length // 2) % 2 == 0 else -1.0)
    return pl.pallas_call(
        kern,
        grid=(depth, 2),
        in_specs=[_resident(feats.shape), pl.BlockSpec((length, GROUP_W), lambda l, o: (0, o)), _resident(shift.shape),
                  lay(w1lo), lay(w1hi), lay(b1p), lay(w2bd), lay(b2p), per_order(128), per_order(128), per_order(1),
                  _resident(fwd.shape)],
        out_specs=[per_order(nf)] * 3,
        out_shape=[out, out, out],
        scratch_shapes=[pltpu.VMEM((length // 2, 128), F32)],
        compiler_params=_cparams("parallel", "arbitrary"),
        name="hyena_filter_spectra",
    )(feats, window, shift, w1lo, w1hi, b1p, w2bd, b2p, w3t, w3b, b3p, fwd)


def _hyena_conv_kernel(sig_ref, gate_ref, cws_ref, cbs_ref, cwg_ref, cbg_ref, skip_ref, f_ref,
                       al_ref, be_ref, ga_ref, o_ref, *, sig_conv):
    s = sig_ref[...].astype(F32)
    if sig_conv:
        s = _dwconv3(s, cws_ref[...], cbs_ref[...])
    gate = _dwconv3(gate_ref[...].astype(F32), cwg_ref[...], cbg_ref[...])
    sb = s.astype(BF16)
    nf = f_ref.shape[0] // 2
    a = _dot(f_ref[0:nf, :], sb)
    b = _dot(f_ref[nf:2 * nf, :], sb)
    ay = a * al_ref[0] + b * be_ref[0]
    by = b * ga_ref[0] - a * be_ref[0]
    y = _dot_tn(f_ref[0:nf, :], ay.astype(BF16)) + _dot_tn(f_ref[nf:2 * nf, :], by.astype(BF16))
    o_ref[...] = (gate * (y + skip_ref[...] * s)).astype(o_ref.dtype)


def _hyena_conv(sig, sig_block, sig_col, p_hy, gate_col, conv_w, conv_b, skip, fwd, spectra, order, layer,
                *, rows, first_block, n_batch, out_dtype, sig_conv):
    cw = lambda col: pl.BlockSpec((3, GROUP_W), lambda b: (0, col))
    cb = lambda col: pl.BlockSpec((1, GROUP_W), lambda b: (0, col))
    spec = pl.BlockSpec((1, fwd.shape[0] // 2, GROUP_W), lambda b: (layer, 0, order), pipeline_mode=pl.Buffered(1))
    kern = functools.partial(_hyena_conv_kernel, sig_conv=sig_conv)
    return pl.pallas_call(
        kern,
        grid=(n_batch,),
        in_specs=[pl.BlockSpec((rows, GROUP_W), lambda b: (sig_block + b, sig_col)),
                  pl.BlockSpec((rows, GROUP_W), lambda b: (first_block + b, gate_col)),
                  cw(sig_col), cb(sig_col), cw(gate_col), cb(gate_col),
                  pl.BlockSpec((1, GROUP_W), lambda b: (0, 0)),
                  _resident(fwd.shape), spec, spec, spec],
        out_specs=pl.BlockSpec((rows, GROUP_W), lambda b: (b, 0)),
        out_shape=jax.ShapeDtypeStruct((n_batch * rows, GROUP_W), out_dtype),
        compiler_params=_cparams("parallel"),
        name="hyena_long_conv",
    )(sig, p_hy, conv_w, conv_b, conv_w, conv_b, skip, fwd, *spectra)


def _hyena_mixer(p_hy, conv_w, conv_b, skip, fwd, spectra, layer, *, rows, first_block, n_batch):
    common = dict(rows=rows, first_block=first_block, n_batch=n_batch)
    z = _hyena_conv(p_hy, first_block, 0, p_hy, 1, conv_w, conv_b, skip[0:1], fwd, spectra, 0, layer,
                    out_dtype=F32, sig_conv=True, **common)
    return _hyena_conv(z, 0, 0, p_hy, 2, conv_w, conv_b, skip[1:2], fwd, spectra, 1, layer,
                       out_dtype=BF16, sig_conv=False, **common)


def _rope_tables(length, dim, reps):
    rows = length // GRID_W
    row = jnp.repeat(jnp.arange(rows), GRID_W).astype(F32)
    col = jnp.tile(jnp.arange(GRID_W), rows).astype(F32)
    n_freq = dim // 4
    inv = ROPE_THETA ** (-jnp.arange(n_freq, dtype=F32) / n_freq)
    ang = jnp.concatenate([row[:, None] * inv, col[:, None] * inv], axis=-1)
    cos, sin = jnp.cos(ang), jnp.sin(ang)
    return (jnp.tile(jnp.concatenate([cos, cos], axis=-1), (1, reps)),
            jnp.tile(jnp.concatenate([-sin, sin], axis=-1), (1, reps)))


def kernel(x, c, ctx, c_ctx, ada_w, ada_b, norm_g, ffn_w13, ffn_w2, w_in, w_out, ml_gate_b, ml_conv_w, ml_conv_b,
           ml_norm_g, gqa_qk_g, hy_conv_w, hy_conv_b, hy_filt_w1, hy_filt_b1, hy_filt_w2, hy_filt_b2, hy_filt_w3,
           hy_filt_b3, hy_skip, diff_qk_g, diff_lambda, diff_subln_g):
    n_batch, seq, d = x.shape
    n_ctx = ctx.shape[1]
    depth = ada_w.shape[0]
    n_lat = n_batch * seq
    n_all = n_lat + n_batch * n_ctx
    assert n_batch * n_ctx == seq and n_ctx == ML_CHUNK and seq % ROW_TILE == 0

    c_all = jnp.pad(jnp.concatenate([c, c_ctx[None]], axis=0), ((0, 16 - n_batch - 1), (0, 0)))
    mod = _modulation(c_all, ada_w, ada_b)[:, :n_batch + 1].reshape(depth, n_batch + 1, N_MOD, d)

    w13 = ffn_w13.astype(BF16)
    w2 = ffn_w2.astype(BF16)
    w_out_b = w_out.astype(BF16)

    rope_gqa = _rope_tables(seq, HEAD_DIM, N_HEADS)
    rope_diff = _rope_tables(seq, DIFF_SUB, 2 * N_HEADS)
    dft_l = _dft_matrix(seq, 3 * seq // 2)
    dft_c = _dft_matrix(n_ctx, 2 * n_ctx)
    filt = (hy_filt_w1, hy_filt_b1, hy_filt_w2, hy_filt_b2, hy_filt_w3, hy_filt_b3)
    spec_l = _hyena_filter_spectra(seq, dft_l, *filt)
    spec_c = _hyena_filter_spectra(n_ctx, dft_c, *filt)

    h = (x.reshape(n_lat, d), ctx.reshape(n_batch * n_ctx, d))
    dims = dict(seq=seq, n_batch=n_batch)
    for l in range(depth):
        need_ctx = l < depth - 1
        lam_init = 0.8 - 0.6 * math.exp(-0.3 * l)
        h = _ffn_half(h, mod[l], norm_g[l, 0], w13, w2, layer=l, half=0, base=0, n_rows=n_all, **dims)
        p_ml, p_gqa, p_hy, p_df, p_gate = _in_projection(h, mod[l], norm_g[l, 1], w_in, layer=l, **dims)

        gate_b = jnp.pad(ml_gate_b[l], (0, IN_GATE - ml_gate_b.shape[1]))[None]
        y_ml = _mlstm_mixer(p_ml, p_gate, ml_conv_w[l], ml_conv_b[l][None], gate_b, ml_norm_g[l][None],
                            n_batch=n_batch, seq=seq, ctx=n_ctx)
        tile_g = lambda g, reps: jnp.tile(g, reps)[None]
        y_gqa = _gqa_mixer(p_gqa, *rope_gqa, tile_g(gqa_qk_g[l, 0], N_HEADS), tile_g(gqa_qk_g[l, 1], N_HEADS),
                           n_batch=n_batch, seq=seq, ctx=n_ctx)
        y_df = _diff_mixer(p_df, *rope_diff, tile_g(diff_qk_g[l, 0], 2 * N_HEADS), tile_g(diff_qk_g[l, 1], 2 * N_HEADS),
                           diff_lambda[l], tile_g(diff_subln_g[l], N_HEADS),
                           lam_init=lam_init, n_batch=n_batch, seq=seq, ctx=n_ctx)
        y_hy = _hyena_mixer(p_hy, hy_conv_w[l], hy_conv_b[l][None], hy_skip[l], dft_l, spec_l, l,
                            rows=seq, first_block=0, n_batch=n_batch)
        y_hy_c = y_hy if not need_ctx else _hyena_mixer(
            p_hy, hy_conv_w[l], hy_conv_b[l][None], hy_skip[l], dft_c, spec_c, l,
            rows=n_ctx, first_block=n_lat // n_ctx, n_batch=n_batch)
        h = _ffn_half(h, mod[l], norm_g[l, 2], w13, w2, layer=l, half=1, base=6,
                      n_rows=n_all if need_ctx else n_lat,
                      mix=[y_ml, y_gqa, (y_hy, y_hy_c), y_df], w_out=w_out_b, **dims)
    return h[:n_lat].reshape(n_batch, seq, d)
```

```python
import functools
import math

import jax
import jax.numpy as jnp
from jax import lax
from jax.experimental import pallas as pl
from jax.experimental.pallas import tpu as pltpu

F32 = jnp.float32
BF16 = jnp.bfloat16

EPS = 1e-6
ROPE_THETA = 10000.0
GRID_W = 64
N_MOD = 9
GROUP_W = 256
HEAD_DIM = 64
N_HEADS = 4
DIFF_SUB = 32
ML_CHUNK = 256
HY_POS_BANDS = 16
HY_DECAY_TARGET = 1e-2
HY_FAST_DECAY = 0.3
HY_SLOW_DECAY = 1.5
ROW_TILE = 1024
FFN_CHUNK = 256
VMEM_LIMIT = 56 * 1024 * 1024


def _cparams(*sem):
    return pltpu.CompilerParams(dimension_semantics=sem, vmem_limit_bytes=VMEM_LIMIT)


def _resident(shape):
    nd = len(shape)
    return pl.BlockSpec(shape, lambda *_: (0,) * nd, pipeline_mode=pl.Buffered(1))


def _resident_slice(arr, index):
    tail = arr.shape[len(index):]
    return pl.BlockSpec((None,) * len(index) + tail, lambda *_: tuple(index) + (0,) * len(tail),
                        pipeline_mode=pl.Buffered(1))


def _dot(a, b):
    return jnp.dot(a, b, preferred_element_type=F32)


def _dot_nt(a, b):
    return lax.dot_general(a, b, (((1,), (1,)), ((), ())), preferred_element_type=F32)


def _dot_tn(a, b):
    return lax.dot_general(a, b, (((0,), (0,)), ((), ())), preferred_element_type=F32)


def _split3(x):
    x1 = x.astype(BF16)
    r = x - x1.astype(F32)
    x2 = r.astype(BF16)
    x3 = (r - x2.astype(F32)).astype(BF16)
    return x1, x2, x3


def _dot_exact_rhs(a_bf16, x):
    x1, x2, x3 = _split3(x)
    return _dot(a_bf16, x1) + _dot(a_bf16, x2) + _dot(a_bf16, x3)


def _dot_f32ish(a, b):
    a1 = a.astype(BF16)
    a2 = (a - a1.astype(F32)).astype(BF16)
    b1 = b.astype(BF16)
    b2 = (b - b1.astype(F32)).astype(BF16)
    return _dot(a1, b1) + _dot(a1, b2) + _dot(a2, b1)


def _silu(x):
    return x * jax.nn.sigmoid(x)


def _modnorm(x, g, shift, scale):
    y = x * lax.rsqrt(jnp.mean(x * x, axis=-1, keepdims=True) + EPS)
    return (y * g) * (1.0 + scale) + shift


def _lane_iota(shape):
    return lax.broadcasted_iota(jnp.int32, shape, len(shape) - 1)


def _group_ones(n, group):
    r = lax.broadcasted_iota(jnp.int32, (n, n), 0) // group
    c = lax.broadcasted_iota(jnp.int32, (n, n), 1) // group
    return jnp.where(r == c, 1.0, 0.0).astype(BF16)


def _group_rmsnorm(x, gain, ones_g, group):
    x2 = x * x
    hi = x2.astype(BF16)
    lo = (x2 - hi.astype(F32)).astype(BF16)
    ms = (_dot(hi, ones_g) + _dot(lo, ones_g)) * (1.0 / group)
    return x * lax.rsqrt(ms + EPS) * gain


def _rope(x, cos_t, sin_t, half):
    n = x.shape[-1]
    left = pltpu.roll(x, n - half, 1)
    right = pltpu.roll(x, half, 1)
    sw = jnp.where((_lane_iota(x.shape) & (2 * half - 1)) < half, left, right)
    return x * cos_t + sw * sin_t


def _dwconv3(x, w, b):
    rows = x.shape[0]
    row = lax.broadcasted_iota(jnp.int32, x.shape, 0)
    xm = jnp.where(row == 0, 0.0, pltpu.roll(x, 1, 0))
    xp = jnp.where(row == rows - 1, 0.0, pltpu.roll(x, rows - 1, 0))
    return xm * w[0:1] + x * w[1:2] + xp * w[2:3] + b


def _head_mask(shape, h, width, dtype):
    lane = _lane_iota(shape)
    return jnp.where((lane >= h * width) & (lane < (h + 1) * width), 1.0, 0.0).astype(dtype)


def _mod_kernel(c_ref, w_ref, b_ref, o_ref):
    sc = _silu(c_ref[...]).astype(BF16)
    o_ref[0] = _dot(sc, w_ref[0].astype(BF16)) + b_ref[0]


def _modulation(c_all, ada_w, ada_b):
    depth, d, nmod = ada_w.shape
    rows = c_all.shape[0]
    tn = 1536
    return pl.pallas_call(
        _mod_kernel,
        grid=(depth, nmod // tn),
        in_specs=[pl.BlockSpec((rows, d), lambda l, j: (0, 0)),
                  pl.BlockSpec((1, d, tn), lambda l, j: (l, 0, j)),
                  pl.BlockSpec((1, 1, tn), lambda l, j: (l, 0, j))],
        out_specs=pl.BlockSpec((1, rows, tn), lambda l, j: (l, 0, j)),
        out_shape=jax.ShapeDtypeStruct((depth, rows, nmod), F32),
        compiler_params=_cparams("parallel", "parallel"),
        name="adaln_modulation",
    )(c_all, ada_w, ada_b.reshape(depth, 1, nmod))


def _row_specs(width, tile, n_lat_tiles):
    return [pl.BlockSpec((tile, width), lambda i: (jnp.minimum(i, n_lat_tiles - 1), 0)),
            pl.BlockSpec((tile, width), lambda i: (jnp.maximum(i - n_lat_tiles, 0), 0))]


def _pick_rows(lat_ref, ctx_ref, n_lat_tiles):
    return jnp.where(pl.program_id(0) < n_lat_tiles, lat_ref[...], ctx_ref[...])


def _ffn_kernel(*refs, base, hidden, chunk, n_lat_tiles, split_x, n_mix):
    refs = list(refs)
    x = _pick_rows(refs.pop(0), refs.pop(0), n_lat_tiles) if split_x else refs.pop(0)[...]
    mod_ref, g_ref, w13_ref, w2_ref = refs[:4]
    mix_refs, (o_ref, acc_ref) = refs[4:-2], refs[-2:]
    if n_mix:
        wo_ref = mix_refs[-1]
        mixed = None
        for k in range(n_mix):
            y = _pick_rows(mix_refs[2 * k], mix_refs[2 * k + 1], n_lat_tiles)
            part = _dot(y, wo_ref[k * GROUP_W:(k + 1) * GROUP_W, :])
            mixed = part if mixed is None else mixed + part
        x = x + mod_ref[0, 5:6] * mixed
    shift, scale, gate = mod_ref[0, base:base + 1], mod_ref[0, base + 1:base + 2], mod_ref[0, base + 2:base + 3]
    xn = _modnorm(x, g_ref[...], shift, scale).astype(BF16)
    for start in range(0, hidden, chunk):
        size = min(chunk, hidden - start)
        a = _dot(xn, w13_ref[:, start:start + size])
        b = _dot(xn, w13_ref[:, hidden + start:hidden + start + size])
        part = _dot((_silu(a) * b).astype(BF16), w2_ref[start:start + size, :])
        if start == 0:
            acc_ref[...] = part
        else:
            acc_ref[...] += part
    o_ref[...] = x + (0.5 * gate) * acc_ref[...]


def _mod_index(tile, seq, n_batch):
    return lambda i: (jnp.minimum((i * tile) // seq, n_batch), 0, 0)


def _ffn_half(h, mod, g, w13, w2, *, layer, half, base, n_rows, seq, n_batch, mix=None, w_out=None):
    split_x = isinstance(h, tuple)
    d = w13.shape[-2]
    hidden = w2.shape[-2]
    tm = ROW_TILE
    n_lat_tiles = n_batch * seq // tm
    kern = functools.partial(_ffn_kernel, base=base, hidden=hidden, chunk=FFN_CHUNK, n_lat_tiles=n_lat_tiles,
                             split_x=split_x, n_mix=len(mix) if mix else 0)
    x_specs = _row_specs(d, tm, n_lat_tiles) if split_x else [pl.BlockSpec((tm, d), lambda i: (i, 0))]
    mix_args, mix_specs = [], []
    for pair in mix or ():
        mix_args += list(pair)
        mix_specs += _row_specs(GROUP_W, tm, n_lat_tiles)
    if mix:
        mix_args.append(w_out)
        mix_specs.append(_resident_slice(w_out, (layer,)))
    return pl.pallas_call(
        kern,
        grid=(n_rows // tm,),
        in_specs=x_specs + [pl.BlockSpec((1, N_MOD, d), _mod_index(tm, seq, n_batch)),
                            _resident((1, d)), _resident_slice(w13, (layer, half)),
                            _resident_slice(w2, (layer, half))] + mix_specs,
        out_specs=pl.BlockSpec((tm, d), lambda i: (i, 0)),
        out_shape=jax.ShapeDtypeStruct((n_rows, d), F32),
        scratch_shapes=[pltpu.VMEM((tm, d), F32)],
        compiler_params=_cparams("parallel"),
        name="swiglu_half_step",
    )(*(h if split_x else (h,)), mod, g.reshape(1, d), w13, w2, *mix_args)


IN_ML, IN_GQA, IN_HY, IN_DIFF, IN_GATE = 1024, 512, 768, 768, 128


def _in_weight_pieces():
    ml, n_gate = 4 * GROUP_W, 16
    o_gqa = ml + n_gate
    o_hy = o_gqa + 2 * GROUP_W
    o_df = o_hy + 3 * GROUP_W
    return [(0, 0, ml), (IN_ML, o_gqa, IN_GQA), (IN_ML + IN_GQA, o_hy, IN_HY), (IN_ML + IN_GQA + IN_HY, o_df, IN_DIFF),
            (IN_ML + IN_GQA + IN_HY + IN_DIFF, ml, n_gate)]


def _inproj_kernel(x_ref, mod_ref, g_ref, w_ref, ml_ref, gqa_ref, hy_ref, df_ref, gate_ref, wb_s):
    @pl.when(pl.program_id(0) == 0)
    def _():
        wb_s[:, IN_ML + IN_GQA + IN_HY + IN_DIFF:] = jnp.zeros((wb_s.shape[0], IN_GATE), BF16)
        for dst, src, width in _in_weight_pieces():
            wb_s[:, dst:dst + width] = w_ref[:, src:src + width].astype(BF16)

    xn = _modnorm(x_ref[...], g_ref[...], mod_ref[0, 3:4], mod_ref[0, 4:5]).astype(BF16)
    off = 0
    for ref, width in ((ml_ref, IN_ML), (gqa_ref, IN_GQA), (hy_ref, IN_HY), (df_ref, IN_DIFF), (gate_ref, IN_GATE)):
        ref[...] = _dot(xn, wb_s[:, off:off + width]).astype(ref.dtype)
        off += width


def _in_projection(h, mod, g, w, *, layer, seq, n_batch):
    n_rows, d = h.shape
    tm = ROW_TILE
    widths = (IN_ML, IN_GQA, IN_HY, IN_DIFF, IN_GATE)
    dtypes = (BF16, BF16, BF16, BF16, F32)
    return pl.pallas_call(
        _inproj_kernel,
        grid=(n_rows // tm,),
        in_specs=[pl.BlockSpec((tm, d), lambda i: (i, 0)),
                  pl.BlockSpec((1, N_MOD, d), _mod_index(tm, seq, n_batch)),
                  _resident((1, d)), _resident_slice(w, (layer,))],
        out_specs=[pl.BlockSpec((tm, wd), lambda i: (i, 0)) for wd in widths],
        out_shape=[jax.ShapeDtypeStruct((n_rows, wd), dt) for wd, dt in zip(widths, dtypes)],
        scratch_shapes=[pltpu.VMEM((d, sum(widths)), BF16)],
        compiler_params=_cparams("arbitrary"),
        name="input_projection",
    )(h, mod, g.reshape(1, d), w)


LOG2E = 1.4426950408889634
GQA_TILING = dict(tq=512, unroll=4, lag=1)
DIFF_TILING = dict(tq=256, unroll=2, lag=2)


def _ones_lane_variants(v):
    grp = _lane_iota((1, GROUP_W)) // HEAD_DIM
    odd = jnp.where(grp % 2 == 1, 1.0, 0.0).astype(BF16)
    even = jnp.where(grp % 2 == 0, 1.0, 0.0).astype(BF16)
    return v * even + odd, v * odd + even


def _scores_to_probs(qm, k_t):
    s = _dot(qm, k_t)
    return jnp.exp2(s - jnp.max(s, axis=-1, keepdims=True)).astype(BF16)


def _probs_times_v(p, v_aug, sum_lane):
    o = _dot(p, v_aug)
    return o * (1.0 / o[:, sum_lane:sum_lane + 1])


def _head_v(h, va, vb):
    return (va, (h + 1) * HEAD_DIM) if h % 2 == 0 else (vb, (h - 1) * HEAD_DIM)


def _attention_stages(q_s, k_s, y_ref, p_s, acc_s, stages, finish, *, ctx, seq, tq, unroll):
    n_keys = ctx + seq

    def probs(q, st):
        return _scores_to_probs(q * st[0], k_s[:, 0:n_keys])

    def out(p, st):
        return _probs_times_v(p, st[1][0:n_keys, :], st[2]) * st[3]

    p_s[...] = jnp.ones_like(p_s)
    acc_s[...] = jnp.zeros_like(acc_s)
    n, lag = len(stages), p_s.shape[0]

    def tail():
        acc = acc_s[...]
        for c in range(lag):
            acc = acc + out(p_s[c], stages[n - lag + c])
        return finish(acc)

    def body(i, carry):
        q = q_s[pl.ds(pl.multiple_of(ctx + i * tq, tq), tq), :]
        prev = jnp.maximum(i - 1, 0)
        y_ref[pl.ds(pl.multiple_of(prev * tq, tq), tq), :] = tail().astype(y_ref.dtype)
        ps, acc = [], None
        for j in range(n):
            ps.append(probs(q, stages[j]))
            if j >= lag:
                o = out(ps[j - lag], stages[j - lag])
                acc = o if acc is None else acc + o
        for c in range(lag):
            p_s[c] = ps[n - lag + c]
        acc_s[...] = acc
        return carry

    n_tiles = seq // tq
    lax.fori_loop(0, n_tiles, body, 0, unroll=unroll)
    y_ref[(n_tiles - 1) * tq:n_tiles * tq, :] = tail().astype(y_ref.dtype)


def _context_attention(q, k_t, stages, finish):
    acc = None
    for st in stages:
        o = _probs_times_v(_scores_to_probs(q * st[0], k_t), st[1][0:k_t.shape[1], :], st[2]) * st[3]
        acc = o if acc is None else acc + o
    return finish(acc)


GQA_OUT_HEADS = (0, 2, 1, 3)


def _gqa_kernel(xl_ref, xc_ref, cos_ref, sin_ref, gq_ref, gk_ref, yl_ref, yc_ref, q_s, k_s, va_s, vb_s, p_s, acc_s,
                *, seq, ctx, tq, unroll):
    kv_w = 2 * HEAD_DIM
    ones_q = _group_ones(GROUP_W, HEAD_DIM)
    ones_k = _group_ones(kv_w, HEAD_DIM)
    scale = HEAD_DIM ** -0.5 * LOG2E
    ones_v = jnp.ones((1, kv_w), BF16)

    def prep(x_ref, rows, dst, rope_rows):
        n_rows = rows.size
        q = _group_rmsnorm(x_ref[rows, 0:GROUP_W].astype(F32), gq_ref[...], ones_q, HEAD_DIM)
        k = _group_rmsnorm(x_ref[rows, GROUP_W:GROUP_W + kv_w].astype(F32), gk_ref[:, 0:kv_w], ones_k, HEAD_DIM)
        if rope_rows is not None:
            cos_t, sin_t = cos_ref[rope_rows, :], sin_ref[rope_rows, :]
            q = _rope(q, cos_t, sin_t, HEAD_DIM // 2)
            k = _rope(k, cos_t[:, 0:kv_w], sin_t[:, 0:kv_w], HEAD_DIM // 2)
        q_s[dst, :] = (q * scale).astype(BF16)
        k_t = jnp.transpose(k).astype(BF16)
        for h in range(N_HEADS):
            k_s[h * HEAD_DIM:(h + 1) * HEAD_DIM, dst] = k_t[(h // 2) * HEAD_DIM:(h // 2 + 1) * HEAD_DIM, :]
        v = x_ref[rows, GROUP_W + kv_w:GROUP_W + 2 * kv_w]
        ones = jnp.broadcast_to(ones_v, (n_rows, kv_w))
        va_s[dst, :] = jnp.concatenate([v, ones], axis=1)
        vb_s[dst, :] = jnp.concatenate([ones, v], axis=1)

    prep(xc_ref, pl.ds(0, ctx), pl.ds(0, ctx), None)

    def prep_body(i, carry):
        r = pl.ds(pl.multiple_of(i * tq, tq), tq)
        prep(xl_ref, r, pl.ds(pl.multiple_of(ctx + i * tq, tq), tq), r)
        return carry

    lax.fori_loop(0, seq // tq, prep_body, 0)

    stages = []
    for h in range(N_HEADS):
        group = GQA_OUT_HEADS.index(h)
        v_ref, sum_lane = (va_s, kv_w) if group < 2 else (vb_s, 0)
        stages.append((_head_mask((1, GROUP_W), h, HEAD_DIM, BF16), v_ref, sum_lane,
                       _head_mask((1, GROUP_W), group, HEAD_DIM, F32)))
    finish = lambda acc: acc
    yc_ref[...] = _context_attention(q_s[0:ctx, :], k_s[:, 0:ctx], stages, finish).astype(yc_ref.dtype)
    _attention_stages(q_s, k_s, yl_ref, p_s, acc_s, stages, finish, ctx=ctx, seq=seq, tq=tq, unroll=unroll)


def _gqa_mixer(p_gqa, cos_t, sin_t, gq, gk, *, n_batch, seq, ctx):
    tq, unroll, lag = (GQA_TILING[k] for k in ("tq", "unroll", "lag"))
    lat_blocks = n_batch * seq // ctx
    kern = functools.partial(_gqa_kernel, seq=seq, ctx=ctx, tq=tq, unroll=unroll)
    yl, yc = pl.pallas_call(
        kern,
        grid=(n_batch,),
        in_specs=[pl.BlockSpec((seq, IN_GQA), lambda b: (b, 0)),
                  pl.BlockSpec((ctx, IN_GQA), lambda b: (lat_blocks + b, 0)),
                  _resident(cos_t.shape), _resident(sin_t.shape),
                  _resident((1, GROUP_W)), _resident((1, GROUP_W))],
        out_specs=[pl.BlockSpec((seq, GROUP_W), lambda b: (b, 0)),
                   pl.BlockSpec((ctx, GROUP_W), lambda b: (b, 0))],
        out_shape=[jax.ShapeDtypeStruct((n_batch * seq, GROUP_W), BF16),
                   jax.ShapeDtypeStruct((n_batch * ctx, GROUP_W), BF16)],
        scratch_shapes=[pltpu.VMEM((ctx + seq, GROUP_W), BF16), pltpu.VMEM((GROUP_W, ctx + seq), BF16)]
                       + [pltpu.VMEM((ctx + seq, GROUP_W), BF16)] * 2
                       + [pltpu.VMEM((lag, tq, ctx + seq), BF16), pltpu.VMEM((tq, GROUP_W), F32)],
        compiler_params=_cparams("parallel"),
        name="gqa_mixer",
    )(p_gqa, p_gqa, cos_t, sin_t, gq, gk)
    return yl, yc


def _diff_kernel(xl_ref, xc_ref, cos_ref, sin_ref, gq_ref, gk_ref, lam_ref, sub_ref, yl_ref, yc_ref,
                 q_s, k_s, va_s, vb_s, p_s, acc_s, *, seq, ctx, tq, unroll, lam_init):
    ones_sub = _group_ones(GROUP_W, DIFF_SUB)
    ones_head = _group_ones(GROUP_W, HEAD_DIM)
    scale = DIFF_SUB ** -0.5 * LOG2E
    lp = lam_ref[...]
    lam = (jnp.exp(jnp.sum(lp[0:1] * lp[1:2], axis=-1, keepdims=True))
           - jnp.exp(jnp.sum(lp[2:3] * lp[3:4], axis=-1, keepdims=True)) + lam_init)

    def prep(x_ref, rows, dst, rope_rows):
        q = _group_rmsnorm(x_ref[rows, 0:GROUP_W].astype(F32), gq_ref[...], ones_sub, DIFF_SUB)
        k = _group_rmsnorm(x_ref[rows, GROUP_W:2 * GROUP_W].astype(F32), gk_ref[...], ones_sub, DIFF_SUB)
        if rope_rows is not None:
            cos_t, sin_t = cos_ref[rope_rows, :], sin_ref[rope_rows, :]
            q, k = _rope(q, cos_t, sin_t, DIFF_SUB // 2), _rope(k, cos_t, sin_t, DIFF_SUB // 2)
        q_s[dst, :] = (q * scale).astype(BF16)
        k_s[:, dst] = jnp.transpose(k).astype(BF16)
        va_s[dst, :], vb_s[dst, :] = _ones_lane_variants(x_ref[rows, 2 * GROUP_W:3 * GROUP_W])

    prep(xc_ref, pl.ds(0, ctx), pl.ds(0, ctx), None)

    def prep_body(i, carry):
        r = pl.ds(pl.multiple_of(i * tq, tq), tq)
        prep(xl_ref, r, pl.ds(pl.multiple_of(ctx + i * tq, tq), tq), r)
        return carry

    lax.fori_loop(0, seq // tq, prep_body, 0)

    stages = []
    for h in range(N_HEADS):
        head = _head_mask((1, GROUP_W), h, HEAD_DIM, F32)
        for j, weight in enumerate((head, -lam * head)):
            stages.append((_head_mask((1, GROUP_W), 2 * h + j, DIFF_SUB, BF16), *_head_v(h, va_s, vb_s), weight))
    finish = lambda acc: _group_rmsnorm(acc, sub_ref[...], ones_head, HEAD_DIM) * (1.0 - lam_init)
    yc_ref[...] = _context_attention(q_s[0:ctx, :], k_s[:, 0:ctx], stages, finish).astype(yc_ref.dtype)
    _attention_stages(q_s, k_s, yl_ref, p_s, acc_s, stages, finish, ctx=ctx, seq=seq, tq=tq, unroll=unroll)


def _diff_mixer(p_df, cos_t, sin_t, gq, gk, lam_p, sub_g, *, lam_init, n_batch, seq, ctx):
    tq, unroll, lag = (DIFF_TILING[k] for k in ("tq", "unroll", "lag"))
    lat_blocks = n_batch * seq // ctx
    kern = functools.partial(_diff_kernel, seq=seq, ctx=ctx, tq=tq, unroll=unroll, lam_init=lam_init)
    yl, yc = pl.pallas_call(
        kern,
        grid=(n_batch,),
        in_specs=[pl.BlockSpec((seq, IN_DIFF), lambda b: (b, 0)),
                  pl.BlockSpec((ctx, IN_DIFF), lambda b: (lat_blocks + b, 0)),
                  _resident(cos_t.shape), _resident(sin_t.shape),
                  _resident((1, GROUP_W)), _resident((1, GROUP_W)),
                  _resident(lam_p.shape), _resident((1, GROUP_W))],
        out_specs=[pl.BlockSpec((seq, GROUP_W), lambda b: (b, 0)),
                   pl.BlockSpec((ctx, GROUP_W), lambda b: (b, 0))],
        out_shape=[jax.ShapeDtypeStruct((n_batch * seq, GROUP_W), BF16),
                   jax.ShapeDtypeStruct((n_batch * ctx, GROUP_W), BF16)],
        scratch_shapes=[pltpu.VMEM((ctx + seq, GROUP_W), BF16), pltpu.VMEM((GROUP_W, ctx + seq), BF16)]
                       + [pltpu.VMEM((ctx + seq, GROUP_W), BF16)] * 2
                       + [pltpu.VMEM((lag, tq, ctx + seq), BF16), pltpu.VMEM((tq, GROUP_W), F32)],
        compiler_params=_cparams("parallel"),
        name="diff_attention_mixer",
    )(p_df, p_df, cos_t, sin_t, gq, gk, lam_p, sub_g)
    return yl, yc


def _dot_split2(x, w_bf16):
    hi = x.astype(BF16)
    lo = (x - hi.astype(F32)).astype(BF16)
    return _dot(hi, w_bf16) + _dot(lo, w_bf16)


def _mlstm_kernel(xl_ref, xc_ref, gl_ref, gc_ref, cw_ref, cb_ref, gb_ref, ng_ref, yl_ref, yc_ref,
                  q_s, k_s, hsum, c_s, n_s, m_s, *, seq, ctx):
    t = ML_CHUNK
    n_chunks = seq // t
    ones_head = _group_ones(GROUP_W, HEAD_DIM)
    block_diag = ones_head.astype(F32)
    r_i = lax.broadcasted_iota(jnp.int32, (t, t), 0)
    c_i = lax.broadcasted_iota(jnp.int32, (t, t), 1)
    tril_b = jnp.where(c_i <= r_i, 1.0, 0.0).astype(BF16)
    causal = (c_i <= r_i, c_i >= r_i)
    row_g = lax.broadcasted_iota(jnp.int32, (t, IN_GATE), 0)
    sel_r = lax.broadcasted_iota(jnp.int32, (IN_GATE, GROUP_W), 0)
    sel_c = lax.broadcasted_iota(jnp.int32, (IN_GATE, GROUP_W), 1) // HEAD_DIM
    select = [jnp.where(sel_r == 8 * d + sel_c, 1.0, 0.0).astype(BF16) for d in range(2)]
    q_masks = [_head_mask((1, GROUP_W), h, HEAD_DIM, BF16) for h in range(N_HEADS)]

    for x_ref, rows, off in ((xc_ref, ctx, 0), (xl_ref, seq, ctx)):
        qk = _silu(_dwconv3(x_ref[:, 0:2 * GROUP_W].astype(F32), cw_ref[...], cb_ref[...]))
        q_s[off:off + rows, :] = (qk[:, 0:GROUP_W] * (HEAD_DIM ** -0.5)).astype(BF16)
        k_s[off:off + rows, :] = qk[:, GROUP_W:2 * GROUP_W].astype(BF16)

    hsum[...] = jnp.zeros_like(hsum)
    c_s[...] = jnp.zeros_like(c_s)
    n_s[...] = jnp.zeros_like(n_s)
    m_s[...] = jnp.zeros_like(m_s)

    def running_max(x, d):
        step = 1
        while step < t:
            if d == 0:
                shifted, valid = pltpu.roll(x, step, 0), row_g >= step
            else:
                shifted, valid = pltpu.roll(x, t - step, 0), row_g < t - step
            x = jnp.maximum(x, jnp.where(valid, shifted, -jnp.inf))
            step *= 2
        return x

    def chunk(d, rows, v, g_raw):
        g = g_raw + gb_ref[...]
        logf = pltpu.roll(jax.nn.log_sigmoid(g), IN_GATE - N_HEADS, 1)
        pre = _dot_exact_rhs(tril_b, logf)
        tot = pre[t - 1:t, :]
        cum = pre if d == 0 else tot - pre + logf
        r = g - cum
        m_prev = m_s[d]
        log_inter = cum + m_prev
        m_t = jnp.maximum(log_inter, cum + running_max(r, d))
        a_t = cum - m_t
        log_g = (tot - cum) + g
        m_new = jnp.maximum(tot + m_prev, jnp.max(log_g, axis=0, keepdims=True))
        m_s[d] = m_new
        rowwise = jnp.concatenate([jnp.exp(log_inter - m_t), jnp.exp(-m_t), jnp.exp(log_g - m_new),
                                   jnp.broadcast_to(jnp.exp(tot + m_prev - m_new), (8, IN_GATE))], axis=0)
        full = _dot_split2(rowwise, select[d])
        w_inter, e_m, w_g, w_c = full[0:t], full[t:2 * t], full[2 * t:3 * t], full[3 * t:3 * t + 1]

        q, k = q_s[rows, :], k_s[rows, :]
        r_t = jnp.transpose(r * LOG2E)
        a_2 = a_t * LOG2E
        pair = [None, None]
        for h in range(N_HEADS):
            col = 8 * d + h
            w_intra = jnp.exp2(jnp.where(causal[d], a_2[:, col:col + 1] + r_t[col:col + 1, :], -jnp.inf))
            s = (_dot_nt(q * q_masks[h], k) * w_intra).astype(BF16)
            o = _dot(s, v * q_masks[h] + q_masks[(h + 2) % N_HEADS])
            pair[h // 2] = o if pair[h // 2] is None else pair[h // 2] + o
        half = GROUP_W // 2
        num_i = jnp.concatenate([pair[0][:, :half], pair[1][:, half:]], axis=1)
        den_i = jnp.concatenate([pair[0][:, half:], pair[1][:, :half]], axis=1)
        ct, n_full = c_s[d], n_s[d]
        inter = _dot(q, ct.astype(BF16))
        qn = _dot((q.astype(F32) * n_full).astype(BF16), ones_head)
        num = w_inter * inter + num_i
        den = w_inter * qn + den_i
        kw = k.astype(F32) * w_g
        c_s[d] = w_c * ct + _dot_tn(kw.astype(BF16), v) * block_diag
        n_s[d] = w_c * n_full + jnp.sum(kw, axis=0, keepdims=True)
        return num / jnp.maximum(jnp.abs(den), e_m)

    assert ctx == t
    v_c = xc_ref[:, 2 * GROUP_W:3 * GROUP_W]
    hsum[0:t, :] = chunk(0, pl.ds(0, t), v_c, gc_ref[...]) + chunk(1, pl.ds(0, t), v_c, gc_ref[...])

    def body(j, carry):
        for d in range(2):
            cidx = j if d == 0 else n_chunks - 1 - j
            lat = pl.ds(pl.multiple_of(cidx * t, t), t)
            rows = pl.ds(pl.multiple_of(ctx + cidx * t, t), t)
            hsum[rows, :] += chunk(d, rows, xl_ref[lat, 2 * GROUP_W:3 * GROUP_W], gl_ref[lat, :])
        return carry

    lax.fori_loop(0, n_chunks, body, 0, unroll=True)

    for x_ref, y_ref, rows, off in ((xc_ref, yc_ref, ctx, 0), (xl_ref, yl_ref, seq, ctx)):
        hn = _group_rmsnorm(hsum[off:off + rows, :], ng_ref[...], ones_head, HEAD_DIM)
        y_ref[...] = (hn * jax.nn.sigmoid(x_ref[:, 3 * GROUP_W:4 * GROUP_W].astype(F32))).astype(y_ref.dtype)


def _mlstm_mixer(p_ml, p_gate, conv_w, conv_b, gate_b, norm_g, *, n_batch, seq, ctx):
    lat_blocks = n_batch * seq // ctx
    kern = functools.partial(_mlstm_kernel, seq=seq, ctx=ctx)
    yl, yc = pl.pallas_call(
        kern,
        grid=(n_batch,),
        in_specs=[pl.BlockSpec((seq, IN_ML), lambda b: (b, 0)),
                  pl.BlockSpec((ctx, IN_ML), lambda b: (lat_blocks + b, 0)),
                  pl.BlockSpec((seq, IN_GATE), lambda b: (b, 0)),
                  pl.BlockSpec((ctx, IN_GATE), lambda b: (lat_blocks + b, 0)),
                  _resident(conv_w.shape), _resident(conv_b.shape), _resident(gate_b.shape), _resident(norm_g.shape)],
        out_specs=[pl.BlockSpec((seq, GROUP_W), lambda b: (b, 0)),
                   pl.BlockSpec((ctx, GROUP_W), lambda b: (b, 0))],
        out_shape=[jax.ShapeDtypeStruct((n_batch * seq, GROUP_W), BF16),
                   jax.ShapeDtypeStruct((n_batch * ctx, GROUP_W), BF16)],
        scratch_shapes=[pltpu.VMEM((ctx + seq, GROUP_W), BF16), pltpu.VMEM((ctx + seq, GROUP_W), BF16),
                        pltpu.VMEM((ctx + seq, GROUP_W), F32),
                        pltpu.VMEM((2, GROUP_W, GROUP_W), F32), pltpu.VMEM((2, 1, GROUP_W), F32),
                        pltpu.VMEM((2, 1, IN_GATE), F32)],
        compiler_params=_cparams("parallel"),
        name="mlstm_mixer",
    )(p_ml, p_ml, p_gate, p_gate, conv_w, conv_b, gate_b, norm_g)
    return yl, yc


def _phase_tables(idx_a, idx_b, n):
    ang = (2.0 * math.pi / n) * ((idx_a[:, None] * idx_b[None, :]) % n).astype(F32)
    return jnp.cos(ang), jnp.sin(ang)


def _dft_matrix(length, n):
    nf, lo = n // 2, 64
    k = jnp.arange(nf, dtype=jnp.int32)
    k1, k0 = lo * jnp.arange(nf // lo, dtype=jnp.int32), jnp.arange(lo, dtype=jnp.int32)
    j = jnp.arange(length, dtype=jnp.int32)
    (ca, sa), (cb, sb) = _phase_tables(k1, j, n), _phase_tables(k0, j, n)
    cos_f = (ca[:, None, :] * cb[None, :, :] - sa[:, None, :] * sb[None, :, :]).reshape(nf, length)
    sin_f = (sa[:, None, :] * cb[None, :, :] + ca[:, None, :] * sb[None, :, :]).reshape(nf, length)
    alt = jnp.where(j % 2 == 0, 1.0, -1.0).astype(F32)
    return jnp.concatenate([cos_f, jnp.where(k[:, None] == 0, alt[None, :], sin_f)], axis=0).astype(BF16)


def _hyena_filter_consts(length, n):
    t = jnp.arange(length, dtype=F32)
    tn = t / length
    bands = jnp.arange(1, HY_POS_BANDS + 1, dtype=F32)
    ang = 2.0 * math.pi * tn[:, None] * bands
    feats = jnp.concatenate([tn[:, None], jnp.cos(ang), jnp.sin(ang)], axis=-1)
    feats = jnp.pad(feats, ((0, 0), (0, 128 - feats.shape[1])))
    dist = jnp.abs(t - length // 2) / (length / 2)
    deltas = jnp.abs(jnp.linspace(math.log(HY_DECAY_TARGET) / HY_SLOW_DECAY,
                                  math.log(HY_DECAY_TARGET) / HY_FAST_DECAY, GROUP_W, dtype=F32))
    window = jnp.exp(-dist[:, None] * jnp.tile(deltas, 2))
    k = jnp.arange(n // 2, dtype=jnp.int32)
    phi = (2.0 * math.pi / n) * ((k * (length // 2)) % n).astype(F32)
    shift = jnp.stack([jnp.cos(phi), jnp.sin(phi)], axis=-1) * (2.0 / n)
    return feats, window, shift


def _filter_kernel(feats_ref, win_ref, shift_ref, w1lo_ref, w1hi_ref, b1_ref, w2_ref, b2_ref, w3t_ref, w3b_ref, b3_ref,
                   f_ref, al_ref, be_ref, ga_ref, hid_s, *, nyquist_sign):
    half = feats_ref.shape[0] // 2

    @pl.when(pl.program_id(1) == 0)
    def _():
        pre = (_dot_f32ish(feats_ref[0:half, :], w1lo_ref[0]) + _dot_f32ish(feats_ref[half:2 * half, :], w1hi_ref[0]))
        h = jnp.sin(pre + b1_ref[0])
        hid_s[...] = jnp.sin(_dot_f32ish(h, w2_ref[0]) + b2_ref[0])

    hid = hid_s[...]
    h = jnp.concatenate([_dot_f32ish(hid, w3t_ref[0]), _dot_f32ish(hid, w3b_ref[0])], axis=0)
    h = (h + b3_ref[0]) * win_ref[...]
    h = h / jnp.sum(jnp.abs(h), axis=0, keepdims=True)
    h1 = h.astype(BF16)
    h2 = (h - h1.astype(F32)).astype(BF16)
    nf = f_ref.shape[0] // 2
    step = math.gcd(nf, 512)
    for r in range(0, nf, step):
        fc, fs = f_ref[r:r + step, :], f_ref[nf + r:nf + r + step, :]
        p = _dot(fc, h1) + _dot(fc, h2)
        q = _dot(fs, h1) + _dot(fs, h2)
        cs, sn = shift_ref[r:r + step, 0:1], shift_ref[r:r + step, 1:2]
        alpha = p * cs + q * sn
        beta = p * sn - q * cs
        if r == 0:
            first = lax.broadcasted_iota(jnp.int32, p.shape, 0) == 0
            al_ref[0, r:r + step, :] = jnp.where(first, 0.5 * alpha, alpha)
            be_ref[0, r:r + step, :] = jnp.where(first, 0.0, beta)
            ga_ref[0, r:r + step, :] = jnp.where(first, (0.5 * nyquist_sign) * q * cs, alpha)
        else:
            al_ref[0, r:r + step, :] = alpha
            be_ref[0, r:r + step, :] = beta
            ga_ref[0, r:r + step, :] = alpha


def _hyena_filter_spectra(length, fwd, w1, b1, w2, b2, w3, b3):
    depth, hid = w1.shape[0], w1.shape[2]
    n = fwd.shape[0]
    feats, window, shift = _hyena_filter_consts(length, n)
    nf = n // 2
    place = lambda a, r0, c0, rows, cols: jnp.pad(a, ((0, 0), (r0, rows - r0 - a.shape[1]), (c0, cols - c0 - a.shape[2])))
    w1lo, w1hi = place(w1, 0, 0, 128, 128), place(w1, 0, hid, 128, 128)
    w2bd = place(w2, 0, 0, 128, 128) + place(w2, hid, hid, 128, 128)
    w3t, w3b = place(w3, 0, 0, 128, w3.shape[2]), place(w3, hid, 0, 128, w3.shape[2])
    b1p, b2p = jnp.tile(b1, (1, 2))[:, None, :], jnp.tile(b2, (1, 2))[:, None, :]
    b3p = b3[:, None, :]
    lay = lambda a: pl.BlockSpec((1,) + a.shape[1:], lambda l, o: (l, 0, 0))
    per_order = lambda rows: pl.BlockSpec((1, rows, GROUP_W), lambda l, o: (l, 0, o))
    out = jax.ShapeDtypeStruct((depth, nf, 2 * GROUP_W), F32)
    kern = functools.partial(_filter_kernel, nyquist_sign=1.0 if (length // 2) % 2 == 0 else -1.0)
    return pl.pallas_call(
        kern,
        grid=(depth, 2),
        in_specs=[_resident(feats.shape), pl.BlockSpec((length, GROUP_W), lambda l, o: (0, o)), _resident(shift.shape),
                  lay(w1lo), lay(w1hi), lay(b1p), lay(w2bd), lay(b2p), per_order(128), per_order(128), per_order(1),
                  _resident(fwd.shape)],
        out_specs=[per_order(nf)] * 3,
        out_shape=[out, out, out],
        scratch_shapes=[pltpu.VMEM((length // 2, 128), F32)],
        compiler_params=_cparams("parallel", "arbitrary"),
        name="hyena_filter_spectra",
    )(feats, window, shift, w1lo, w1hi, b1p, w2bd, b2p, w3t, w3b, b3p, fwd)


def _hyena_conv_kernel(sig_ref, gate_ref, cws_ref, cbs_ref, cwg_ref, cbg_ref, skip_ref, f_ref,
                       al_ref, be_ref, ga_ref, o_ref, *, sig_conv):
    s = sig_ref[...].astype(F32)
    if sig_conv:
        s = _dwconv3(s, cws_ref[...], cbs_ref[...])
    gate = _dwconv3(gate_ref[...].astype(F32), cwg_ref[...], cbg_ref[...])
    sb = s.astype(BF16)
    nf = f_ref.shape[0] // 2
    a = _dot(f_ref[0:nf, :], sb)
    b = _dot(f_ref[nf:2 * nf, :], sb)
    ay = a * al_ref[0] + b * be_ref[0]
    by = b * ga_ref[0] - a * be_ref[0]
    y = _dot_tn(f_ref[0:nf, :], ay.astype(BF16)) + _dot_tn(f_ref[nf:2 * nf, :], by.astype(BF16))
    o_ref[...] = (gate * (y + skip_ref[...] * s)).astype(o_ref.dtype)


def _hyena_conv(sig, sig_block, sig_col, p_hy, gate_col, conv_w, conv_b, skip, fwd, spectra, order, layer,
                *, rows, first_block, n_batch, out_dtype, sig_conv):
    cw = lambda col: pl.BlockSpec((3, GROUP_W), lambda b: (0, col))
    cb = lambda col: pl.BlockSpec((1, GROUP_W), lambda b: (0, col))
    spec = pl.BlockSpec((1, fwd.shape[0] // 2, GROUP_W), lambda b: (layer, 0, order), pipeline_mode=pl.Buffered(1))
    kern = functools.partial(_hyena_conv_kernel, sig_conv=sig_conv)
    return pl.pallas_call(
        kern,
        grid=(n_batch,),
        in_specs=[pl.BlockSpec((rows, GROUP_W), lambda b: (sig_block + b, sig_col)),
                  pl.BlockSpec((rows, GROUP_W), lambda b: (first_block + b, gate_col)),
                  cw(sig_col), cb(sig_col), cw(gate_col), cb(gate_col),
                  pl.BlockSpec((1, GROUP_W), lambda b: (0, 0)),
                  _resident(fwd.shape), spec, spec, spec],
        out_specs=pl.BlockSpec((rows, GROUP_W), lambda b: (b, 0)),
        out_shape=jax.ShapeDtypeStruct((n_batch * rows, GROUP_W), out_dtype),
        compiler_params=_cparams("parallel"),
        name="hyena_long_conv",
    )(sig, p_hy, conv_w, conv_b, conv_w, conv_b, skip, fwd, *spectra)


def _hyena_mixer(p_hy, conv_w, conv_b, skip, fwd, spectra, layer, *, rows, first_block, n_batch):
    common = dict(rows=rows, first_block=first_block, n_batch=n_batch)
    z = _hyena_conv(p_hy, first_block, 0, p_hy, 1, conv_w, conv_b, skip[0:1], fwd, spectra, 0, layer,
                    out_dtype=F32, sig_conv=True, **common)
    return _hyena_conv(z, 0, 0, p_hy, 2, conv_w, conv_b, skip[1:2], fwd, spectra, 1, layer,
                       out_dtype=BF16, sig_conv=False, **common)


def _rope_tables(length, dim, reps):
    rows = length // GRID_W
    row = jnp.repeat(jnp.arange(rows), GRID_W).astype(F32)
    col = jnp.tile(jnp.arange(GRID_W), rows).astype(F32)
    n_freq = dim // 4
    inv = ROPE_THETA ** (-jnp.arange(n_freq, dtype=F32) / n_freq)
    ang = jnp.concatenate([row[:, None] * inv, col[:, None] * inv], axis=-1)
    cos, sin = jnp.cos(ang), jnp.sin(ang)
    return (jnp.tile(jnp.concatenate([cos, cos], axis=-1), (1, reps)),
            jnp.tile(jnp.concatenate([-sin, sin], axis=-1), (1, reps)))


def kernel(x, c, ctx, c_ctx, ada_w, ada_b, norm_g, ffn_w13, ffn_w2, w_in, w_out, ml_gate_b, ml_conv_w, ml_conv_b,
           ml_norm_g, gqa_qk_g, hy_conv_w, hy_conv_b, hy_filt_w1, hy_filt_b1, hy_filt_w2, hy_filt_b2, hy_filt_w3,
           hy_filt_b3, hy_skip, diff_qk_g, diff_lambda, diff_subln_g):
    n_batch, seq, d = x.shape
    n_ctx = ctx.shape[1]
    depth = ada_w.shape[0]
    n_lat = n_batch * seq
    n_all = n_lat + n_batch * n_ctx
    assert n_batch * n_ctx == seq and n_ctx == ML_CHUNK and seq % ROW_TILE == 0

    c_all = jnp.pad(jnp.concatenate([c, c_ctx[None]], axis=0), ((0, 16 - n_batch - 1), (0, 0)))
    mod = _modulation(c_all, ada_w, ada_b)[:, :n_batch + 1].reshape(depth, n_batch + 1, N_MOD, d)

    w13 = ffn_w13.astype(BF16)
    w2 = ffn_w2.astype(BF16)
    gqa_rows = GROUP_W + jnp.concatenate([jnp.arange(HEAD_DIM) + h * HEAD_DIM for h in GQA_OUT_HEADS])
    w_out_b = w_out.at[:, GROUP_W:2 * GROUP_W].set(w_out[:, gqa_rows]).astype(BF16)

    rope_gqa = _rope_tables(seq, HEAD_DIM, N_HEADS)
    rope_diff = _rope_tables(seq, DIFF_SUB, 2 * N_HEADS)
    dft_l = _dft_matrix(seq, 3 * seq // 2)
    dft_c = _dft_matrix(n_ctx, 2 * n_ctx)
    filt = (hy_filt_w1, hy_filt_b1, hy_filt_w2, hy_filt_b2, hy_filt_w3, hy_filt_b3)
    spec_l = _hyena_filter_spectra(seq, dft_l, *filt)
    spec_c = _hyena_filter_spectra(n_ctx, dft_c, *filt)

    h = (x.reshape(n_lat, d), ctx.reshape(n_batch * n_ctx, d))
    dims = dict(seq=seq, n_batch=n_batch)
    for l in range(depth):
        need_ctx = l < depth - 1
        lam_init = 0.8 - 0.6 * math.exp(-0.3 * l)
        h = _ffn_half(h, mod[l], norm_g[l, 0], w13, w2, layer=l, half=0, base=0, n_rows=n_all, **dims)
        p_ml, p_gqa, p_hy, p_df, p_gate = _in_projection(h, mod[l], norm_g[l, 1], w_in, layer=l, **dims)

        gate_b = jnp.pad(ml_gate_b[l], (0, IN_GATE - ml_gate_b.shape[1]))[None]
        y_ml = _mlstm_mixer(p_ml, p_gate, ml_conv_w[l], ml_conv_b[l][None], gate_b, ml_norm_g[l][None],
                            n_batch=n_batch, seq=seq, ctx=n_ctx)
        tile_g = lambda g, reps: jnp.tile(g, reps)[None]
        y_gqa = _gqa_mixer(p_gqa, *rope_gqa, tile_g(gqa_qk_g[l, 0], N_HEADS), tile_g(gqa_qk_g[l, 1], N_HEADS),
                           n_batch=n_batch, seq=seq, ctx=n_ctx)
        y_df = _diff_mixer(p_df, *rope_diff, tile_g(diff_qk_g[l, 0], 2 * N_HEADS), tile_g(diff_qk_g[l, 1], 2 * N_HEADS),
                           diff_lambda[l], tile_g(diff_subln_g[l], N_HEADS),
                           lam_init=lam_init, n_batch=n_batch, seq=seq, ctx=n_ctx)
        y_hy = _hyena_mixer(p_hy, hy_conv_w[l], hy_conv_b[l][None], hy_skip[l], dft_l, spec_l, l,
                            rows=seq, first_block=0, n_batch=n_batch)
        y_hy_c = y_hy if not need_ctx else _hyena_mixer(
            p_hy, hy_conv_w[l], hy_conv_b[l][None], hy_skip[l], dft_c, spec_c, l,
            rows=n_ctx, first_block=n_lat // n_ctx, n_batch=n_batch)
        h = _ffn_half(h, mod[l], norm_g[l, 2], w13, w2, layer=l, half=1, base=6,
                      n_rows=n_all if need_ctx else n_lat,
                      mix=[y_ml, y_gqa, (y_hy, y_hy_c), y_df], w_out=w_out_b, **dims)
    return h[:n_lat].reshape(n_batch, seq, d)
```
